```python
import math
import jax
import jax.numpy as jnp
from jax import lax
import numpy as np

D_MODEL = 1024
BATCH = 2
SEQ = 8192
DEPTH = 1
DEC_BATCH = 128
DEC_SEQ = 4
PAST_LEN = 8192
PAGE_SIZE = 128

RMS_EPS = 1e-6
NEG_INF = -1e30
FOX_HEADS = 8
FOX_HD = 64
FOX_W = FOX_HEADS * FOX_HD
FOX_SCALE = FOX_HD ** -0.5
Q_BLOCK = 128
FOX_IN = 3 * FOX_W + FOX_HEADS
RW_HEADS = 8
RW_HD = 64
RW_W = RW_HEADS * RW_HD
RW_W_RANK = 64
RW_A_RANK = 64
RW_G_RANK = 128
RW_IN = 3 * RW_W + RW_W_RANK + RW_A_RANK + RW_G_RANK
RW_DECAY = math.exp(-0.5)
GN_EPS = 64e-5
MEM_TOKENS = 256
MEM_HEADS = 4
MEM_HD = 128
MEM_W = MEM_HEADS * MEM_HD
MEM_SCALE = MEM_HD ** -0.5
N_BRANCH = 3
BRANCH_W = 512
D_IN = FOX_IN + RW_IN + MEM_W + N_BRANCH * D_MODEL
PEER_HEADS = 8
PEER_NK = 128
PEER_EXPERTS = PEER_NK * PEER_NK
PEER_QDIM = 256
PEER_HALF = PEER_QDIM // 2
PEER_TOPK = 16
PEER_BLOCK = 128

kernel_name = 'fox_rwkv7_memxattn_peer_hybrid_step'


def rms_norm(x, g):
    xf = x.astype(jnp.float32)
    y = xf * lax.rsqrt(jnp.mean(xf * xf, axis=-1, keepdims=True) + RMS_EPS)
    return (y * g.astype(jnp.float32)).astype(x.dtype)


def split_cols(p, widths):
    offs = [int(o) for o in np.cumsum(widths)[:-1]]
    return jnp.split(p, offs, axis=-1)


def heads(x, n, d):
    return x.reshape(x.shape[:-1] + (n, d))


def project_in(xn, w_in):
    p = jnp.einsum('btd,de->bte', xn, w_in)
    return split_cols(p, [FOX_IN, RW_IN, MEM_W, N_BRANCH * D_MODEL])


def fox_prepare(p_fox, b_f, qn_w, kn_w):
    q, k, v, f = split_cols(p_fox, [FOX_W, FOX_W, FOX_W, FOX_HEADS])
    q = rms_norm(heads(q, FOX_HEADS, FOX_HD), qn_w)
    k = rms_norm(heads(k, FOX_HEADS, FOX_HD), kn_w)
    v = heads(v, FOX_HEADS, FOX_HD)
    logf = jax.nn.log_sigmoid(f.astype(jnp.float32) + b_f.astype(jnp.float32))
    return q, k, v, logf


def fox_logits(q, k, c_q, c_k, q_pos, k_pos):
    s = jnp.einsum('bqhd,bkhd->bhqk', q, k, preferred_element_type=jnp.float32) * FOX_SCALE
    bias = jnp.swapaxes(c_q, 1, 2)[..., :, None] - jnp.swapaxes(c_k, 1, 2)[..., None, :]
    return jnp.where(k_pos[None, :] <= q_pos[:, None], s + bias, NEG_INF)


def fox_prompt(q, k, v, logf):
    b, s = q.shape[:2]
    c = jnp.cumsum(logf, axis=1)
    k_pos = jnp.arange(s)

    def block(i):
        start = i * Q_BLOCK
        qb = lax.dynamic_slice_in_dim(q, start, Q_BLOCK, axis=1)
        cb = lax.dynamic_slice_in_dim(c, start, Q_BLOCK, axis=1)
        p = jax.nn.softmax(fox_logits(qb, k, cb, c, start + jnp.arange(Q_BLOCK), k_pos), axis=-1)
        return jnp.einsum('bhqk,bkhd->bqhd', p.astype(v.dtype), v)

    o = lax.map(block, jnp.arange(s // Q_BLOCK))
    return jnp.swapaxes(o, 0, 1).reshape(b, s, FOX_W)


def fox_sample(q, k, v, logf, k_past, v_past, logf_past):
    b, t = q.shape[:2]
    past = k_past.shape[1]
    c = jnp.cumsum(jnp.concatenate([logf_past.astype(jnp.float32), logf], axis=1), axis=1)
    c_past, c_new = c[:, :past], c[:, past:]
    q_pos = past + jnp.arange(t)
    l_past = fox_logits(q, k_past, c_new, c_past, q_pos, jnp.arange(past))
    l_new = fox_logits(q, k, c_new, c_new, q_pos, q_pos)
    p = jax.nn.softmax(jnp.concatenate([l_past, l_new], axis=-1), axis=-1)
    o = (jnp.einsum('bhqk,bkhd->bqhd', p[..., :past].astype(v_past.dtype), v_past)
         + jnp.einsum('bhqk,bkhd->bqhd', p[..., past:].astype(v.dtype), v))
    return o.reshape(b, t, FOX_W).astype(v.dtype)


def rwkv_branch(p_rw, prev_row, state, w):
    f32 = jnp.float32
    prev = jnp.concatenate([prev_row[:, None, :].astype(p_rw.dtype), p_rw[:, :-1]], axis=1)
    xs = p_rw + (prev - p_rw) * w['rw_mu']
    r, k, v, zw, za, zg = split_cols(xs, [RW_W, RW_W, RW_W, RW_W_RANK, RW_A_RANK, RW_G_RANK])
    log_w = -RW_DECAY * jax.nn.sigmoid((w['rw_w0'] + jnp.tanh(zw) @ w['rw_w_up']).astype(f32))
    a = jax.nn.sigmoid((w['rw_a0'] + za @ w['rw_a_up']).astype(f32))
    g = jax.nn.sigmoid(zg) @ w['rw_g_up']
    kk = heads((k * w['rw_k_k']).astype(f32), RW_HEADS, RW_HD)
    kk = kk / jnp.maximum(jnp.sqrt(jnp.sum(kk * kk, axis=-1, keepdims=True)), 1e-12)
    k = k.astype(f32) * (1.0 + (a - 1.0) * w['rw_k_a'].astype(f32))
    r_h = heads(r.astype(f32), RW_HEADS, RW_HD)
    k_h = heads(k, RW_HEADS, RW_HD)
    v_h = heads(v.astype(f32), RW_HEADS, RW_HD)
    a_h = heads(a, RW_HEADS, RW_HD)
    decay = jnp.exp(heads(log_w, RW_HEADS, RW_HD))

    def step(s_mat, inp):
        r_t, w_t, k_t, v_t, kk_t, a_t = inp
        sa = jnp.einsum('bhij,bhj->bhi', s_mat, kk_t)
        s_mat = (s_mat * w_t[:, :, None, :] - sa[..., None] * (kk_t * a_t)[:, :, None, :]
                 + v_t[..., None] * k_t[:, :, None, :])
        return s_mat, jnp.einsum('bhij,bhj->bhi', s_mat, r_t)

    seq = [jnp.swapaxes(z, 0, 1) for z in (r_h, decay, k_h, v_h, kk, a_h)]
    s_fin, o = lax.scan(step, state.astype(f32), seq)
    o = jnp.swapaxes(o, 0, 1)
    mean = jnp.mean(o, axis=-1, keepdims=True)
    var = jnp.mean(jnp.square(o - mean), axis=-1, keepdims=True)
    bt = o.shape[:2]
    o_gn = ((o - mean) * lax.rsqrt(var + GN_EPS)).reshape(bt + (RW_W,))
    o_gn = o_gn * w['rw_ln_w'].astype(f32) + w['rw_ln_b'].astype(f32)
    bonus = jnp.sum(r_h * k_h * w['rw_r_k'].astype(f32), axis=-1, keepdims=True) * v_h
    out = (o_gn + bonus.reshape(bt + (RW_W,))) * g.astype(f32)
    return out.astype(p_rw.dtype), s_fin.astype(p_rw.dtype), p_rw[:, -1]


def mem_kv(mem, w):
    kv = jnp.einsum('bmd,de->bme', rms_norm(mem, w['mem_norm_w']), w['w_mem_kv'])
    k, v = split_cols(kv, [MEM_W, MEM_W])
    return rms_norm(heads(k, MEM_HEADS, MEM_HD), w['mem_kn_w']), heads(v, MEM_HEADS, MEM_HD)


def mem_attend(p_mq, mk, mv, qn_w):
    q = rms_norm(heads(p_mq, MEM_HEADS, MEM_HD), qn_w)
    s = jnp.einsum('bqhd,bkhd->bhqk', q, mk, preferred_element_type=jnp.float32) * MEM_SCALE
    p = jax.nn.softmax(s, axis=-1)
    o = jnp.einsum('bhqk,bkhd->bqhd', p.astype(mv.dtype), mv)
    return o.reshape(o.shape[:2] + (MEM_W,)).astype(p_mq.dtype)


def peer(xn, w):
    lead = xn.shape[:-1]
    xf = xn.reshape(-1, D_MODEL)
    n_tok = xf.shape[0]
    n_blk = -(-n_tok // PEER_BLOCK)
    xf = jnp.pad(xf, ((0, n_blk * PEER_BLOCK - n_tok), (0, 0))).reshape(n_blk, PEER_BLOCK, D_MODEL)

    def block(xb):
        q = (xb @ w['peer_w_q']).reshape(PEER_BLOCK, PEER_HEADS, PEER_QDIM)
        s1 = jnp.einsum('thc,hkc->thk', q[..., :PEER_HALF], w['peer_keys1'], preferred_element_type=jnp.float32)
        s2 = jnp.einsum('thc,hkc->thk', q[..., PEER_HALF:], w['peer_keys2'], preferred_element_type=jnp.float32)
        v1, i1 = lax.top_k(s1, PEER_TOPK)
        v2, i2 = lax.top_k(s2, PEER_TOPK)
        cand = (v1[..., :, None] + v2[..., None, :]).reshape(PEER_BLOCK, PEER_HEADS, PEER_TOPK * PEER_TOPK)
        cidx = (i1[..., :, None] * PEER_NK + i2[..., None, :]).reshape(PEER_BLOCK, PEER_HEADS, PEER_TOPK * PEER_TOPK)
        sc, pos = lax.top_k(cand, PEER_TOPK)
        eidx = jnp.take_along_axis(cidx, pos, axis=-1)
        gate = jax.nn.softmax(sc, axis=-1)
        u = jnp.take(w['peer_u'], eidx, axis=0)
        act = jax.nn.gelu(jnp.einsum('td,thkd->thk', xb, u, preferred_element_type=jnp.float32))
        vv = jnp.take(w['peer_v'], eidx, axis=0)
        return jnp.einsum('thk,thkd->td', (gate * act).astype(vv.dtype), vv)

    out = lax.map(block, xf).reshape(-1, D_MODEL)[:n_tok]
    return out.reshape(lead + (D_MODEL,)).astype(xn.dtype)


def finish(x, p_gate, o_fox, o_rw, o_mem, w):
    f32 = jnp.float32
    gates = jax.nn.sigmoid(p_gate.astype(f32)).reshape(p_gate.shape[:-1] + (N_BRANCH, D_MODEL))
    br = jnp.stack([o_fox.astype(x.dtype), o_rw.astype(x.dtype), o_mem.astype(x.dtype)], axis=-2)
    u = jnp.einsum('btnc,ncd->btnd', br, w['w_branch'])
    merged = jnp.sum(gates * u.astype(f32), axis=-2).astype(x.dtype)
    h = x + jnp.einsum('btd,de->bte', merged, w['w_out'])
    return h + peer(rms_norm(h, w['norm2_w']), w)


def setup_inputs(seed: int = 0) -> dict:
    key = jax.random.key(seed)
    keys = iter(jax.random.split(key, 64))

    def nrm(shape, scale=1.0):
        return scale * jax.random.normal(next(keys), shape, jnp.float32)

    def unif(shape):
        return jax.random.uniform(next(keys), shape, jnp.float32)

    n_pages = PAST_LEN // PAGE_SIZE
    n_used = DEC_BATCH * n_pages
    n_pool = n_used + max(1, n_used // 4)
    page_table = jax.random.permutation(next(keys), n_pool)[:n_used].reshape(DEC_BATCH, n_pages).astype(jnp.int32)
    return {
        'x_prompt': nrm((BATCH, SEQ, D_MODEL)),
        'x_sample': nrm((DEC_BATCH, DEC_SEQ, D_MODEL)),
        'cache_fox_k': nrm((n_pool, PAGE_SIZE, FOX_HEADS, FOX_HD)),
        'cache_fox_v': nrm((n_pool, PAGE_SIZE, FOX_HEADS, FOX_HD)),
        'cache_fox_logf': jax.nn.log_sigmoid(3.0 + nrm((n_pool, PAGE_SIZE, FOX_HEADS), 0.5)),
        'cache_mem_k': nrm((DEC_BATCH, MEM_TOKENS, MEM_HEADS, MEM_HD)),
        'cache_mem_v': nrm((DEC_BATCH, MEM_TOKENS, MEM_HEADS, MEM_HD)),
        'state_rwkv': nrm((DEC_BATCH, RW_HEADS, RW_HD, RW_HD), 0.5),
        'state_rwkv_shift': nrm((DEC_BATCH, RW_IN)),
        'page_table': page_table,
        'mem_prompt': nrm((BATCH, MEM_TOKENS, D_MODEL)),
        'norm1_w': 1.0 + nrm((D_MODEL,), 0.02),
        'w_in': nrm((D_MODEL, D_IN), D_MODEL ** -0.5),
        'fox_b_f': 3.0 + nrm((FOX_HEADS,), 0.5),
        'fox_qn_w': 1.0 + nrm((FOX_HD,), 0.02),
        'fox_kn_w': 1.0 + nrm((FOX_HD,), 0.02),
        'rw_mu': unif((RW_IN,)),
        'rw_w0': nrm((RW_W,), 0.5),
        'rw_w_up': nrm((RW_W_RANK, RW_W), RW_W_RANK ** -0.5),
        'rw_a0': nrm((RW_W,), 0.1),
        'rw_a_up': nrm((RW_A_RANK, RW_W), RW_A_RANK ** -0.5),
        'rw_g_up': nrm((RW_G_RANK, RW_W), RW_G_RANK ** -0.5),
        'rw_k_k': 0.85 + nrm((RW_W,), 0.02),
        'rw_k_a': 1.0 + nrm((RW_W,), 0.02),
        'rw_r_k': nrm((RW_HEADS, RW_HD), 0.1),
        'rw_ln_w': 1.0 + nrm((RW_W,), 0.02),
        'rw_ln_b': nrm((RW_W,), 0.02),
        'mem_norm_w': 1.0 + nrm((D_MODEL,), 0.02),
        'w_mem_kv': nrm((D_MODEL, 2 * MEM_W), D_MODEL ** -0.5),
        'mem_qn_w': 1.0 + nrm((MEM_HD,), 0.02),
        'mem_kn_w': 1.0 + nrm((MEM_HD,), 0.02),
        'w_branch': nrm((N_BRANCH, BRANCH_W, D_MODEL), BRANCH_W ** -0.5),
        'w_out': nrm((D_MODEL, D_MODEL), D_MODEL ** -0.5),
        'norm2_w': 1.0 + nrm((D_MODEL,), 0.02),
        'peer_w_q': nrm((D_MODEL, PEER_HEADS * PEER_QDIM), D_MODEL ** -0.5),
        'peer_keys1': nrm((PEER_HEADS, PEER_NK, PEER_HALF), PEER_HALF ** -0.5),
        'peer_keys2': nrm((PEER_HEADS, PEER_NK, PEER_HALF), PEER_HALF ** -0.5),
        'peer_u': nrm((PEER_EXPERTS, D_MODEL), D_MODEL ** -0.5),
        'peer_v': nrm((PEER_EXPERTS, D_MODEL), PEER_HEADS ** -0.5),
    }


def reference(x_prompt, x_sample, cache_fox_k, cache_fox_v, cache_fox_logf, cache_mem_k, cache_mem_v,
              state_rwkv, state_rwkv_shift, page_table, mem_prompt, norm1_w, w_in, fox_b_f, fox_qn_w,
              fox_kn_w, rw_mu, rw_w0, rw_w_up, rw_a0, rw_a_up, rw_g_up, rw_k_k, rw_k_a, rw_r_k, rw_ln_w,
              rw_ln_b, mem_norm_w, w_mem_kv, mem_qn_w, mem_kn_w, w_branch, w_out, norm2_w, peer_w_q,
              peer_keys1, peer_keys2, peer_u, peer_v):
    w = {'rw_mu': rw_mu, 'rw_w0': rw_w0, 'rw_w_up': rw_w_up, 'rw_a0': rw_a0, 'rw_a_up': rw_a_up,
         'rw_g_up': rw_g_up, 'rw_k_k': rw_k_k, 'rw_k_a': rw_k_a, 'rw_r_k': rw_r_k, 'rw_ln_w': rw_ln_w,
         'rw_ln_b': rw_ln_b, 'mem_norm_w': mem_norm_w, 'w_mem_kv': w_mem_kv, 'mem_kn_w': mem_kn_w,
         'w_branch': w_branch, 'w_out': w_out, 'norm2_w': norm2_w, 'peer_w_q': peer_w_q,
         'peer_keys1': peer_keys1, 'peer_keys2': peer_keys2, 'peer_u': peer_u, 'peer_v': peer_v}

    bp = x_prompt.shape[0]
    p_fox, p_rw, p_mq, p_gate = project_in(rms_norm(x_prompt, norm1_w), w_in)
    q_p, k_p, v_p, logf_p = fox_prepare(p_fox, fox_b_f, fox_qn_w, fox_kn_w)
    o_fox = fox_prompt(q_p, k_p, v_p, logf_p)
    o_rw, rw_state_p, rw_shift_p = rwkv_branch(
        p_rw, jnp.zeros((bp, RW_IN), p_rw.dtype), jnp.zeros((bp, RW_HEADS, RW_HD, RW_HD), jnp.float32), w)
    mem_k_p, mem_v_p = mem_kv(mem_prompt, w)
    o_mem = mem_attend(p_mq, mem_k_p, mem_v_p, mem_qn_w)
    y_prompt = finish(x_prompt, p_gate, o_fox, o_rw, o_mem, w)

    db = x_sample.shape[0]
    s_fox, s_rw, s_mq, s_gate = project_in(rms_norm(x_sample, norm1_w), w_in)
    q_s, k_s, v_s, logf_s = fox_prepare(s_fox, fox_b_f, fox_qn_w, fox_kn_w)
    past = page_table.shape[1] * cache_fox_k.shape[1]
    k_past = jnp.take(cache_fox_k, page_table, axis=0).reshape(db, past, FOX_HEADS, FOX_HD)
    v_past = jnp.take(cache_fox_v, page_table, axis=0).reshape(db, past, FOX_HEADS, FOX_HD)
    logf_past = jnp.take(cache_fox_logf, page_table, axis=0).reshape(db, past, FOX_HEADS)
    o_fox_s = fox_sample(q_s, k_s, v_s, logf_s, k_past, v_past, logf_past)
    o_rw_s, rw_state_s, rw_shift_s = rwkv_branch(s_rw, state_rwkv_shift, state_rwkv, w)
    o_mem_s = mem_attend(s_mq, cache_mem_k, cache_mem_v, mem_qn_w)
    y_sample = finish(x_sample, s_gate, o_fox_s, o_rw_s, o_mem_s, w)

    return (y_prompt, y_sample, k_p, v_p, logf_p, rw_state_p, rw_shift_p, mem_k_p, mem_v_p,
            k_s, v_s, logf_s, rw_state_s, rw_shift_s)
```

```python
import functools
import math

import jax
import jax.numpy as jnp
from jax import lax
from jax.experimental import pallas as pl
from jax.experimental.pallas import tpu as pltpu

f32 = jnp.float32
bf16 = jnp.bfloat16

D_MODEL = 1024
RMS_EPS = 1e-6
NEG_INF = -1e30
FOX_HEADS = 8
FOX_HD = 64
FOX_W = 512
FOX_SCALE = FOX_HD ** -0.5
RW_W = 512
RW_IN = 1792
RW_DECAY = math.exp(-0.5)
GN_EPS = 64e-5
MEM_HEADS = 4
MEM_HD = 128
MEM_W = 512
MEM_SCALE = MEM_HD ** -0.5
PEER_HEADS = 8
PEER_NK = 128
PEER_HALF = 128
PEER_TOPK = 16

LANES = 128
VMEM_LIMIT = 56 * 1024 * 1024

_PK_Q, _PK_K, _PK_V, _PK_RW, _PK_MQ, _PK_F, _PK_END = 0, 512, 1024, 1536, 3328, 3840, 3968


def _cparams(sem):
    return pltpu.CompilerParams(dimension_semantics=sem, vmem_limit_bytes=VMEM_LIMIT)


def _block_diag_ones(width, group):
    r = lax.broadcasted_iota(jnp.int32, (width, width), 0) // group
    c = lax.broadcasted_iota(jnp.int32, (width, width), 1) // group
    return (r == c).astype(bf16)


def _split_dot(x, w):
    hi = x.astype(bf16)
    lo = (x - hi.astype(f32)).astype(bf16)
    return (jnp.dot(hi, w, preferred_element_type=f32)
            + jnp.dot(lo, w, preferred_element_type=f32))


def _rms(x, g):
    return x * lax.rsqrt(jnp.mean(x * x, axis=-1, keepdims=True) + RMS_EPS) * g


def _log_sigmoid(x):
    return jnp.minimum(x, 0.0) - jnp.log1p(jnp.exp(-jnp.abs(x)))


def _sigmoid(x):
    return 1.0 / (1.0 + jnp.exp(-x))


def _inproj_body(x_ref, g_ref, w_ref, bf_ref, qnw_ref, knw_ref, mqw_ref, bd_ref, tri_ref,
                 q_ref, kb_ref, vb_ref, kp_ref, vp_ref, logf_ref, c_ref, ct_ref, prw_ref, mq_ref,
                 carry_ref, *, head_major):
    @pl.when(pl.program_id(1) == 0)
    def _():
        carry_ref[...] = jnp.zeros_like(carry_ref)

    x = x_ref[0]
    xn = _rms(x, g_ref[...]).astype(bf16)
    p = jnp.dot(xn, w_ref[...], preferred_element_type=f32)
    q = p[:, _PK_Q:_PK_K]
    k = p[:, _PK_K:_PK_V]
    v = p[:, _PK_V:_PK_RW]
    prw_ref[0] = p[:, _PK_RW:_PK_MQ]
    bd = bd_ref[...]
    qn = q * lax.rsqrt(_split_dot(q * q, bd) * (1.0 / FOX_HD) + RMS_EPS) * qnw_ref[...]
    kn = k * lax.rsqrt(_split_dot(k * k, bd) * (1.0 / FOX_HD) + RMS_EPS) * knw_ref[...]
    kp_ref[0] = kn
    vp_ref[0] = v
    qs = (qn * FOX_SCALE).astype(bf16)
    kb = kn.astype(bf16)
    vb = v.astype(bf16)
    if head_major:
        for h in range(FOX_HEADS):
            sl = slice(h * FOX_HD, (h + 1) * FOX_HD)
            q_ref[0, h] = qs[:, sl]
            kb_ref[0, h] = kb[:, sl]
            vb_ref[0, h] = vb[:, sl]
    else:
        q_ref[0] = qs
        kb_ref[0] = kb
        vb_ref[0] = vb
    for h in range(MEM_HEADS):
        sl = slice(_PK_MQ + h * MEM_HD, _PK_MQ + (h + 1) * MEM_HD)
        mq_ref[0, :, h * MEM_HD:(h + 1) * MEM_HD] = _rms(p[:, sl], mqw_ref[...]).astype(bf16)
    lane = lax.broadcasted_iota(jnp.int32, (x.shape[0], LANES), 1)
    logf = jnp.where(lane < FOX_HEADS, _log_sigmoid(p[:, _PK_F:_PK_END] + bf_ref[...]), 0.0)
    logf_ref[0] = logf
    c = _split_dot_left(tri_ref[...], logf) + carry_ref[0:1, :]
    c_ref[0] = c
    carry_ref[0:1, :] = c[x.shape[0] - 1:x.shape[0], :]
    ct_ref[0] = jnp.transpose(c)[0:FOX_HEADS, :]


def _split_dot_left(w, x):
    hi = x.astype(bf16)
    lo = (x - hi.astype(f32)).astype(bf16)
    return (jnp.dot(w, hi, preferred_element_type=f32)
            + jnp.dot(w, lo, preferred_element_type=f32))


def _inproj(x3, norm1_w, w_pack, bf_pad, qnw, knw, mqw, *, head_major, tile):
    nb, ns, _ = x3.shape
    nt = ns // tile
    bd = _block_diag_ones(FOX_W, FOX_HD)
    tri = (lax.broadcasted_iota(jnp.int32, (tile, tile), 0)
           >= lax.broadcasted_iota(jnp.int32, (tile, tile), 1)).astype(bf16)
    const = lambda shape: pl.BlockSpec(shape, lambda b, j: (0,) * len(shape))
    tok = lambda w: pl.BlockSpec((1, tile, w), lambda b, j: (b, j, 0))
    if head_major:
        qkv_shape = jax.ShapeDtypeStruct((nb, FOX_HEADS, ns, FOX_HD), bf16)
        qkv_spec = pl.BlockSpec((1, FOX_HEADS, tile, FOX_HD), lambda b, j: (b, 0, j, 0))
    else:
        qkv_shape = jax.ShapeDtypeStruct((nb, ns, FOX_W), bf16)
        qkv_spec = tok(FOX_W)
    out_shape = (
        qkv_shape, qkv_shape, qkv_shape,
        jax.ShapeDtypeStruct((nb, ns, FOX_W), f32),
        jax.ShapeDtypeStruct((nb, ns, FOX_W), f32),
        jax.ShapeDtypeStruct((nb, ns, LANES), f32),
        jax.ShapeDtypeStruct((nb, ns, LANES), f32),
        jax.ShapeDtypeStruct((nb, FOX_HEADS, ns), f32),
        jax.ShapeDtypeStruct((nb, ns, RW_IN), f32),
        jax.ShapeDtypeStruct((nb, ns, MEM_W), bf16),
    )
    out_specs = (
        qkv_spec, qkv_spec, qkv_spec, tok(FOX_W), tok(FOX_W), tok(LANES), tok(LANES),
        pl.BlockSpec((1, FOX_HEADS, tile), lambda b, j: (b, 0, j)),
        tok(RW_IN), tok(MEM_W),
    )
    return pl.pallas_call(
        functools.partial(_inproj_body, head_major=head_major),
        grid=(nb, nt),
        in_specs=[tok(D_MODEL), const((1, D_MODEL)), const((D_MODEL, _PK_END)), const((1, LANES)),
                  const((1, FOX_W)), const((1, FOX_W)), const((1, MEM_HD)), const((FOX_W, FOX_W)),
                  const((tile, tile))],
        out_specs=out_specs,
        out_shape=out_shape,
        scratch_shapes=[pltpu.VMEM((8, LANES), f32)],
        compiler_params=_cparams(("arbitrary", "arbitrary")),
        name="inproj",
    )(x3, norm1_w, w_pack, bf_pad, qnw, knw, mqw, bd, tri)


def _prep_inproj_weights(w_in, fox_b_f, fox_qn_w, fox_kn_w, mem_qn_w):
    fox_in = 3 * FOX_W + FOX_HEADS
    w_fox = w_in[:, :fox_in]
    w_rw = w_in[:, fox_in:fox_in + RW_IN]
    w_mq = w_in[:, fox_in + RW_IN:fox_in + RW_IN + MEM_W]
    w_gate = w_in[:, fox_in + RW_IN + MEM_W:]
    w_f = jnp.pad(w_fox[:, 3 * FOX_W:], ((0, 0), (0, LANES - FOX_HEADS)))
    w_pack = jnp.concatenate([w_fox[:, :3 * FOX_W], w_rw, w_mq, w_f], axis=1).astype(bf16)
    bf_pad = jnp.pad(fox_b_f, (0, LANES - FOX_HEADS)).reshape(1, LANES)
    qnw = jnp.tile(fox_qn_w, FOX_HEADS).reshape(1, FOX_W)
    knw = jnp.tile(fox_kn_w, FOX_HEADS).reshape(1, FOX_W)
    return w_pack, bf_pad, qnw, knw, mem_qn_w.reshape(1, MEM_HD), w_gate.astype(bf16)


def _fox_prompt_body(q_ref, k_ref, v_ref, c_ref, ct_ref, o_ref, *, blk):
    hp = pl.program_id(1)
    qi = pl.program_id(2)
    lane = lax.broadcasted_iota(jnp.int32, (blk, LANES), 1)
    c_blk = c_ref[0]
    row = lax.broadcasted_iota(jnp.int32, (blk, blk), 0)
    col = lax.broadcasted_iota(jnp.int32, (blk, blk), 1)
    outs = []
    for hh in range(2):
        h = hp * 2 + hh
        q = q_ref[0, hh]
        cq = jnp.sum(jnp.where(lane == h, c_blk, 0.0), axis=1, keepdims=True)

        def step(kj, carry, masked):
            m, l, acc = carry
            start = pl.multiple_of(kj * blk, blk)
            k = k_ref[0, hh, pl.ds(start, blk), :]
            v = v_ref[0, hh, pl.ds(start, blk), :]
            s = lax.dot_general(q, k, (((1,), (1,)), ((), ())), preferred_element_type=f32)
            ck = ct_ref[0, pl.ds(h, 1), pl.ds(start, blk)]
            s = s + (cq - ck)
            if masked:
                s = jnp.where(col <= row, s, NEG_INF)
            m_new = jnp.maximum(m, jnp.max(s, axis=1, keepdims=True))
            alpha = jnp.exp(m - m_new)
            p = jnp.exp(s - m_new)
            l = l * alpha + jnp.sum(p, axis=1, keepdims=True)
            acc = acc * alpha + jnp.dot(p.astype(bf16), v, preferred_element_type=f32)
            return m_new, l, acc

        init = (jnp.full((blk, 1), NEG_INF, f32), jnp.zeros((blk, 1), f32), jnp.zeros((blk, FOX_HD), f32))
        carry = lax.fori_loop(0, qi, functools.partial(step, masked=False), init)
        m, l, acc = step(qi, carry, True)
        outs.append(acc / l)
    o_ref[0] = jnp.concatenate(outs, axis=1).astype(o_ref.dtype)


def _fox_prompt(q, k, v, c, ct, *, blk):
    nb, nh, ns, hd = q.shape
    pair = lambda: pl.BlockSpec((1, 2, ns, hd), lambda b, hp, qi: (b, hp, 0, 0))
    return pl.pallas_call(
        functools.partial(_fox_prompt_body, blk=blk),
        grid=(nb, nh // 2, ns // blk),
        in_specs=[pl.BlockSpec((1, 2, blk, hd), lambda b, hp, qi: (b, hp, qi, 0)), pair(), pair(),
                  pl.BlockSpec((1, blk, LANES), lambda b, hp, qi: (b, qi, 0)),
                  pl.BlockSpec((1, nh, ns), lambda b, hp, qi: (b, 0, 0))],
        out_specs=pl.BlockSpec((1, blk, 2 * hd), lambda b, hp, qi: (b, qi, hp)),
        out_shape=jax.ShapeDtypeStruct((nb, ns, nh * hd), bf16),
        compiler_params=_cparams(("arbitrary", "arbitrary", "arbitrary")),
        name="fox_prompt",
    )(q, k, v, c, ct)


UNIT_COLS = 16
RW_PAIRS = 4


def _rwkv_body(*refs, n_blk_b, tc, n_sb, n_st, prompt_mode):
    if prompt_mode:
        (p_ref, mu_ref, w0_ref, wup_ref, a0_ref, aup_ref, gup_ref, kk_w_ref, ka_ref, rk_ref, lnw_ref, lnb_ref,
         bd_ref, o_ref, s_ref, prev_ref, kk_s, dec_s, beta_s, kt_s, r_s, v_s, g_s, vts_s, ots_s, stage_s, o_s) = refs
    else:
        (p_ref, pf_ref, sin_ref, mu_ref, w0_ref, wup_ref, a0_ref, aup_ref, gup_ref, kk_w_ref, ka_ref, rk_ref,
         lnw_ref, lnb_ref, bd_ref, o_ref, s_ref, kk_s, dec_s, beta_s, kt_s, r_s, v_s, g_s, vts_s, ots_s,
         stage_s, o_s) = refs
    n_rows = n_blk_b * tc
    n_units = n_rows // UNIT_COLS
    step = pl.program_id(0)

    if prompt_mode:
        @pl.when(step == 0)
        def _():
            s_ref[...] = jnp.zeros_like(s_ref)
            prev_ref[...] = jnp.zeros_like(prev_ref)
    else:
        s_ref[...] = sin_ref[...]
    stage_s[...] = jnp.zeros_like(stage_s)
    ots_s[...] = jnp.zeros_like(ots_s)

    p = p_ref[...].reshape(n_rows, RW_IN)
    rowi = lax.broadcasted_iota(jnp.int32, (n_rows, 1), 0)
    rolled = pltpu.roll(p, 1, axis=0)
    if prompt_mode:
        prev = rolled
        for bb in range(n_blk_b):
            prev = jnp.where(rowi == bb * tc, prev_ref[bb, 0:1, :], prev)
            prev_ref[bb, 0:1, :] = p[(bb + 1) * tc - 1:(bb + 1) * tc, :]
    else:
        prev = jnp.where(rowi % tc == 0, pf_ref[...], rolled)
    xs = p + (prev - p) * mu_ref[...]
    r = xs[:, 0:512]
    k = xs[:, 512:1024]
    v = xs[:, 1024:1536]
    zwa = xs[:, 1536:1664]
    zg = xs[:, 1664:1792]
    log_w = -RW_DECAY * _sigmoid(w0_ref[...] + jnp.dot(jnp.tanh(zwa).astype(bf16), wup_ref[...],
                                                        preferred_element_type=f32))
    a = _sigmoid(a0_ref[...] + jnp.dot(zwa.astype(bf16), aup_ref[...], preferred_element_type=f32))
    g = jnp.dot(_sigmoid(zg).astype(bf16), gup_ref[...], preferred_element_type=f32)
    bd = bd_ref[...]
    kk = k * kk_w_ref[...]
    kk = kk / jnp.maximum(jnp.sqrt(_split_dot(kk * kk, bd)), 1e-12)
    kt = k * (1.0 + (a - 1.0) * ka_ref[...])
    kk_s[...] = kk
    dec_s[...] = jnp.exp(log_w)
    beta_s[...] = kk * a
    kt_s[...] = kt
    r_s[...] = r
    v_s[...] = v
    g_s[...] = g

    def unit_rows(u):
        if prompt_mode:
            return [(bb * tc + u * n_st, n_st) for bb in range(n_sb)]
        return [(u * UNIT_COLS, UNIT_COLS)]

    def vt_body(u, _):
        off = 0
        for start, n in unit_rows(u):
            stage_s[off:off + n, :] = v_s[pl.ds(pl.multiple_of(start, 8), n), :]
            off += n
        vts_s[u] = jnp.transpose(stage_s[...])
        return 0

    lax.fori_loop(0, n_units, vt_body, 0)

    lane = lax.broadcasted_iota(jnp.int32, (FOX_HD, LANES), 1)
    lo = lane < FOX_HD

    def unit_body(u, _):
        for bb in range(n_sb):
            if prompt_mode:
                sidx = bb
                blk0, blk_n, roff = bb * tc + u * n_st, n_st, 0
            else:
                sidx = u * n_sb + bb
                blk0, blk_n, roff = u * UNIT_COLS, UNIT_COLS, bb * n_st
            for pr in range(RW_PAIRS):
                sl = slice(pr * LANES, (pr + 1) * LANES)
                s = s_ref[sidx, pr]
                rows = pl.ds(pl.multiple_of(blk0, 8), blk_n)
                kk_b, w_b, be_b, kt_b, r_b = (ref[rows, sl] for ref in (kk_s, dec_s, beta_s, kt_s, r_s))
                for j in range(n_st):
                    col = bb * n_st + j
                    jr = slice(roff + j, roff + j + 1)
                    kk_r, w_r, be_r, kt_r, r_r = kk_b[jr], w_b[jr], be_b[jr], kt_b[jr], r_b[jr]
                    vt = vts_s[u, sl, col:col + 1]
                    vcol = jnp.where(lo, vt[0:FOX_HD], vt[FOX_HD:LANES])
                    pk = s * kk_r
                    sa_lo = jnp.sum(jnp.where(lo, pk, 0.0), axis=1, keepdims=True)
                    sa_hi = jnp.sum(jnp.where(lo, 0.0, pk), axis=1, keepdims=True)
                    sa = jnp.where(lo, sa_lo, sa_hi)
                    s = s * w_r - sa * be_r + vcol * kt_r
                    qr = s * r_r
                    ots_s[u, pr * LANES:pr * LANES + FOX_HD, col:col + 1] = jnp.sum(
                        jnp.where(lo, qr, 0.0), axis=1, keepdims=True)
                    ots_s[u, pr * LANES + FOX_HD:(pr + 1) * LANES, col:col + 1] = jnp.sum(
                        jnp.where(lo, 0.0, qr), axis=1, keepdims=True)
                s_ref[sidx, pr] = s
        return 0

    lax.fori_loop(0, n_units, unit_body, 0)

    def ot_body(u, _):
        t = jnp.transpose(ots_s[u])
        off = 0
        for start, n in unit_rows(u):
            o_s[pl.ds(pl.multiple_of(start, 8), n), :] = t[off:off + n, :]
            off += n
        return 0

    lax.fori_loop(0, n_units, ot_body, 0)

    o = o_s[...]
    inv = 1.0 / FOX_HD
    mean = _split_dot(o, bd) * inv
    d = o - mean
    var = _split_dot(d * d, bd) * inv
    o_gn = d * lax.rsqrt(var + GN_EPS) * lnw_ref[...] + lnb_ref[...]
    bonus = _split_dot(r_s[...] * kt_s[...] * rk_ref[...], bd) * v_s[...]
    out = (o_gn + bonus) * g_s[...]
    o_ref[...] = out.reshape(o_ref.shape).astype(o_ref.dtype)


def _rwkv_weights(rw_mu, rw_w0, rw_w_up, rw_a0, rw_a_up, rw_g_up, rw_k_k, rw_k_a, rw_r_k, rw_ln_w, rw_ln_b):
    row = lambda a: a.reshape(1, -1)
    wup = jnp.concatenate([rw_w_up, jnp.zeros_like(rw_a_up)], axis=0).astype(bf16)
    aup = jnp.concatenate([jnp.zeros_like(rw_w_up), rw_a_up], axis=0).astype(bf16)
    return (row(rw_mu), row(rw_w0), wup, row(rw_a0), aup, rw_g_up.astype(bf16), row(rw_k_k), row(rw_k_a),
            row(rw_r_k), row(rw_ln_w), row(rw_ln_b), _block_diag_ones(RW_W, FOX_HD))


def _rwkv_scratch(n_rows):
    n_units = n_rows // UNIT_COLS
    rows = [pltpu.VMEM((n_rows, RW_W), f32) for _ in range(7)]
    return rows + [pltpu.VMEM((n_units, RW_W, LANES), f32), pltpu.VMEM((n_units, RW_W, LANES), f32),
                   pltpu.VMEM((LANES, RW_W), f32), pltpu.VMEM((n_rows, RW_W), f32)]


def _rwkv_prompt(prw, weights, *, tc):
    nb, ns, _ = prw.shape
    const = lambda a: pl.BlockSpec(a.shape, lambda j: (0,) * a.ndim)
    n_st = UNIT_COLS // nb
    return pl.pallas_call(
        functools.partial(_rwkv_body, n_blk_b=nb, tc=tc, n_sb=nb, n_st=n_st, prompt_mode=True),
        grid=(ns // tc,),
        in_specs=[pl.BlockSpec((nb, tc, RW_IN), lambda j: (0, j, 0))] + [const(w) for w in weights],
        out_specs=(pl.BlockSpec((nb, tc, RW_W), lambda j: (0, j, 0)),
                   pl.BlockSpec((nb, RW_PAIRS, FOX_HD, LANES), lambda j: (0, 0, 0, 0))),
        out_shape=(jax.ShapeDtypeStruct((nb, ns, RW_W), bf16),
                   jax.ShapeDtypeStruct((nb, RW_PAIRS, FOX_HD, LANES), f32)),
        scratch_shapes=[pltpu.VMEM((nb, 8, RW_IN), f32)] + _rwkv_scratch(nb * tc),
        compiler_params=_cparams(("arbitrary",)),
        name="rwkv_prompt",
    )(prw, *weights)


def _rwkv_sample(prw_rows, prev_first_rows, state, weights, *, seq, rows_per_blk):
    n_rows = prw_rows.shape[0]
    nb_blk = rows_per_blk // seq
    const = lambda a: pl.BlockSpec(a.shape, lambda j: (0,) * a.ndim)
    n_sb = UNIT_COLS // seq
    return pl.pallas_call(
        functools.partial(_rwkv_body, n_blk_b=nb_blk, tc=seq, n_sb=n_sb, n_st=seq, prompt_mode=False),
        grid=(n_rows // rows_per_blk,),
        in_specs=[pl.BlockSpec((rows_per_blk, RW_IN), lambda j: (j, 0)),
                  pl.BlockSpec((rows_per_blk, RW_IN), lambda j: (j, 0)),
                  pl.BlockSpec((nb_blk, RW_PAIRS, FOX_HD, LANES), lambda j: (j, 0, 0, 0))]
                 + [const(w) for w in weights],
        out_specs=(pl.BlockSpec((rows_per_blk, RW_W), lambda j: (j, 0)),
                   pl.BlockSpec((nb_blk, RW_PAIRS, FOX_HD, LANES), lambda j: (j, 0, 0, 0))),
        out_shape=(jax.ShapeDtypeStruct((n_rows, RW_W), bf16),
                   jax.ShapeDtypeStruct((n_rows // seq, RW_PAIRS, FOX_HD, LANES), f32)),
        scratch_shapes=_rwkv_scratch(rows_per_blk),
        compiler_params=_cparams(("arbitrary",)),
        name="rwkv_sample",
    )(prw_rows, prev_first_rows, state, *weights)


def _pack_state(state):
    b = state.shape[0]
    return state.reshape(b, RW_PAIRS, 2, FOX_HD, FOX_HD).transpose(0, 1, 3, 2, 4).reshape(b, RW_PAIRS, FOX_HD, LANES)


def _unpack_state(packed):
    b = packed.shape[0]
    return packed.reshape(b, RW_PAIRS, FOX_HD, 2, FOX_HD).transpose(0, 1, 3, 2, 4).reshape(b, 2 * RW_PAIRS, FOX_HD, FOX_HD)


def _fox_sample_body(pt_ref, *refs, n_pg, n_q):
    del pt_ref
    q_ref, kn_ref, vn_ref, lfn_ref, tri_ref, hm_ref = refs[:6]
    k_refs = refs[6:6 + n_pg]
    v_refs = refs[6 + n_pg:6 + 2 * n_pg]
    lf_refs = refs[6 + 2 * n_pg:6 + 3 * n_pg]
    o_ref, m_s, l_s, acc_s, carry_s = refs[6 + 3 * n_pg:]
    jg = pl.program_id(1)
    n_rows = n_q * FOX_HEADS
    page = tri_ref.shape[0]

    @pl.when(jg == 0)
    def _():
        m_s[...] = jnp.full_like(m_s, NEG_INF)
        l_s[...] = jnp.zeros_like(l_s)
        acc_s[...] = jnp.zeros_like(acc_s)
        carry_s[...] = jnp.zeros_like(carry_s)

    hm = hm_ref[...]
    q4 = q_ref[0]
    qm = jnp.concatenate([jnp.broadcast_to(q4[i:i + 1], (FOX_HEADS, FOX_W)) for i in range(n_q)], axis=0) * hm
    tri = tri_ref[...]

    def cum_bias(lf):
        ct = _split_dot(lf, tri) + carry_s[...]
        carry_s[...] = jnp.broadcast_to(ct[:, page - 1:page], carry_s.shape)
        return jnp.concatenate([ct] * n_q, axis=0)

    def update(s_all, vs):
        m_old = m_s[...]
        m_new = jnp.maximum(m_old, jnp.max(s_all, axis=1, keepdims=True))
        alpha = jnp.exp(m_old - m_new)
        p = jnp.exp(s_all - m_new)
        l_s[...] = l_s[...] * alpha + jnp.sum(p, axis=1, keepdims=True)
        acc = acc_s[...] * alpha
        for g, vv in enumerate(vs):
            acc = acc + jnp.dot(p[:, g * page:(g + 1) * page].astype(bf16), vv, preferred_element_type=f32)
        acc_s[...] = acc
        m_s[...] = m_new

    s_list = []
    for g in range(n_pg):
        kp = k_refs[g][0].astype(bf16)
        s = lax.dot_general(qm, kp, (((1,), (1,)), ((), ())), preferred_element_type=f32)
        s_list.append(s - cum_bias(lf_refs[g][0]))
    update(jnp.concatenate(s_list, axis=1), [v_refs[g][0].astype(bf16) for g in range(n_pg)])

    @pl.when(jg == pl.num_programs(1) - 1)
    def _():
        pad = jnp.zeros((page - n_q, FOX_W), bf16)
        kn = jnp.concatenate([kn_ref[0], pad], axis=0)
        vn = jnp.concatenate([vn_ref[0], pad], axis=0)
        s = lax.dot_general(qm, kn, (((1,), (1,)), ((), ())), preferred_element_type=f32)
        bias = cum_bias(lfn_ref[0])
        key = lax.broadcasted_iota(jnp.int32, (n_rows, page), 1)
        qpos = lax.broadcasted_iota(jnp.int32, (n_rows, page), 0) // FOX_HEADS
        update(jnp.where(key <= qpos, s - bias, NEG_INF), [vn])
        o = acc_s[...] / l_s[...] * hm.astype(f32)
        o_ref[0] = jnp.concatenate(
            [jnp.sum(o[i * FOX_HEADS:(i + 1) * FOX_HEADS], axis=0, keepdims=True) for i in range(n_q)],
            axis=0).astype(o_ref.dtype)


def _fox_sample(page_table, q, kn, vn, lfn_t, cache_k, cache_v, cache_lft, *, n_pg):
    nb, n_q, _ = q.shape
    n_pages = page_table.shape[1]
    page = cache_k.shape[1]
    n_rows = n_q * FOX_HEADS
    tri = (lax.broadcasted_iota(jnp.int32, (page, page), 0)
           <= lax.broadcasted_iota(jnp.int32, (page, page), 1)).astype(bf16)
    hm = (lax.broadcasted_iota(jnp.int32, (n_rows, FOX_W), 0) % FOX_HEADS
          == lax.broadcasted_iota(jnp.int32, (n_rows, FOX_W), 1) // FOX_HD).astype(bf16)
    per_b = lambda shape: pl.BlockSpec((1,) + shape, lambda b, j, pt: (b, 0, 0))
    const = lambda a: pl.BlockSpec(a.shape, lambda b, j, pt: (0, 0))
    paged = lambda shape, g: pl.BlockSpec((1,) + shape, lambda b, j, pt: (pt[b, j * n_pg + g], 0, 0))
    in_specs = ([per_b((n_q, FOX_W))] * 3 + [per_b((FOX_HEADS, LANES)), const(tri), const(hm)]
                + [paged((page, FOX_W), g) for g in range(n_pg)]
                + [paged((page, FOX_W), g) for g in range(n_pg)]
                + [paged((FOX_HEADS, page), g) for g in range(n_pg)])
    grid_spec = pltpu.PrefetchScalarGridSpec(
        num_scalar_prefetch=1, grid=(nb, n_pages // n_pg), in_specs=in_specs,
        out_specs=pl.BlockSpec((1, n_q, FOX_W), lambda b, j, pt: (b, 0, 0)),
        scratch_shapes=[pltpu.VMEM((n_rows, 1), f32), pltpu.VMEM((n_rows, 1), f32),
                        pltpu.VMEM((n_rows, FOX_W), f32), pltpu.VMEM((FOX_HEADS, LANES), f32)])
    return pl.pallas_call(
        functools.partial(_fox_sample_body, n_pg=n_pg, n_q=n_q),
        grid_spec=grid_spec,
        out_shape=jax.ShapeDtypeStruct((nb, n_q, FOX_W), bf16),
        compiler_params=_cparams(("arbitrary", "arbitrary")),
        name="fox_sample",
    )(page_table, q, kn, vn, lfn_t, tri, hm, *([cache_k] * n_pg), *([cache_v] * n_pg), *([cache_lft] * n_pg))


def _mem_kv_body(m_ref, g_ref, w_ref, knw_ref, k_ref, v_ref):
    n = _rms(m_ref[0], g_ref[...]).astype(bf16)
    kv = jnp.dot(n, w_ref[...], preferred_element_type=f32)
    for h in range(MEM_HEADS):
        sl = slice(h * MEM_HD, (h + 1) * MEM_HD)
        k_ref[0, :, sl] = _rms(kv[:, sl], knw_ref[...])
    v_ref[0] = kv[:, MEM_W:]


def _mem_kv(mem, mem_norm_w, w_mem_kv, mem_kn_w):
    nb, nm, _ = mem.shape
    const = lambda shape: pl.BlockSpec(shape, lambda b: (0,) * len(shape))
    out = jax.ShapeDtypeStruct((nb, nm, MEM_W), f32)
    return pl.pallas_call(
        _mem_kv_body,
        grid=(nb,),
        in_specs=[pl.BlockSpec((1, nm, D_MODEL), lambda b: (b, 0, 0)), const((1, D_MODEL)),
                  const((D_MODEL, 2 * MEM_W)), const((1, MEM_HD))],
        out_specs=(pl.BlockSpec((1, nm, MEM_W), lambda b: (b, 0, 0)),) * 2,
        out_shape=(out, out),
        compiler_params=_cparams(("arbitrary",)),
        name="mem_kv",
    )(mem, mem_norm_w.reshape(1, D_MODEL), w_mem_kv.astype(bf16), mem_kn_w.reshape(1, MEM_HD))


def _mem_attend_body(q_ref, k_ref, v_ref, o_ref):
    q = q_ref[0]
    for h in range(MEM_HEADS):
        sl = slice(h * MEM_HD, (h + 1) * MEM_HD)
        k = k_ref[0, :, sl].astype(bf16)
        v = v_ref[0, :, sl].astype(bf16)
        s = lax.dot_general(q[:, sl], k, (((1,), (1,)), ((), ())), preferred_element_type=f32) * MEM_SCALE
        e = jnp.exp(s - jnp.max(s, axis=1, keepdims=True))
        p = e / jnp.sum(e, axis=1, keepdims=True)
        o_ref[0, :, sl] = jnp.dot(p.astype(bf16), v, preferred_element_type=f32).astype(o_ref.dtype)


def _mem_attend(q, mk, mv, *, tq):
    nb, ns, _ = q.shape
    nm = mk.shape[1]
    kv = pl.BlockSpec((1, nm, MEM_W), lambda b, j: (b, 0, 0))
    return pl.pallas_call(
        _mem_attend_body,
        grid=(nb, ns // tq),
        in_specs=[pl.BlockSpec((1, tq, MEM_W), lambda b, j: (b, j, 0)), kv, kv],
        out_specs=pl.BlockSpec((1, tq, MEM_W), lambda b, j: (b, j, 0)),
        out_shape=jax.ShapeDtypeStruct((nb, ns, MEM_W), bf16),
        compiler_params=_cparams(("arbitrary", "arbitrary")),
        name="mem_attend",
    )(q, mk, mv)


def _finish_body(x_ref, of_ref, or_ref, om_ref, g1_ref, wg_ref, wb_ref, wo_ref, g2_ref, h_ref, xn2_ref):
    x = x_ref[...]
    xn = _rms(x, g1_ref[...]).astype(bf16)
    merged = jnp.zeros_like(x)
    for n, br in enumerate((of_ref, or_ref, om_ref)):
        gate = _sigmoid(jnp.dot(xn, wg_ref[:, n * D_MODEL:(n + 1) * D_MODEL], preferred_element_type=f32))
        merged = merged + gate * jnp.dot(br[...], wb_ref[n], preferred_element_type=f32)
    h = x + jnp.dot(merged.astype(bf16), wo_ref[...], preferred_element_type=f32)
    h_ref[...] = h
    xn2_ref[...] = _rms(h, g2_ref[...]).astype(bf16)


def _finish(x, o_fox, o_rw, o_mem, norm1_w, w_gate, w_branch, w_out, norm2_w, *, tile):
    n = x.shape[0]
    const = lambda shape: pl.BlockSpec(shape, lambda j: (0,) * len(shape))
    tok = lambda w: pl.BlockSpec((tile, w), lambda j: (j, 0))
    return pl.pallas_call(
        _finish_body,
        grid=(n // tile,),
        in_specs=[tok(D_MODEL), tok(FOX_W), tok(RW_W), tok(MEM_W), const((1, D_MODEL)),
                  const((D_MODEL, 3 * D_MODEL)), const((3, FOX_W, D_MODEL)), const((D_MODEL, D_MODEL)),
                  const((1, D_MODEL))],
        out_specs=(tok(D_MODEL), tok(D_MODEL)),
        out_shape=(jax.ShapeDtypeStruct((n, D_MODEL), f32), jax.ShapeDtypeStruct((n, D_MODEL), bf16)),
        compiler_params=_cparams(("arbitrary",)),
        name="finish",
    )(x, o_fox, o_rw, o_mem, norm1_w.reshape(1, D_MODEL), w_gate, w_branch.astype(bf16), w_out.astype(bf16),
      norm2_w.reshape(1, D_MODEL))


_PEER_SLOTS = [(ra, rb) for ra in range(PEER_TOPK) for rb in range(PEER_TOPK) if (ra + 1) * (rb + 1) <= PEER_TOPK]
_PEER_SLOT_ROWS = -(-len(_PEER_SLOTS) // 8) * 8


def _top_rows(s, n_take):
    iota = lax.broadcasted_iota(jnp.int32, s.shape, 0).astype(f32)
    big = float(s.shape[0])
    vals, idxs = [], []
    for _ in range(n_take):
        m = jnp.max(s, axis=0, keepdims=True)
        idx = jnp.min(jnp.where(s == m, iota, big), axis=0, keepdims=True)
        vals.append(m)
        idxs.append(idx)
        s = jnp.where(iota == idx, -jnp.inf, s)
    return vals, idxs


def _route_body(x_ref, wq_ref, k1_ref, k2_ref, a_ref, b_ref, g_ref,
                q_s, cand_s, ea_s, eb_s, a_s, b_s, g_s, at_s, bt_s, gt_s):
    q_s[...] = lax.dot_general(wq_ref[...], x_ref[...], (((1,), (1,)), ((), ())),
                               preferred_element_type=f32).astype(bf16)
    cand_s[...] = jnp.full_like(cand_s, -jnp.inf)
    ea_s[...] = jnp.zeros_like(ea_s)
    eb_s[...] = jnp.zeros_like(eb_s)

    def head(h, _):
        base = pl.multiple_of(h * 2 * PEER_HALF, 2 * PEER_HALF)
        s1 = jnp.dot(k1_ref[h], q_s[pl.ds(base, PEER_HALF), :], preferred_element_type=f32)
        s2 = jnp.dot(k2_ref[h], q_s[pl.ds(base + PEER_HALF, PEER_HALF), :], preferred_element_type=f32)
        v1, i1 = _top_rows(s1, PEER_TOPK)
        v2, i2 = _top_rows(s2, PEER_TOPK)
        for slot, (ra, rb) in enumerate(_PEER_SLOTS):
            cand_s[slot:slot + 1, :] = v1[ra] + v2[rb]
            ea_s[slot:slot + 1, :] = i1[ra]
            eb_s[slot:slot + 1, :] = i2[rb]
        cand = cand_s[...]
        ea = ea_s[...]
        eb = eb_s[...]
        iota = lax.broadcasted_iota(jnp.int32, cand.shape, 0).astype(f32)
        scs = []
        for r in range(PEER_TOPK):
            m = jnp.max(cand, axis=0, keepdims=True)
            slot = jnp.min(jnp.where(cand == m, iota, float(_PEER_SLOT_ROWS)), axis=0, keepdims=True)
            hit = iota == slot
            a_s[r:r + 1, :] = jnp.sum(jnp.where(hit, ea, 0.0), axis=0, keepdims=True)
            b_s[r:r + 1, :] = jnp.sum(jnp.where(hit, eb, 0.0), axis=0, keepdims=True)
            cand = jnp.where(hit, -jnp.inf, cand)
            scs.append(m)
        es = [jnp.exp(sc - scs[0]) for sc in scs]
        z = es[0]
        for e in es[1:]:
            z = z + e
        for r in range(PEER_TOPK):
            g_s[r:r + 1, :] = es[r] / z
        rows = pl.ds(pl.multiple_of(h * PEER_TOPK, PEER_TOPK), PEER_TOPK)
        at_s[rows, :] = a_s[...]
        bt_s[rows, :] = b_s[...]
        gt_s[rows, :] = g_s[...]
        return 0

    lax.fori_loop(0, PEER_HEADS, head, 0)
    a_ref[...] = jnp.transpose(at_s[...])
    b_ref[...] = jnp.transpose(bt_s[...])
    g_ref[...] = jnp.transpose(gt_s[...])


def _route(xn2, wq_t, keys1, keys2, *, tile):
    n = xn2.shape[0]
    n_slots = PEER_HEADS * PEER_TOPK
    const = lambda shape: pl.BlockSpec(shape, lambda j: (0,) * len(shape))
    out = jax.ShapeDtypeStruct((n, n_slots), f32)
    ospec = pl.BlockSpec((tile, n_slots), lambda j: (j, 0))
    cand = lambda: pltpu.VMEM((_PEER_SLOT_ROWS, tile), f32)
    top = lambda: pltpu.VMEM((PEER_TOPK, tile), f32)
    full = lambda: pltpu.VMEM((n_slots, tile), f32)
    return pl.pallas_call(
        _route_body,
        grid=(n // tile,),
        in_specs=[pl.BlockSpec((tile, D_MODEL), lambda j: (j, 0)), const(wq_t.shape), const(keys1.shape),
                  const(keys2.shape)],
        out_specs=(ospec, ospec, ospec),
        out_shape=(out, out, out),
        scratch_shapes=[pltpu.VMEM((wq_t.shape[0], tile), bf16), cand(), cand(), cand(), top(), top(), top(),
                        full(), full(), full()],
        compiler_params=_cparams(("arbitrary",)),
        name="peer_route",
    )(xn2, wq_t, keys1, keys2)


G_PITCH = PEER_NK + 8
EXPERT_SUB = 512


def _gelu_tanh(x):
    return 0.5 * x * (1.0 + jnp.tanh(math.sqrt(2.0 / math.pi) * (x + 0.044715 * (x * x * x))))


def _peer_body(x_ref, a_ref, b_ref, g_ref, h_ref, u_ref, v_ref, y_ref, gs_ref, acc_ref):
    tile = x_ref.shape[0]
    c = pl.program_id(1)
    n_exp = u_ref.shape[0]

    @pl.when(c == 0)
    def _():
        acc_ref[...] = jnp.zeros_like(acc_ref)
        sub = lax.broadcasted_iota(jnp.int32, (PEER_NK, LANES), 0).astype(f32)

        def tok8(t8, _):
            rows = pl.ds(pl.multiple_of(t8 * 8, 8), 8)
            a8, b8, g8 = a_ref[rows, :], b_ref[rows, :], g_ref[rows, :]
            for j in range(8):
                hit_a = sub == a8[j:j + 1]
                gate = jnp.where(hit_a, g8[j:j + 1], 0.0)
                g_hi = gate.astype(bf16)
                g_lo = (gate - g_hi.astype(f32)).astype(bf16)
                bt = jnp.where(sub == b8[j:j + 1], 1.0, 0.0).astype(bf16)
                dn = (((1,), (1,)), ((), ()))
                gt = (lax.dot_general(g_hi, bt, dn, preferred_element_type=f32)
                      + lax.dot_general(g_lo, bt, dn, preferred_element_type=f32))
                gs_ref[pl.ds(pl.multiple_of((t8 * 8 + j) * G_PITCH, 8), PEER_NK), :] = gt
            return 0

        lax.fori_loop(0, tile // 8, tok8, 0)

    x = x_ref[...]
    acc = acc_ref[...]
    for s in range(n_exp // EXPERT_SUB):
        e0 = s * EXPERT_SUB
        act = lax.dot_general(x, u_ref[e0:e0 + EXPERT_SUB, :], (((1,), (1,)), ((), ())),
                              preferred_element_type=f32)
        i1 = (c * n_exp + e0) // PEER_NK
        gates = jnp.concatenate(
            [gs_ref[pl.ds(i1 + k, tile, stride=G_PITCH), :] for k in range(EXPERT_SUB // PEER_NK)], axis=1)
        w = (gates * _gelu_tanh(act)).astype(bf16)
        acc = acc + jnp.dot(w, v_ref[e0:e0 + EXPERT_SUB, :], preferred_element_type=f32)
    acc_ref[...] = acc

    @pl.when(c == pl.num_programs(1) - 1)
    def _():
        y_ref[...] = h_ref[...] + acc


def _peer(xn2, a, b, g, h, u_bf, v_bf, *, tile, chunk):
    n = xn2.shape[0]
    n_slots = a.shape[1]
    n_experts = u_bf.shape[0]
    tok = lambda w: pl.BlockSpec((tile, w), lambda j, c: (j, 0))
    tab = pl.BlockSpec((chunk, D_MODEL), lambda j, c: (c, 0))
    return pl.pallas_call(
        _peer_body,
        grid=(n // tile, n_experts // chunk),
        in_specs=[tok(D_MODEL), tok(n_slots), tok(n_slots), tok(n_slots), tok(D_MODEL), tab, tab],
        out_specs=tok(D_MODEL),
        out_shape=jax.ShapeDtypeStruct((n, D_MODEL), f32),
        scratch_shapes=[pltpu.VMEM((tile * G_PITCH, PEER_NK), f32), pltpu.VMEM((tile, D_MODEL), f32)],
        compiler_params=_cparams(("arbitrary", "arbitrary")),
        name="peer_experts",
    )(xn2, a, b, g, h, u_bf, v_bf)


TOK_TILE = 256
ATT_BLK = 512
RW_CHUNK = 128
RW_SAMPLE_ROWS = 128
PAGES_PER_STEP = 8
EXPERT_CHUNK = 2048


def _merge_and_peer(x2, o_fox, o_rw, o_mem, shared):
    h, xn2 = _finish(x2, o_fox, o_rw, o_mem, shared["norm1_w"], shared["w_gate"], shared["w_branch"],
                     shared["w_out"], shared["norm2_w"], tile=TOK_TILE)
    a, b, g = _route(xn2, shared["wq_t"], shared["keys1"], shared["keys2"], tile=TOK_TILE)
    return _peer(xn2, a, b, g, h, shared["u_bf"], shared["v_bf"], tile=TOK_TILE, chunk=EXPERT_CHUNK)


def kernel(x_prompt, x_sample, cache_fox_k, cache_fox_v, cache_fox_logf, cache_mem_k, cache_mem_v, state_rwkv, state_rwkv_shift, page_table, mem_prompt, norm1_w, w_in, fox_b_f, fox_qn_w, fox_kn_w, rw_mu, rw_w0, rw_w_up, rw_a0, rw_a_up, rw_g_up, rw_k_k, rw_k_a, rw_r_k, rw_ln_w, rw_ln_b, mem_norm_w, w_mem_kv, mem_qn_w, mem_kn_w, w_branch, w_out, norm2_w, peer_w_q, peer_keys1, peer_keys2, peer_u, peer_v):
    bp, sp, _ = x_prompt.shape
    db, ds, _ = x_sample.shape
    w_pack, bf_pad, qnw, knw, mqw, w_gate = _prep_inproj_weights(w_in, fox_b_f, fox_qn_w, fox_kn_w, mem_qn_w)
    g1 = norm1_w.reshape(1, D_MODEL)
    rw_w = _rwkv_weights(rw_mu, rw_w0, rw_w_up, rw_a0, rw_a_up, rw_g_up, rw_k_k, rw_k_a, rw_r_k, rw_ln_w, rw_ln_b)
    shared = dict(norm1_w=norm1_w, w_gate=w_gate, w_branch=w_branch, w_out=w_out, norm2_w=norm2_w,
                  wq_t=peer_w_q.T.astype(bf16), keys1=peer_keys1.astype(bf16), keys2=peer_keys2.astype(bf16),
                  u_bf=peer_u.astype(bf16), v_bf=peer_v.astype(bf16))

    qh, kh, vh, k_p, v_p, logf_p, c_p, ct_p, prw_p, mq_p = _inproj(
        x_prompt, g1, w_pack, bf_pad, qnw, knw, mqw, head_major=True, tile=TOK_TILE)
    o_fox = _fox_prompt(qh, kh, vh, c_p, ct_p, blk=ATT_BLK)
    o_rw, st_p = _rwkv_prompt(prw_p, rw_w, tc=RW_CHUNK)
    mem_k_p, mem_v_p = _mem_kv(mem_prompt, mem_norm_w, w_mem_kv, mem_kn_w)
    o_mem = _mem_attend(mq_p, mem_k_p, mem_v_p, tq=ATT_BLK)
    n_p = bp * sp
    y_prompt = _merge_and_peer(x_prompt.reshape(n_p, D_MODEL), o_fox.reshape(n_p, FOX_W), o_rw.reshape(n_p, RW_W),
                               o_mem.reshape(n_p, MEM_W), shared).reshape(bp, sp, D_MODEL)

    n_s = db * ds
    qt, kt, vt, k_s, v_s, logf_s, _, _, prw_s, mq_s = _inproj(
        x_sample.reshape(1, n_s, D_MODEL), g1, w_pack, bf_pad, qnw, knw, mqw, head_major=False, tile=TOK_TILE)
    pool, page = cache_fox_k.shape[:2]
    logf_s = logf_s.reshape(db, ds, LANES)[:, :, :FOX_HEADS]
    lfn_t = jnp.pad(jnp.swapaxes(logf_s, 1, 2), ((0, 0), (0, 0), (0, LANES - ds)))
    o_fox_s = _fox_sample(page_table, qt.reshape(db, ds, FOX_W), kt.reshape(db, ds, FOX_W), vt.reshape(db, ds, FOX_W),
                          lfn_t, cache_fox_k.reshape(pool, page, FOX_W), cache_fox_v.reshape(pool, page, FOX_W),
                          jnp.swapaxes(cache_fox_logf, 1, 2), n_pg=PAGES_PER_STEP)
    prw_rows = prw_s.reshape(n_s, RW_IN)
    o_rw_s, st_s = _rwkv_sample(prw_rows, jnp.repeat(state_rwkv_shift, ds, axis=0), _pack_state(state_rwkv), rw_w,
                                seq=ds, rows_per_blk=RW_SAMPLE_ROWS)
    nm = cache_mem_k.shape[1]
    o_mem_s = _mem_attend(mq_s.reshape(db, ds, MEM_W), cache_mem_k.reshape(db, nm, MEM_W),
                          cache_mem_v.reshape(db, nm, MEM_W), tq=ds)
    y_sample = _merge_and_peer(x_sample.reshape(n_s, D_MODEL), o_fox_s.reshape(n_s, FOX_W), o_rw_s,
                               o_mem_s.reshape(n_s, MEM_W), shared).reshape(db, ds, D_MODEL)

    heads = lambda a, b, s: a.reshape(b, s, FOX_HEADS, FOX_HD)
    return (y_prompt, y_sample,
            heads(k_p, bp, sp), heads(v_p, bp, sp), logf_p[:, :, :FOX_HEADS],
            _unpack_state(st_p), prw_p[:, -1],
            mem_k_p.reshape(bp, nm, MEM_HEADS, MEM_HD), mem_v_p.reshape(bp, nm, MEM_HEADS, MEM_HD),
            heads(k_s, db, ds), heads(v_s, db, ds), logf_s,
            _unpack_state(st_s), prw_rows.reshape(db, ds, RW_IN)[:, -1])
```

```python
import functools
import math

import jax
import jax.numpy as jnp
from jax import lax
from jax.experimental import pallas as pl
from jax.experimental.pallas import tpu as pltpu

f32 = jnp.float32
bf16 = jnp.bfloat16

D_MODEL = 1024
RMS_EPS = 1e-6
NEG_INF = -1e30
FOX_HEADS = 8
FOX_HD = 64
FOX_W = 512
FOX_SCALE = FOX_HD ** -0.5
RW_W = 512
RW_IN = 1792
RW_DECAY = math.exp(-0.5)
GN_EPS = 64e-5
MEM_HEADS = 4
MEM_HD = 128
MEM_W = 512
MEM_SCALE = MEM_HD ** -0.5
PEER_HEADS = 8
PEER_NK = 128
PEER_HALF = 128
PEER_TOPK = 16

LANES = 128
VMEM_LIMIT = 56 * 1024 * 1024

_PK_Q, _PK_K, _PK_V, _PK_RW, _PK_MQ, _PK_F, _PK_END = 0, 512, 1024, 1536, 3328, 3840, 3968


def _cparams(sem):
    return pltpu.CompilerParams(dimension_semantics=sem, vmem_limit_bytes=VMEM_LIMIT)


def _block_diag_ones(width, group):
    r = lax.broadcasted_iota(jnp.int32, (width, width), 0) // group
    c = lax.broadcasted_iota(jnp.int32, (width, width), 1) // group
    return (r == c).astype(bf16)


def _split_dot(x, w):
    hi = x.astype(bf16)
    lo = (x - hi.astype(f32)).astype(bf16)
    return (jnp.dot(hi, w, preferred_element_type=f32)
            + jnp.dot(lo, w, preferred_element_type=f32))


def _rms(x, g):
    return x * lax.rsqrt(jnp.mean(x * x, axis=-1, keepdims=True) + RMS_EPS) * g


def _log_sigmoid(x):
    return jnp.minimum(x, 0.0) - jnp.log1p(jnp.exp(-jnp.abs(x)))


def _sigmoid(x):
    return 1.0 / (1.0 + jnp.exp(-x))


def _inproj_body(x_ref, g_ref, w_ref, bf_ref, qnw_ref, knw_ref, mqw_ref, bd_ref, tri_ref,
                 q_ref, kb_ref, vb_ref, kp_ref, vp_ref, logf_ref, c_ref, ct_ref, prw_ref, mq_ref,
                 carry_ref, *, head_major):
    @pl.when(pl.program_id(1) == 0)
    def _():
        carry_ref[...] = jnp.zeros_like(carry_ref)

    x = x_ref[0]
    xn = _rms(x, g_ref[...]).astype(bf16)
    p = jnp.dot(xn, w_ref[...], preferred_element_type=f32)
    q = p[:, _PK_Q:_PK_K]
    k = p[:, _PK_K:_PK_V]
    v = p[:, _PK_V:_PK_RW]
    prw_ref[0] = p[:, _PK_RW:_PK_MQ]
    bd = bd_ref[...]
    qn = q * lax.rsqrt(_split_dot(q * q, bd) * (1.0 / FOX_HD) + RMS_EPS) * qnw_ref[...]
    kn = k * lax.rsqrt(_split_dot(k * k, bd) * (1.0 / FOX_HD) + RMS_EPS) * knw_ref[...]
    kp_ref[0] = kn
    vp_ref[0] = v
    qs = (qn * FOX_SCALE).astype(bf16)
    kb = kn.astype(bf16)
    vb = v.astype(bf16)
    if head_major:
        for h in range(FOX_HEADS):
            sl = slice(h * FOX_HD, (h + 1) * FOX_HD)
            q_ref[0, h] = qs[:, sl]
            kb_ref[0, h] = kb[:, sl]
            vb_ref[0, h] = vb[:, sl]
    else:
        q_ref[0] = qs
        kb_ref[0] = kb
        vb_ref[0] = vb
    for h in range(MEM_HEADS):
        sl = slice(_PK_MQ + h * MEM_HD, _PK_MQ + (h + 1) * MEM_HD)
        mq_ref[0, :, h * MEM_HD:(h + 1) * MEM_HD] = _rms(p[:, sl], mqw_ref[...]).astype(bf16)
    lane = lax.broadcasted_iota(jnp.int32, (x.shape[0], LANES), 1)
    logf = jnp.where(lane < FOX_HEADS, _log_sigmoid(p[:, _PK_F:_PK_END] + bf_ref[...]), 0.0)
    logf_ref[0] = logf
    c = _split_dot_left(tri_ref[...], logf) + carry_ref[0:1, :]
    c_ref[0] = c
    carry_ref[0:1, :] = c[x.shape[0] - 1:x.shape[0], :]
    ct_ref[0] = jnp.transpose(c)[0:FOX_HEADS, :]


def _split_dot_left(w, x):
    hi = x.astype(bf16)
    lo = (x - hi.astype(f32)).astype(bf16)
    return (jnp.dot(w, hi, preferred_element_type=f32)
            + jnp.dot(w, lo, preferred_element_type=f32))


def _inproj(x3, norm1_w, w_pack, bf_pad, qnw, knw, mqw, *, head_major, tile):
    nb, ns, _ = x3.shape
    nt = ns // tile
    bd = _block_diag_ones(FOX_W, FOX_HD)
    tri = (lax.broadcasted_iota(jnp.int32, (tile, tile), 0)
           >= lax.broadcasted_iota(jnp.int32, (tile, tile), 1)).astype(bf16)
    const = lambda shape: pl.BlockSpec(shape, lambda b, j: (0,) * len(shape))
    tok = lambda w: pl.BlockSpec((1, tile, w), lambda b, j: (b, j, 0))
    if head_major:
        qkv_shape = jax.ShapeDtypeStruct((nb, FOX_HEADS, ns, FOX_HD), bf16)
        qkv_spec = pl.BlockSpec((1, FOX_HEADS, tile, FOX_HD), lambda b, j: (b, 0, j, 0))
    else:
        qkv_shape = jax.ShapeDtypeStruct((nb, ns, FOX_W), bf16)
        qkv_spec = tok(FOX_W)
    out_shape = (
        qkv_shape, qkv_shape, qkv_shape,
        jax.ShapeDtypeStruct((nb, ns, FOX_W), f32),
        jax.ShapeDtypeStruct((nb, ns, FOX_W), f32),
        jax.ShapeDtypeStruct((nb, ns, LANES), f32),
        jax.ShapeDtypeStruct((nb, ns, LANES), f32),
        jax.ShapeDtypeStruct((nb, FOX_HEADS, ns), f32),
        jax.ShapeDtypeStruct((nb, ns, RW_IN), f32),
        jax.ShapeDtypeStruct((nb, ns, MEM_W), bf16),
    )
    out_specs = (
        qkv_spec, qkv_spec, qkv_spec, tok(FOX_W), tok(FOX_W), tok(LANES), tok(LANES),
        pl.BlockSpec((1, FOX_HEADS, tile), lambda b, j: (b, 0, j)),
        tok(RW_IN), tok(MEM_W),
    )
    return pl.pallas_call(
        functools.partial(_inproj_body, head_major=head_major),
        grid=(nb, nt),
        in_specs=[tok(D_MODEL), const((1, D_MODEL)), const((D_MODEL, _PK_END)), const((1, LANES)),
                  const((1, FOX_W)), const((1, FOX_W)), const((1, MEM_HD)), const((FOX_W, FOX_W)),
                  const((tile, tile))],
        out_specs=out_specs,
        out_shape=out_shape,
        scratch_shapes=[pltpu.VMEM((8, LANES), f32)],
        compiler_params=_cparams(("arbitrary", "arbitrary")),
        name="inproj",
    )(x3, norm1_w, w_pack, bf_pad, qnw, knw, mqw, bd, tri)


def _prep_inproj_weights(w_in, fox_b_f, fox_qn_w, fox_kn_w, mem_qn_w):
    fox_in = 3 * FOX_W + FOX_HEADS
    w_fox = w_in[:, :fox_in]
    w_rw = w_in[:, fox_in:fox_in + RW_IN]
    w_mq = w_in[:, fox_in + RW_IN:fox_in + RW_IN + MEM_W]
    w_gate = w_in[:, fox_in + RW_IN + MEM_W:]
    w_f = jnp.pad(w_fox[:, 3 * FOX_W:], ((0, 0), (0, LANES - FOX_HEADS)))
    w_pack = jnp.concatenate([w_fox[:, :3 * FOX_W], w_rw, w_mq, w_f], axis=1).astype(bf16)
    bf_pad = jnp.pad(fox_b_f, (0, LANES - FOX_HEADS)).reshape(1, LANES)
    qnw = jnp.tile(fox_qn_w, FOX_HEADS).reshape(1, FOX_W)
    knw = jnp.tile(fox_kn_w, FOX_HEADS).reshape(1, FOX_W)
    return w_pack, bf_pad, qnw, knw, mem_qn_w.reshape(1, MEM_HD), w_gate.astype(bf16)


def _fox_prompt_body(q_ref, k_ref, v_ref, c_ref, ct_ref, o_ref, *, blk):
    hp = pl.program_id(1)
    qi = pl.program_id(2)
    lane = lax.broadcasted_iota(jnp.int32, (blk, LANES), 1)
    c_blk = c_ref[0]
    row = lax.broadcasted_iota(jnp.int32, (blk, blk), 0)
    col = lax.broadcasted_iota(jnp.int32, (blk, blk), 1)
    outs = []
    for hh in range(2):
        h = hp * 2 + hh
        q = q_ref[0, hh]
        cq = jnp.sum(jnp.where(lane == h, c_blk, 0.0), axis=1, keepdims=True)

        def step(kj, carry, masked):
            m, l, acc = carry
            start = pl.multiple_of(kj * blk, blk)
            k = k_ref[0, hh, pl.ds(start, blk), :]
            v = v_ref[0, hh, pl.ds(start, blk), :]
            s = lax.dot_general(q, k, (((1,), (1,)), ((), ())), preferred_element_type=f32)
            ck = ct_ref[0, pl.ds(h, 1), pl.ds(start, blk)]
            s = s + (cq - ck)
            if masked:
                s = jnp.where(col <= row, s, NEG_INF)
            m_new = jnp.maximum(m, jnp.max(s, axis=1, keepdims=True))
            alpha = jnp.exp(m - m_new)
            p = jnp.exp(s - m_new)
            l = l * alpha + jnp.sum(p, axis=1, keepdims=True)
            acc = acc * alpha + jnp.dot(p.astype(bf16), v, preferred_element_type=f32)
            return m_new, l, acc

        init = (jnp.full((blk, 1), NEG_INF, f32), jnp.zeros((blk, 1), f32), jnp.zeros((blk, FOX_HD), f32))
        carry = lax.fori_loop(0, qi, functools.partial(step, masked=False), init)
        m, l, acc = step(qi, carry, True)
        outs.append(acc / l)
    o_ref[0] = jnp.concatenate(outs, axis=1).astype(o_ref.dtype)


def _fox_prompt(q, k, v, c, ct, *, blk):
    nb, nh, ns, hd = q.shape
    pair = lambda: pl.BlockSpec((1, 2, ns, hd), lambda b, hp, qi: (b, hp, 0, 0))
    return pl.pallas_call(
        functools.partial(_fox_prompt_body, blk=blk),
        grid=(nb, nh // 2, ns // blk),
        in_specs=[pl.BlockSpec((1, 2, blk, hd), lambda b, hp, qi: (b, hp, qi, 0)), pair(), pair(),
                  pl.BlockSpec((1, blk, LANES), lambda b, hp, qi: (b, qi, 0)),
                  pl.BlockSpec((1, nh, ns), lambda b, hp, qi: (b, 0, 0))],
        out_specs=pl.BlockSpec((1, blk, 2 * hd), lambda b, hp, qi: (b, qi, hp)),
        out_shape=jax.ShapeDtypeStruct((nb, ns, nh * hd), bf16),
        compiler_params=_cparams(("arbitrary", "arbitrary", "arbitrary")),
        name="fox_prompt",
    )(q, k, v, c, ct)


UNIT_COLS = 16
RW_PAIRS = 4


def _rwkv_body(*refs, n_blk_b, tc, n_sb, n_st, prompt_mode):
    if prompt_mode:
        (p_ref, mu_ref, w0_ref, wup_ref, a0_ref, aup_ref, gup_ref, kk_w_ref, ka_ref, rk_ref, lnw_ref, lnb_ref,
         bd_ref, wsa_ref, ob_ref, e_ref, o_ref, s_ref, prev_ref, kk_s, dec_s, beta_s, kt_s, r_s, v_s, g_s,
         v0_s, v1_s, v2_s, vts_s, ots_s, stage_s, o_s) = refs
    else:
        (p_ref, pf_ref, sin_ref, mu_ref, w0_ref, wup_ref, a0_ref, aup_ref, gup_ref, kk_w_ref, ka_ref, rk_ref,
         lnw_ref, lnb_ref, bd_ref, wsa_ref, ob_ref, e_ref, o_ref, s_ref, kk_s, dec_s, beta_s, kt_s, r_s, v_s,
         g_s, v0_s, v1_s, v2_s, vts_s, ots_s, stage_s, o_s) = refs
    n_rows = n_blk_b * tc
    n_units = n_rows // UNIT_COLS
    step = pl.program_id(0)

    if prompt_mode:
        @pl.when(step == 0)
        def _():
            s_ref[...] = jnp.zeros_like(s_ref)
            prev_ref[...] = jnp.zeros_like(prev_ref)
    else:
        s_ref[...] = sin_ref[...]
    stage_s[...] = jnp.zeros_like(stage_s)

    p = p_ref[...].reshape(n_rows, RW_IN)
    rowi = lax.broadcasted_iota(jnp.int32, (n_rows, 1), 0)
    rolled = pltpu.roll(p, 1, axis=0)
    if prompt_mode:
        prev = rolled
        for bb in range(n_blk_b):
            prev = jnp.where(rowi == bb * tc, prev_ref[bb, 0:1, :], prev)
            prev_ref[bb, 0:1, :] = p[(bb + 1) * tc - 1:(bb + 1) * tc, :]
    else:
        prev = jnp.where(rowi % tc == 0, pf_ref[...], rolled)
    xs = p + (prev - p) * mu_ref[...]
    r = xs[:, 0:512]
    k = xs[:, 512:1024]
    v = xs[:, 1024:1536]
    zwa = xs[:, 1536:1664]
    zg = xs[:, 1664:1792]
    log_w = -RW_DECAY * _sigmoid(w0_ref[...] + jnp.dot(jnp.tanh(zwa).astype(bf16), wup_ref[...],
                                                        preferred_element_type=f32))
    a = _sigmoid(a0_ref[...] + jnp.dot(zwa.astype(bf16), aup_ref[...], preferred_element_type=f32))
    g = jnp.dot(_sigmoid(zg).astype(bf16), gup_ref[...], preferred_element_type=f32)
    bd = bd_ref[...]
    kk = k * kk_w_ref[...]
    kk = kk / jnp.maximum(jnp.sqrt(_split_dot(kk * kk, bd)), 1e-12)
    kt = k * (1.0 + (a - 1.0) * ka_ref[...])
    kk_s[...] = kk
    dec_s[...] = jnp.exp(log_w)
    beta_s[...] = kk * a
    kt_s[...] = kt
    r_s[...] = r
    v_s[...] = v
    g_s[...] = g
    v0 = v.astype(bf16).astype(f32)
    v1 = (v - v0).astype(bf16).astype(f32)
    v0_s[...] = v0
    v1_s[...] = v1
    v2_s[...] = v - v0 - v1

    def unit_block(u, bb):
        if prompt_mode:
            return bb * tc + u * n_st, n_st, 0
        return u * UNIT_COLS, UNIT_COLS, bb * n_st

    def vt_body(u, _):
        for bb in range(n_sb):
            blk0, blk_n, roff = unit_block(u, bb)
            rows = pl.ds(pl.multiple_of(blk0, 8), blk_n)
            for part, ref in enumerate((v0_s, v1_s, v2_s)):
                stage_s[part * n_st:(part + 1) * n_st, :] = ref[rows, :][roff:roff + n_st]
            vts_s[u * n_sb + bb] = jnp.transpose(stage_s[...])
        return 0

    lax.fori_loop(0, n_units, vt_body, 0)

    lane = lax.broadcasted_iota(jnp.int32, (FOX_HD, LANES), 1) % FOX_HD
    pairs = [(bb, pr) for bb in range(n_sb) for pr in range(RW_PAIRS)]

    def unit_body(u, _):
        rowvecs, states, v_lhs = [], [], []
        for bb, pr in pairs:
            blk0, blk_n, roff = unit_block(u, bb)
            rows = pl.ds(pl.multiple_of(blk0, 8), blk_n)
            sl = slice(pr * LANES, (pr + 1) * LANES)
            rowvecs.append(tuple(ref[rows, sl][roff:roff + n_st] for ref in (kk_s, dec_s, beta_s, kt_s, r_s)))
            states.append(s_ref[u * n_sb + bb if not prompt_mode else bb, pr])
            vt = vts_s[u * n_sb + bb, sl, :]
            v_lhs.append(jnp.concatenate([vt[0:FOX_HD], vt[FOX_HD:LANES]], axis=1).astype(bf16))
        v_lhs = jnp.concatenate(v_lhs, axis=0)
        ots = [jnp.zeros((FOX_HD, LANES), f32) for _ in range(RW_PAIRS)]
        for j in range(n_st):
            jr = slice(j, j + 1)
            lhs = []
            for (kk_b, _, _, _, _), s in zip(rowvecs, states):
                pk = s * kk_b[jr]
                hi = pk.astype(bf16)
                lhs.append(jnp.concatenate([hi, (pk - hi.astype(f32)).astype(bf16)], axis=1))
            sa_all = jnp.dot(jnp.concatenate(lhs, axis=0), wsa_ref[...], preferred_element_type=f32)
            vb_all = jnp.dot(v_lhs, e_ref[j], preferred_element_type=f32)
            q_lhs = []
            for n, (_, w_b, be_b, kt_b, r_b) in enumerate(rowvecs):
                blk = slice(n * FOX_HD, (n + 1) * FOX_HD)
                states[n] = states[n] * w_b[jr] - sa_all[blk] * be_b[jr] + vb_all[blk] * kt_b[jr]
                q_lhs.append((states[n] * r_b[jr]).astype(bf16))
            res_all = jnp.dot(jnp.concatenate(q_lhs, axis=0), ob_ref[...], preferred_element_type=f32)
            for n, (bb, pr) in enumerate(pairs):
                ots[pr] = jnp.where(lane == bb * n_st + j, res_all[n * FOX_HD:(n + 1) * FOX_HD], ots[pr])
        for n, (bb, pr) in enumerate(pairs):
            s_ref[u * n_sb + bb if not prompt_mode else bb, pr] = states[n]
        for pr in range(RW_PAIRS):
            ots_s[u, pr * FOX_HD:(pr + 1) * FOX_HD, :] = ots[pr]
        return 0

    lax.fori_loop(0, n_units, unit_body, 0)

    def ot_body(u, _):
        t = jnp.transpose(ots_s[u])
        ta, tb = t[0:UNIT_COLS], t[FOX_HD:FOX_HD + UNIT_COLS]
        o16 = jnp.concatenate(
            [x[:, pr * FOX_HD:(pr + 1) * FOX_HD] for pr in range(RW_PAIRS) for x in (ta, tb)], axis=1)
        if prompt_mode:
            for bb in range(n_sb):
                o_s[pl.ds(pl.multiple_of(bb * tc + u * n_st, 8), n_st), :] = o16[bb * n_st:(bb + 1) * n_st]
        else:
            o_s[pl.ds(pl.multiple_of(u * UNIT_COLS, 8), UNIT_COLS), :] = o16
        return 0

    lax.fori_loop(0, n_units, ot_body, 0)

    o = o_s[...]
    inv = 1.0 / FOX_HD
    mean = _split_dot(o, bd) * inv
    d = o - mean
    var = _split_dot(d * d, bd) * inv
    o_gn = d * lax.rsqrt(var + GN_EPS) * lnw_ref[...] + lnb_ref[...]
    bonus = _split_dot(r_s[...] * kt_s[...] * rk_ref[...], bd) * v_s[...]
    out = (o_gn + bonus) * g_s[...]
    o_ref[...] = out.reshape(o_ref.shape).astype(o_ref.dtype)


def _rwkv_weights(rw_mu, rw_w0, rw_w_up, rw_a0, rw_a_up, rw_g_up, rw_k_k, rw_k_a, rw_r_k, rw_ln_w, rw_ln_b):
    row = lambda a: a.reshape(1, -1)
    wup = jnp.concatenate([rw_w_up, jnp.zeros_like(rw_a_up)], axis=0).astype(bf16)
    aup = jnp.concatenate([jnp.zeros_like(rw_w_up), rw_a_up], axis=0).astype(bf16)
    return (row(rw_mu), row(rw_w0), wup, row(rw_a0), aup, rw_g_up.astype(bf16), row(rw_k_k), row(rw_k_a),
            row(rw_r_k), row(rw_ln_w), row(rw_ln_b), _block_diag_ones(RW_W, FOX_HD))


def _rwkv_scratch(n_rows, n_sb):
    n_units = n_rows // UNIT_COLS
    rows = [pltpu.VMEM((n_rows, RW_W), f32) for _ in range(10)]
    return rows + [pltpu.VMEM((n_units * n_sb, RW_W, LANES), f32),
                   pltpu.VMEM((n_units, RW_PAIRS * FOX_HD, LANES), f32),
                   pltpu.VMEM((LANES, RW_W), f32), pltpu.VMEM((n_rows, RW_W), f32)]


def _rwkv_selectors(n_st):
    ob = _block_diag_ones(LANES, FOX_HD)
    wsa = jnp.concatenate([ob, ob], axis=0)
    r = lax.broadcasted_iota(jnp.int32, (n_st, 2 * LANES, LANES), 1)
    c = lax.broadcasted_iota(jnp.int32, (n_st, 2 * LANES, LANES), 2)
    j = lax.broadcasted_iota(jnp.int32, (n_st, 2 * LANES, LANES), 0)
    col = r % LANES
    e = ((col < 3 * n_st) & (col % n_st == j) & (r // LANES == c // FOX_HD)).astype(bf16)
    return wsa, ob, e


def _rwkv_prompt(prw, weights, *, tc):
    nb, ns, _ = prw.shape
    const = lambda a: pl.BlockSpec(a.shape, lambda j: (0,) * a.ndim)
    n_st = UNIT_COLS // nb
    weights = tuple(weights) + _rwkv_selectors(n_st)
    return pl.pallas_call(
        functools.partial(_rwkv_body, n_blk_b=nb, tc=tc, n_sb=nb, n_st=n_st, prompt_mode=True),
        grid=(ns // tc,),
        in_specs=[pl.BlockSpec((nb, tc, RW_IN), lambda j: (0, j, 0))] + [const(w) for w in weights],
        out_specs=(pl.BlockSpec((nb, tc, RW_W), lambda j: (0, j, 0)),
                   pl.BlockSpec((nb, RW_PAIRS, FOX_HD, LANES), lambda j: (0, 0, 0, 0))),
        out_shape=(jax.ShapeDtypeStruct((nb, ns, RW_W), bf16),
                   jax.ShapeDtypeStruct((nb, RW_PAIRS, FOX_HD, LANES), f32)),
        scratch_shapes=[pltpu.VMEM((nb, 8, RW_IN), f32)] + _rwkv_scratch(nb * tc, nb),
        compiler_params=_cparams(("arbitrary",)),
        name="rwkv_prompt",
    )(prw, *weights)


def _rwkv_sample(prw_rows, prev_first_rows, state, weights, *, seq, rows_per_blk):
    n_rows = prw_rows.shape[0]
    nb_blk = rows_per_blk // seq
    const = lambda a: pl.BlockSpec(a.shape, lambda j: (0,) * a.ndim)
    n_sb = UNIT_COLS // seq
    weights = tuple(weights) + _rwkv_selectors(seq)
    return pl.pallas_call(
        functools.partial(_rwkv_body, n_blk_b=nb_blk, tc=seq, n_sb=n_sb, n_st=seq, prompt_mode=False),
        grid=(n_rows // rows_per_blk,),
        in_specs=[pl.BlockSpec((rows_per_blk, RW_IN), lambda j: (j, 0)),
                  pl.BlockSpec((rows_per_blk, RW_IN), lambda j: (j, 0)),
                  pl.BlockSpec((nb_blk, RW_PAIRS, FOX_HD, LANES), lambda j: (j, 0, 0, 0))]
                 + [const(w) for w in weights],
        out_specs=(pl.BlockSpec((rows_per_blk, RW_W), lambda j: (j, 0)),
                   pl.BlockSpec((nb_blk, RW_PAIRS, FOX_HD, LANES), lambda j: (j, 0, 0, 0))),
        out_shape=(jax.ShapeDtypeStruct((n_rows, RW_W), bf16),
                   jax.ShapeDtypeStruct((n_rows // seq, RW_PAIRS, FOX_HD, LANES), f32)),
        scratch_shapes=_rwkv_scratch(rows_per_blk, n_sb),
        compiler_params=_cparams(("arbitrary",)),
        name="rwkv_sample",
    )(prw_rows, prev_first_rows, state, *weights)


def _pack_state(state):
    b = state.shape[0]
    return state.reshape(b, RW_PAIRS, 2, FOX_HD, FOX_HD).transpose(0, 1, 3, 2, 4).reshape(b, RW_PAIRS, FOX_HD, LANES)


def _unpack_state(packed):
    b = packed.shape[0]
    return packed.reshape(b, RW_PAIRS, FOX_HD, 2, FOX_HD).transpose(0, 1, 3, 2, 4).reshape(b, 2 * RW_PAIRS, FOX_HD, FOX_HD)


def _fox_sample_body(pt_ref, *refs, n_pg, n_q):
    del pt_ref
    q_ref, kn_ref, vn_ref, lfn_ref, tflat_ref, tile_ref = refs[:6]
    k_refs = refs[6:6 + n_pg]
    v_refs = refs[6 + n_pg:6 + 2 * n_pg]
    lf_refs = refs[6 + 2 * n_pg:6 + 3 * n_pg]
    o_ref, m_s, l_s, acc_s, carry_s = refs[6 + 3 * n_pg:]
    jg = pl.program_id(1)
    n_rows = n_q * FOX_HEADS
    page = tflat_ref.shape[0]

    @pl.when(jg == 0)
    def _():
        m_s[...] = jnp.full_like(m_s, NEG_INF)
        l_s[...] = jnp.zeros_like(l_s)
        acc_s[...] = jnp.zeros_like(acc_s)
        carry_s[...] = jnp.zeros_like(carry_s)

    q = q_ref[0]
    own_head = (lax.broadcasted_iota(jnp.int32, (n_rows, page), 0) % FOX_HEADS
                == lax.broadcasted_iota(jnp.int32, (n_rows, page), 1) % FOX_HEADS)

    def cum_rows(lf_rows):
        c_local = _split_dot(lf_rows, tflat_ref[...])
        return c_local, _split_dot(c_local[:, page - LANES:], tile_ref[...])

    def update(s_all, vs):
        m_old = m_s[...]
        m_new = jnp.maximum(m_old, jnp.max(s_all, axis=1, keepdims=True))
        alpha = jnp.exp(m_old - m_new)
        p = jnp.exp(s_all - m_new)
        l_s[...] = l_s[...] * alpha + jnp.sum(p, axis=1, keepdims=True)
        acc = acc_s[...] * alpha
        for g, vv in enumerate(vs):
            acc = acc + jnp.dot(p[:, g * page:(g + 1) * page].astype(bf16), vv, preferred_element_type=f32)
        acc_s[...] = acc
        m_s[...] = m_new

    c_local, tot = cum_rows(jnp.concatenate([lf_refs[g][0] for g in range(n_pg)], axis=0))
    running = carry_s[0:1, :]
    s_list = []
    for g in range(n_pg):
        kp = k_refs[g][0].astype(bf16)
        s = lax.dot_general(q, kp, (((1,), (1,)), ((), ())), preferred_element_type=f32)
        s_list.append(jnp.where(own_head, s - (c_local[g:g + 1] + running), NEG_INF))
        running = running + tot[g:g + 1]
    carry_s[0:1, :] = running
    update(jnp.concatenate(s_list, axis=1), [v_refs[g][0].astype(bf16) for g in range(n_pg)])

    @pl.when(jg == pl.num_programs(1) - 1)
    def _():
        pad = jnp.zeros((page - n_rows, FOX_HD), bf16)
        kn = jnp.concatenate([kn_ref[0], pad], axis=0)
        vn = jnp.concatenate([vn_ref[0], pad], axis=0)
        s = lax.dot_general(q, kn, (((1,), (1,)), ((), ())), preferred_element_type=f32)
        lfn = jnp.concatenate([lfn_ref[0], jnp.zeros((7, page), f32)], axis=0)
        c_new = cum_rows(lfn)[0][0:1] + carry_s[0:1, :]
        kpos = lax.broadcasted_iota(jnp.int32, (n_rows, page), 1) // FOX_HEADS
        qpos = lax.broadcasted_iota(jnp.int32, (n_rows, page), 0) // FOX_HEADS
        update(jnp.where(own_head & (kpos <= qpos), s - c_new, NEG_INF), [vn])
        o_ref[0] = (acc_s[...] / l_s[...]).astype(o_ref.dtype)


def _fox_sample(page_table, q, kn, vn, lfn, cache_k, cache_v, cache_lf, *, n_pg):
    nb, n_rows, _ = q.shape
    n_q = n_rows // FOX_HEADS
    n_pages = page_table.shape[1]
    page = cache_k.shape[1]
    r = lax.broadcasted_iota(jnp.int32, (page, page), 0)
    c = lax.broadcasted_iota(jnp.int32, (page, page), 1)
    tflat = ((r % FOX_HEADS == c % FOX_HEADS) & (r <= c)).astype(bf16)
    tr = lax.broadcasted_iota(jnp.int32, (LANES, page), 0)
    tc = lax.broadcasted_iota(jnp.int32, (LANES, page), 1)
    tile = (tr - (LANES - FOX_HEADS) == tc % FOX_HEADS).astype(bf16)
    per_b = lambda shape: pl.BlockSpec((1,) + shape, lambda b, j, pt: (b, 0, 0))
    const = lambda a: pl.BlockSpec(a.shape, lambda b, j, pt: (0, 0))
    paged = lambda shape, g: pl.BlockSpec((1,) + shape, lambda b, j, pt: (pt[b, j * n_pg + g], 0, 0))
    in_specs = ([per_b((n_rows, FOX_HD))] * 3 + [per_b((1, page)), const(tflat), const(tile)]
                + [paged((page, FOX_HD), g) for g in range(n_pg)]
                + [paged((page, FOX_HD), g) for g in range(n_pg)]
                + [paged((1, page), g) for g in range(n_pg)])
    grid_spec = pltpu.PrefetchScalarGridSpec(
        num_scalar_prefetch=1, grid=(nb, n_pages // n_pg), in_specs=in_specs,
        out_specs=pl.BlockSpec((1, n_rows, FOX_HD), lambda b, j, pt: (b, 0, 0)),
        scratch_shapes=[pltpu.VMEM((n_rows, 1), f32), pltpu.VMEM((n_rows, 1), f32),
                        pltpu.VMEM((n_rows, FOX_HD), f32), pltpu.VMEM((8, page), f32)])
    return pl.pallas_call(
        functools.partial(_fox_sample_body, n_pg=n_pg, n_q=n_q),
        grid_spec=grid_spec,
        out_shape=jax.ShapeDtypeStruct((nb, n_rows, FOX_HD), bf16),
        compiler_params=_cparams(("arbitrary", "arbitrary")),
        name="fox_sample",
    )(page_table, q, kn, vn, lfn, tflat, tile, *([cache_k] * n_pg), *([cache_v] * n_pg), *([cache_lf] * n_pg))


def _mem_kv_body(m_ref, g_ref, w_ref, knw_ref, k_ref, v_ref):
    n = _rms(m_ref[0], g_ref[...]).astype(bf16)
    kv = jnp.dot(n, w_ref[...], preferred_element_type=f32)
    for h in range(MEM_HEADS):
        sl = slice(h * MEM_HD, (h + 1) * MEM_HD)
        k_ref[0, :, sl] = _rms(kv[:, sl], knw_ref[...])
    v_ref[0] = kv[:, MEM_W:]


def _mem_kv(mem, mem_norm_w, w_mem_kv, mem_kn_w):
    nb, nm, _ = mem.shape
    const = lambda shape: pl.BlockSpec(shape, lambda b: (0,) * len(shape))
    out = jax.ShapeDtypeStruct((nb, nm, MEM_W), f32)
    return pl.pallas_call(
        _mem_kv_body,
        grid=(nb,),
        in_specs=[pl.BlockSpec((1, nm, D_MODEL), lambda b: (b, 0, 0)), const((1, D_MODEL)),
                  const((D_MODEL, 2 * MEM_W)), const((1, MEM_HD))],
        out_specs=(pl.BlockSpec((1, nm, MEM_W), lambda b: (b, 0, 0)),) * 2,
        out_shape=(out, out),
        compiler_params=_cparams(("arbitrary",)),
        name="mem_kv",
    )(mem, mem_norm_w.reshape(1, D_MODEL), w_mem_kv.astype(bf16), mem_kn_w.reshape(1, MEM_HD))


def _mem_attend_body(q_ref, k_ref, v_ref, o_ref):
    q = q_ref[0]
    for h in range(MEM_HEADS):
        sl = slice(h * MEM_HD, (h + 1) * MEM_HD)
        k = k_ref[0, :, sl].astype(bf16)
        v = v_ref[0, :, sl].astype(bf16)
        s = lax.dot_general(q[:, sl], k, (((1,), (1,)), ((), ())), preferred_element_type=f32) * MEM_SCALE
        e = jnp.exp(s - jnp.max(s, axis=1, keepdims=True))
        p = e / jnp.sum(e, axis=1, keepdims=True)
        o_ref[0, :, sl] = jnp.dot(p.astype(bf16), v, preferred_element_type=f32).astype(o_ref.dtype)


def _mem_attend(q, mk, mv, *, tq):
    nb, ns, _ = q.shape
    nm = mk.shape[1]
    kv = pl.BlockSpec((1, nm, MEM_W), lambda b, j: (b, 0, 0))
    return pl.pallas_call(
        _mem_attend_body,
        grid=(nb, ns // tq),
        in_specs=[pl.BlockSpec((1, tq, MEM_W), lambda b, j: (b, j, 0)), kv, kv],
        out_specs=pl.BlockSpec((1, tq, MEM_W), lambda b, j: (b, j, 0)),
        out_shape=jax.ShapeDtypeStruct((nb, ns, MEM_W), bf16),
        compiler_params=_cparams(("arbitrary", "arbitrary")),
        name="mem_attend",
    )(q, mk, mv)


def _finish_body(x_ref, of_ref, or_ref, om_ref, g1_ref, wg_ref, wb_ref, wo_ref, g2_ref, h_ref, xn2_ref):
    x = x_ref[...]
    xn = _rms(x, g1_ref[...]).astype(bf16)
    merged = jnp.zeros_like(x)
    for n, br in enumerate((of_ref, or_ref, om_ref)):
        gate = _sigmoid(jnp.dot(xn, wg_ref[:, n * D_MODEL:(n + 1) * D_MODEL], preferred_element_type=f32))
        merged = merged + gate * jnp.dot(br[...], wb_ref[n], preferred_element_type=f32)
    h = x + jnp.dot(merged.astype(bf16), wo_ref[...], preferred_element_type=f32)
    h_ref[...] = h
    xn2_ref[...] = _rms(h, g2_ref[...]).astype(bf16)


def _finish(x, o_fox, o_rw, o_mem, norm1_w, w_gate, w_branch, w_out, norm2_w, *, tile):
    n = x.shape[0]
    const = lambda shape: pl.BlockSpec(shape, lambda j: (0,) * len(shape))
    tok = lambda w: pl.BlockSpec((tile, w), lambda j: (j, 0))
    return pl.pallas_call(
        _finish_body,
        grid=(n // tile,),
        in_specs=[tok(D_MODEL), tok(FOX_W), tok(RW_W), tok(MEM_W), const((1, D_MODEL)),
                  const((D_MODEL, 3 * D_MODEL)), const((3, FOX_W, D_MODEL)), const((D_MODEL, D_MODEL)),
                  const((1, D_MODEL))],
        out_specs=(tok(D_MODEL), tok(D_MODEL)),
        out_shape=(jax.ShapeDtypeStruct((n, D_MODEL), f32), jax.ShapeDtypeStruct((n, D_MODEL), bf16)),
        compiler_params=_cparams(("arbitrary",)),
        name="finish",
    )(x, o_fox, o_rw, o_mem, norm1_w.reshape(1, D_MODEL), w_gate, w_branch.astype(bf16), w_out.astype(bf16),
      norm2_w.reshape(1, D_MODEL))


_PEER_SLOTS = [(ra, rb) for ra in range(PEER_TOPK) for rb in range(PEER_TOPK) if (ra + 1) * (rb + 1) <= PEER_TOPK]
_PEER_SLOT_ROWS = -(-len(_PEER_SLOTS) // 8) * 8


def _top_rows(s, n_take):
    iota = lax.broadcasted_iota(jnp.int32, s.shape, 0).astype(f32)
    big = float(s.shape[0])
    vals, idxs = [], []
    for _ in range(n_take):
        m = jnp.max(s, axis=0, keepdims=True)
        idx = jnp.min(jnp.where(s == m, iota, big), axis=0, keepdims=True)
        vals.append(m)
        idxs.append(idx)
        s = jnp.where(iota == idx, -jnp.inf, s)
    return vals, idxs


def _route_body(x_ref, wq_ref, k1_ref, k2_ref, a_ref, b_ref, g_ref,
                q_s, cand_s, ea_s, eb_s, a_s, b_s, g_s, at_s, bt_s, gt_s):
    q_s[...] = lax.dot_general(wq_ref[...], x_ref[...], (((1,), (1,)), ((), ())),
                               preferred_element_type=f32).astype(bf16)
    cand_s[...] = jnp.full_like(cand_s, -jnp.inf)
    ea_s[...] = jnp.zeros_like(ea_s)
    eb_s[...] = jnp.zeros_like(eb_s)

    def head(h, _):
        base = pl.multiple_of(h * 2 * PEER_HALF, 2 * PEER_HALF)
        s1 = jnp.dot(k1_ref[h], q_s[pl.ds(base, PEER_HALF), :], preferred_element_type=f32)
        s2 = jnp.dot(k2_ref[h], q_s[pl.ds(base + PEER_HALF, PEER_HALF), :], preferred_element_type=f32)
        v1, i1 = _top_rows(s1, PEER_TOPK)
        v2, i2 = _top_rows(s2, PEER_TOPK)
        for slot, (ra, rb) in enumerate(_PEER_SLOTS):
            cand_s[slot:slot + 1, :] = v1[ra] + v2[rb]
            ea_s[slot:slot + 1, :] = i1[ra]
            eb_s[slot:slot + 1, :] = i2[rb]
        cand = cand_s[...]
        ea = ea_s[...]
        eb = eb_s[...]
        iota = lax.broadcasted_iota(jnp.int32, cand.shape, 0).astype(f32)
        scs = []
        for r in range(PEER_TOPK):
            m = jnp.max(cand, axis=0, keepdims=True)
            slot = jnp.min(jnp.where(cand == m, iota, float(_PEER_SLOT_ROWS)), axis=0, keepdims=True)
            hit = iota == slot
            a_s[r:r + 1, :] = jnp.sum(jnp.where(hit, ea, 0.0), axis=0, keepdims=True)
            b_s[r:r + 1, :] = jnp.sum(jnp.where(hit, eb, 0.0), axis=0, keepdims=True)
            cand = jnp.where(hit, -jnp.inf, cand)
            scs.append(m)
        es = [jnp.exp(sc - scs[0]) for sc in scs]
        z = es[0]
        for e in es[1:]:
            z = z + e
        for r in range(PEER_TOPK):
            g_s[r:r + 1, :] = es[r] / z
        rows = pl.ds(pl.multiple_of(h * PEER_TOPK, PEER_TOPK), PEER_TOPK)
        at_s[rows, :] = a_s[...]
        bt_s[rows, :] = b_s[...]
        gt_s[rows, :] = g_s[...]
        return 0

    lax.fori_loop(0, PEER_HEADS, head, 0)
    a_ref[...] = jnp.transpose(at_s[...])
    b_ref[...] = jnp.transpose(bt_s[...])
    g_ref[...] = jnp.transpose(gt_s[...])


def _route(xn2, wq_t, keys1, keys2, *, tile):
    n = xn2.shape[0]
    n_slots = PEER_HEADS * PEER_TOPK
    const = lambda shape: pl.BlockSpec(shape, lambda j: (0,) * len(shape))
    out = jax.ShapeDtypeStruct((n, n_slots), f32)
    ospec = pl.BlockSpec((tile, n_slots), lambda j: (j, 0))
    cand = lambda: pltpu.VMEM((_PEER_SLOT_ROWS, tile), f32)
    top = lambda: pltpu.VMEM((PEER_TOPK, tile), f32)
    full = lambda: pltpu.VMEM((n_slots, tile), f32)
    return pl.pallas_call(
        _route_body,
        grid=(n // tile,),
        in_specs=[pl.BlockSpec((tile, D_MODEL), lambda j: (j, 0)), const(wq_t.shape), const(keys1.shape),
                  const(keys2.shape)],
        out_specs=(ospec, ospec, ospec),
        out_shape=(out, out, out),
        scratch_shapes=[pltpu.VMEM((wq_t.shape[0], tile), bf16), cand(), cand(), cand(), top(), top(), top(),
                        full(), full(), full()],
        compiler_params=_cparams(("arbitrary",)),
        name="peer_route",
    )(xn2, wq_t, keys1, keys2)


G_PITCH = PEER_NK + 8
EXPERT_SUB = 512


def _gelu_tanh(x):
    return 0.5 * x * (1.0 + jnp.tanh(math.sqrt(2.0 / math.pi) * (x + 0.044715 * (x * x * x))))


def _peer_body(x_ref, a_ref, b_ref, g_ref, h_ref, u_ref, v_ref, y_ref, gs_ref, acc_ref):
    tile = x_ref.shape[0]
    c = pl.program_id(1)
    n_exp = u_ref.shape[0]

    @pl.when(c == 0)
    def _():
        acc_ref[...] = jnp.zeros_like(acc_ref)
        sub = lax.broadcasted_iota(jnp.int32, (PEER_NK, LANES), 0).astype(f32)

        def tok8(t8, _):
            rows = pl.ds(pl.multiple_of(t8 * 8, 8), 8)
            a8, b8, g8 = a_ref[rows, :], b_ref[rows, :], g_ref[rows, :]
            for j in range(8):
                hit_a = sub == a8[j:j + 1]
                gate = jnp.where(hit_a, g8[j:j + 1], 0.0)
                g_hi = gate.astype(bf16)
                g_lo = (gate - g_hi.astype(f32)).astype(bf16)
                bt = jnp.where(sub == b8[j:j + 1], 1.0, 0.0).astype(bf16)
                gt = lax.dot_general(jnp.concatenate([g_hi, g_lo], axis=1), jnp.concatenate([bt, bt], axis=1),
                                     (((1,), (1,)), ((), ())), preferred_element_type=f32)
                gs_ref[pl.ds(pl.multiple_of((t8 * 8 + j) * G_PITCH, 8), PEER_NK), :] = gt
            return 0

        lax.fori_loop(0, tile // 8, tok8, 0)

    x = x_ref[...]
    acc = acc_ref[...]
    for s in range(n_exp // EXPERT_SUB):
        e0 = s * EXPERT_SUB
        act = lax.dot_general(x, u_ref[e0:e0 + EXPERT_SUB, :], (((1,), (1,)), ((), ())),
                              preferred_element_type=f32)
        i1 = (c * n_exp + e0) // PEER_NK
        gates = jnp.concatenate(
            [gs_ref[pl.ds(i1 + k, tile, stride=G_PITCH), :] for k in range(EXPERT_SUB // PEER_NK)], axis=1)
        w = (gates * _gelu_tanh(act)).astype(bf16)
        acc = acc + jnp.dot(w, v_ref[e0:e0 + EXPERT_SUB, :], preferred_element_type=f32)
    acc_ref[...] = acc

    @pl.when(c == pl.num_programs(1) - 1)
    def _():
        y_ref[...] = h_ref[...] + acc


def _peer(xn2, a, b, g, h, u_bf, v_bf, *, tile, chunk):
    n = xn2.shape[0]
    n_slots = a.shape[1]
    n_experts = u_bf.shape[0]
    tok = lambda w: pl.BlockSpec((tile, w), lambda j, c: (j, 0))
    tab = pl.BlockSpec((chunk, D_MODEL), lambda j, c: (c, 0))
    return pl.pallas_call(
        _peer_body,
        grid=(n // tile, n_experts // chunk),
        in_specs=[tok(D_MODEL), tok(n_slots), tok(n_slots), tok(n_slots), tok(D_MODEL), tab, tab],
        out_specs=tok(D_MODEL),
        out_shape=jax.ShapeDtypeStruct((n, D_MODEL), f32),
        scratch_shapes=[pltpu.VMEM((tile * G_PITCH, PEER_NK), f32), pltpu.VMEM((tile, D_MODEL), f32)],
        compiler_params=_cparams(("arbitrary", "arbitrary")),
        name="peer_experts",
    )(xn2, a, b, g, h, u_bf, v_bf)


TOK_TILE = 256
ATT_BLK = 512
RW_CHUNK = 128
RW_SAMPLE_ROWS = 128
PAGES_PER_STEP = 16
EXPERT_CHUNK = 2048


def _merge_and_peer(x2, o_fox, o_rw, o_mem, shared):
    h, xn2 = _finish(x2, o_fox, o_rw, o_mem, shared["norm1_w"], shared["w_gate"], shared["w_branch"],
                     shared["w_out"], shared["norm2_w"], tile=TOK_TILE)
    a, b, g = _route(xn2, shared["wq_t"], shared["keys1"], shared["keys2"], tile=TOK_TILE)
    return _peer(xn2, a, b, g, h, shared["u_bf"], shared["v_bf"], tile=TOK_TILE, chunk=EXPERT_CHUNK)


def kernel(x_prompt, x_sample, cache_fox_k, cache_fox_v, cache_fox_logf, cache_mem_k, cache_mem_v, state_rwkv, state_rwkv_shift, page_table, mem_prompt, norm1_w, w_in, fox_b_f, fox_qn_w, fox_kn_w, rw_mu, rw_w0, rw_w_up, rw_a0, rw_a_up, rw_g_up, rw_k_k, rw_k_a, rw_r_k, rw_ln_w, rw_ln_b, mem_norm_w, w_mem_kv, mem_qn_w, mem_kn_w, w_branch, w_out, norm2_w, peer_w_q, peer_keys1, peer_keys2, peer_u, peer_v):
    bp, sp, _ = x_prompt.shape
    db, ds, _ = x_sample.shape
    w_pack, bf_pad, qnw, knw, mqw, w_gate = _prep_inproj_weights(w_in, fox_b_f, fox_qn_w, fox_kn_w, mem_qn_w)
    g1 = norm1_w.reshape(1, D_MODEL)
    rw_w = _rwkv_weights(rw_mu, rw_w0, rw_w_up, rw_a0, rw_a_up, rw_g_up, rw_k_k, rw_k_a, rw_r_k, rw_ln_w, rw_ln_b)
    shared = dict(norm1_w=norm1_w, w_gate=w_gate, w_branch=w_branch, w_out=w_out, norm2_w=norm2_w,
                  wq_t=peer_w_q.T.astype(bf16), keys1=peer_keys1.astype(bf16), keys2=peer_keys2.astype(bf16),
                  u_bf=peer_u.astype(bf16), v_bf=peer_v.astype(bf16))

    qh, kh, vh, k_p, v_p, logf_p, c_p, ct_p, prw_p, mq_p = _inproj(
        x_prompt, g1, w_pack, bf_pad, qnw, knw, mqw, head_major=True, tile=TOK_TILE)
    o_fox = _fox_prompt(qh, kh, vh, c_p, ct_p, blk=ATT_BLK)
    o_rw, st_p = _rwkv_prompt(prw_p, rw_w, tc=RW_CHUNK)
    mem_k_p, mem_v_p = _mem_kv(mem_prompt, mem_norm_w, w_mem_kv, mem_kn_w)
    o_mem = _mem_attend(mq_p, mem_k_p, mem_v_p, tq=ATT_BLK)
    n_p = bp * sp
    y_prompt = _merge_and_peer(x_prompt.reshape(n_p, D_MODEL), o_fox.reshape(n_p, FOX_W), o_rw.reshape(n_p, RW_W),
                               o_mem.reshape(n_p, MEM_W), shared).reshape(bp, sp, D_MODEL)

    n_s = db * ds
    qt, kt, vt, k_s, v_s, logf_s, _, _, prw_s, mq_s = _inproj(
        x_sample.reshape(1, n_s, D_MODEL), g1, w_pack, bf_pad, qnw, knw, mqw, head_major=False, tile=TOK_TILE)
    pool, page = cache_fox_k.shape[:2]
    logf_s = logf_s.reshape(db, ds, LANES)[:, :, :FOX_HEADS]
    rows = page * FOX_HEADS
    lfn = jnp.pad(logf_s.reshape(db, 1, ds * FOX_HEADS), ((0, 0), (0, 0), (0, rows - ds * FOX_HEADS)))
    per_head = lambda a: a.reshape(db, ds * FOX_HEADS, FOX_HD)
    o_fox_s = _fox_sample(page_table, per_head(qt), per_head(kt), per_head(vt), lfn,
                          cache_fox_k.reshape(pool, rows, FOX_HD), cache_fox_v.reshape(pool, rows, FOX_HD),
                          cache_fox_logf.reshape(pool, 1, rows), n_pg=PAGES_PER_STEP)
    prw_rows = prw_s.reshape(n_s, RW_IN)
    o_rw_s, st_s = _rwkv_sample(prw_rows, jnp.repeat(state_rwkv_shift, ds, axis=0), _pack_state(state_rwkv), rw_w,
                                seq=ds, rows_per_blk=RW_SAMPLE_ROWS)
    nm = cache_mem_k.shape[1]
    o_mem_s = _mem_attend(mq_s.reshape(db, ds, MEM_W), cache_mem_k.reshape(db, nm, MEM_W),
                          cache_mem_v.reshape(db, nm, MEM_W), tq=ds)
    y_sample = _merge_and_peer(x_sample.reshape(n_s, D_MODEL), o_fox_s.reshape(n_s, FOX_W), o_rw_s,
                               o_mem_s.reshape(n_s, MEM_W), shared).reshape(db, ds, D_MODEL)

    heads = lambda a, b, s: a.reshape(b, s, FOX_HEADS, FOX_HD)
    return (y_prompt, y_sample,
            heads(k_p, bp, sp), heads(v_p, bp, sp), logf_p[:, :, :FOX_HEADS],
            _unpack_state(st_p), prw_p[:, -1],
            mem_k_p.reshape(bp, nm, MEM_HEADS, MEM_HD), mem_v_p.reshape(bp, nm, MEM_HEADS, MEM_HD),
            heads(k_s, db, ds), heads(v_s, db, ds), logf_s,
            _unpack_state(st_s), prw_rows.reshape(db, ds, RW_IN)[:, -1])
```

```python
import functools
import math

import jax
import jax.numpy as jnp
from jax import lax
from jax.experimental import pallas as pl
from jax.experimental.pallas import tpu as pltpu

f32 = jnp.float32
bf16 = jnp.bfloat16

D_MODEL = 1024
RMS_EPS = 1e-6
NEG_INF = -1e30
FOX_HEADS = 8
FOX_HD = 64
FOX_W = 512
FOX_SCALE = FOX_HD ** -0.5
RW_W = 512
RW_IN = 1792
RW_DECAY = math.exp(-0.5)
GN_EPS = 64e-5
MEM_HEADS = 4
MEM_HD = 128
MEM_W = 512
MEM_SCALE = MEM_HD ** -0.5
PEER_HEADS = 8
PEER_NK = 128
PEER_HALF = 128
PEER_TOPK = 16

LANES = 128
VMEM_LIMIT = 56 * 1024 * 1024

_PK_Q, _PK_K, _PK_V, _PK_RW, _PK_MQ, _PK_F, _PK_END = 0, 512, 1024, 1536, 3328, 3840, 3968


def _cparams(sem):
    return pltpu.CompilerParams(dimension_semantics=sem, vmem_limit_bytes=VMEM_LIMIT)


def _block_diag_ones(width, group):
    r = lax.broadcasted_iota(jnp.int32, (width, width), 0) // group
    c = lax.broadcasted_iota(jnp.int32, (width, width), 1) // group
    return (r == c).astype(bf16)


def _split_dot(x, w):
    hi = x.astype(bf16)
    lo = (x - hi.astype(f32)).astype(bf16)
    return (jnp.dot(hi, w, preferred_element_type=f32)
            + jnp.dot(lo, w, preferred_element_type=f32))


def _rms(x, g):
    return x * lax.rsqrt(jnp.mean(x * x, axis=-1, keepdims=True) + RMS_EPS) * g


def _log_sigmoid(x):
    return jnp.minimum(x, 0.0) - jnp.log1p(jnp.exp(-jnp.abs(x)))


def _sigmoid(x):
    return 1.0 / (1.0 + jnp.exp(-x))


def _inproj_body(x_ref, g_ref, w_ref, bf_ref, qnw_ref, knw_ref, mqw_ref, bd_ref, tri_ref,
                 q_ref, kb_ref, vb_ref, kp_ref, vp_ref, logf_ref, c_ref, ct_ref, prw_ref, mq_ref,
                 carry_ref, *, head_major):
    @pl.when(pl.program_id(1) == 0)
    def _():
        carry_ref[...] = jnp.zeros_like(carry_ref)

    x = x_ref[0]
    xn = _rms(x, g_ref[...]).astype(bf16)
    p = jnp.dot(xn, w_ref[...], preferred_element_type=f32)
    q = p[:, _PK_Q:_PK_K]
    k = p[:, _PK_K:_PK_V]
    v = p[:, _PK_V:_PK_RW]
    prw_ref[0] = p[:, _PK_RW:_PK_MQ]
    bd = bd_ref[...]
    qn = q * lax.rsqrt(_split_dot(q * q, bd) * (1.0 / FOX_HD) + RMS_EPS) * qnw_ref[...]
    kn = k * lax.rsqrt(_split_dot(k * k, bd) * (1.0 / FOX_HD) + RMS_EPS) * knw_ref[...]
    kp_ref[0] = kn
    vp_ref[0] = v
    qs = (qn * FOX_SCALE).astype(bf16)
    kb = kn.astype(bf16)
    vb = v.astype(bf16)
    if head_major:
        for h in range(FOX_HEADS):
            sl = slice(h * FOX_HD, (h + 1) * FOX_HD)
            q_ref[0, h] = qs[:, sl]
            kb_ref[0, h] = kb[:, sl]
            vb_ref[0, h] = vb[:, sl]
    else:
        q_ref[0] = qs
        kb_ref[0] = kb
        vb_ref[0] = vb
    for h in range(MEM_HEADS):
        sl = slice(_PK_MQ + h * MEM_HD, _PK_MQ + (h + 1) * MEM_HD)
        mq_ref[0, :, h * MEM_HD:(h + 1) * MEM_HD] = _rms(p[:, sl], mqw_ref[...]).astype(bf16)
    lane = lax.broadcasted_iota(jnp.int32, (x.shape[0], LANES), 1)
    logf = jnp.where(lane < FOX_HEADS, _log_sigmoid(p[:, _PK_F:_PK_END] + bf_ref[...]), 0.0)
    logf_ref[0] = logf
    c = _split_dot_left(tri_ref[...], logf) + carry_ref[0:1, :]
    c_ref[0] = c
    carry_ref[0:1, :] = c[x.shape[0] - 1:x.shape[0], :]
    ct_ref[0] = jnp.transpose(c)[0:FOX_HEADS, :]


def _split_dot_left(w, x):
    hi = x.astype(bf16)
    lo = (x - hi.astype(f32)).astype(bf16)
    return (jnp.dot(w, hi, preferred_element_type=f32)
            + jnp.dot(w, lo, preferred_element_type=f32))


def _inproj(x3, norm1_w, w_pack, bf_pad, qnw, knw, mqw, *, head_major, tile):
    nb, ns, _ = x3.shape
    nt = ns // tile
    bd = _block_diag_ones(FOX_W, FOX_HD)
    tri = (lax.broadcasted_iota(jnp.int32, (tile, tile), 0)
           >= lax.broadcasted_iota(jnp.int32, (tile, tile), 1)).astype(bf16)
    const = lambda shape: pl.BlockSpec(shape, lambda b, j: (0,) * len(shape))
    tok = lambda w: pl.BlockSpec((1, tile, w), lambda b, j: (b, j, 0))
    if head_major:
        qkv_shape = jax.ShapeDtypeStruct((nb, FOX_HEADS, ns, FOX_HD), bf16)
        qkv_spec = pl.BlockSpec((1, FOX_HEADS, tile, FOX_HD), lambda b, j: (b, 0, j, 0))
    else:
        qkv_shape = jax.ShapeDtypeStruct((nb, ns, FOX_W), bf16)
        qkv_spec = tok(FOX_W)
    out_shape = (
        qkv_shape, qkv_shape, qkv_shape,
        jax.ShapeDtypeStruct((nb, ns, FOX_W), f32),
        jax.ShapeDtypeStruct((nb, ns, FOX_W), f32),
        jax.ShapeDtypeStruct((nb, ns, LANES), f32),
        jax.ShapeDtypeStruct((nb, ns, LANES), f32),
        jax.ShapeDtypeStruct((nb, FOX_HEADS, ns), f32),
        jax.ShapeDtypeStruct((nb, ns, RW_IN), f32),
        jax.ShapeDtypeStruct((nb, ns, MEM_W), bf16),
    )
    out_specs = (
        qkv_spec, qkv_spec, qkv_spec, tok(FOX_W), tok(FOX_W), tok(LANES), tok(LANES),
        pl.BlockSpec((1, FOX_HEADS, tile), lambda b, j: (b, 0, j)),
        tok(RW_IN), tok(MEM_W),
    )
    return pl.pallas_call(
        functools.partial(_inproj_body, head_major=head_major),
        grid=(nb, nt),
        in_specs=[tok(D_MODEL), const((1, D_MODEL)), const((D_MODEL, _PK_END)), const((1, LANES)),
                  const((1, FOX_W)), const((1, FOX_W)), const((1, MEM_HD)), const((FOX_W, FOX_W)),
                  const((tile, tile))],
        out_specs=out_specs,
        out_shape=out_shape,
        scratch_shapes=[pltpu.VMEM((8, LANES), f32)],
        compiler_params=_cparams(("arbitrary", "arbitrary")),
        name="inproj",
    )(x3, norm1_w, w_pack, bf_pad, qnw, knw, mqw, bd, tri)


def _prep_inproj_weights(w_in, fox_b_f, fox_qn_w, fox_kn_w, mem_qn_w):
    fox_in = 3 * FOX_W + FOX_HEADS
    w_fox = w_in[:, :fox_in]
    w_rw = w_in[:, fox_in:fox_in + RW_IN]
    w_mq = w_in[:, fox_in + RW_IN:fox_in + RW_IN + MEM_W]
    w_gate = w_in[:, fox_in + RW_IN + MEM_W:]
    w_f = jnp.pad(w_fox[:, 3 * FOX_W:], ((0, 0), (0, LANES - FOX_HEADS)))
    w_pack = jnp.concatenate([w_fox[:, :3 * FOX_W], w_rw, w_mq, w_f], axis=1).astype(bf16)
    bf_pad = jnp.pad(fox_b_f, (0, LANES - FOX_HEADS)).reshape(1, LANES)
    qnw = jnp.tile(fox_qn_w, FOX_HEADS).reshape(1, FOX_W)
    knw = jnp.tile(fox_kn_w, FOX_HEADS).reshape(1, FOX_W)
    return w_pack, bf_pad, qnw, knw, mem_qn_w.reshape(1, MEM_HD), w_gate.astype(bf16)


def _fox_prompt_body(q_ref, k_ref, v_ref, c_ref, ct_ref, o_ref, *, blk):
    hp = pl.program_id(1)
    qi = pl.program_id(2)
    lane = lax.broadcasted_iota(jnp.int32, (blk, LANES), 1)
    c_blk = c_ref[0]
    row = lax.broadcasted_iota(jnp.int32, (blk, blk), 0)
    col = lax.broadcasted_iota(jnp.int32, (blk, blk), 1)
    outs = []
    for hh in range(2):
        h = hp * 2 + hh
        q = q_ref[0, hh]
        cq = jnp.sum(jnp.where(lane == h, c_blk, 0.0), axis=1, keepdims=True)

        def step(kj, carry, masked):
            m, l, acc = carry
            start = pl.multiple_of(kj * blk, blk)
            k = k_ref[0, hh, pl.ds(start, blk), :]
            v = v_ref[0, hh, pl.ds(start, blk), :]
            s = lax.dot_general(q, k, (((1,), (1,)), ((), ())), preferred_element_type=f32)
            ck = ct_ref[0, pl.ds(h, 1), pl.ds(start, blk)]
            s = s + (cq - ck)
            if masked:
                s = jnp.where(col <= row, s, NEG_INF)
            m_new = jnp.maximum(m, jnp.max(s, axis=1, keepdims=True))
            alpha = jnp.exp(m - m_new)
            p = jnp.exp(s - m_new)
            l = l * alpha + jnp.sum(p, axis=1, keepdims=True)
            acc = acc * alpha + jnp.dot(p.astype(bf16), v, preferred_element_type=f32)
            return m_new, l, acc

        init = (jnp.full((blk, 1), NEG_INF, f32), jnp.zeros((blk, 1), f32), jnp.zeros((blk, FOX_HD), f32))
        carry = lax.fori_loop(0, qi, functools.partial(step, masked=False), init)
        m, l, acc = step(qi, carry, True)
        outs.append(acc / l)
    o_ref[0] = jnp.concatenate(outs, axis=1).astype(o_ref.dtype)


def _fox_prompt(q, k, v, c, ct, *, blk):
    nb, nh, ns, hd = q.shape
    pair = lambda: pl.BlockSpec((1, 2, ns, hd), lambda b, hp, qi: (b, hp, 0, 0))
    return pl.pallas_call(
        functools.partial(_fox_prompt_body, blk=blk),
        grid=(nb, nh // 2, ns // blk),
        in_specs=[pl.BlockSpec((1, 2, blk, hd), lambda b, hp, qi: (b, hp, qi, 0)), pair(), pair(),
                  pl.BlockSpec((1, blk, LANES), lambda b, hp, qi: (b, qi, 0)),
                  pl.BlockSpec((1, nh, ns), lambda b, hp, qi: (b, 0, 0))],
        out_specs=pl.BlockSpec((1, blk, 2 * hd), lambda b, hp, qi: (b, qi, hp)),
        out_shape=jax.ShapeDtypeStruct((nb, ns, nh * hd), bf16),
        compiler_params=_cparams(("arbitrary", "arbitrary", "arbitrary")),
        name="fox_prompt",
    )(q, k, v, c, ct)


UNIT_COLS = 16
RW_PAIRS = 4


def _rwkv_body(*refs, n_blk_b, tc, n_sb, n_st, prompt_mode):
    if prompt_mode:
        (p_ref, mu_ref, w0_ref, wup_ref, a0_ref, aup_ref, gup_ref, kk_w_ref, ka_ref, rk_ref, lnw_ref, lnb_ref,
         bd_ref, wsa_ref, ob_ref, e_ref, o_ref, s_ref, prev_ref, kk_s, dec_s, beta_s, kt_s, r_s, v_s, g_s,
         v0_s, v1_s, v2_s, vts_s, ots_s, stage_s, o_s) = refs
    else:
        (p_ref, pf_ref, sin_ref, mu_ref, w0_ref, wup_ref, a0_ref, aup_ref, gup_ref, kk_w_ref, ka_ref, rk_ref,
         lnw_ref, lnb_ref, bd_ref, wsa_ref, ob_ref, e_ref, o_ref, s_ref, kk_s, dec_s, beta_s, kt_s, r_s, v_s,
         g_s, v0_s, v1_s, v2_s, vts_s, ots_s, stage_s, o_s) = refs
    n_rows = n_blk_b * tc
    n_units = n_rows // UNIT_COLS
    step = pl.program_id(0)

    if prompt_mode:
        @pl.when(step == 0)
        def _():
            s_ref[...] = jnp.zeros_like(s_ref)
            prev_ref[...] = jnp.zeros_like(prev_ref)
    else:
        s_ref[...] = sin_ref[...]
    stage_s[...] = jnp.zeros_like(stage_s)

    p = p_ref[...].reshape(n_rows, RW_IN)
    rowi = lax.broadcasted_iota(jnp.int32, (n_rows, 1), 0)
    rolled = pltpu.roll(p, 1, axis=0)
    if prompt_mode:
        prev = rolled
        for bb in range(n_blk_b):
            prev = jnp.where(rowi == bb * tc, prev_ref[bb, 0:1, :], prev)
            prev_ref[bb, 0:1, :] = p[(bb + 1) * tc - 1:(bb + 1) * tc, :]
    else:
        prev = jnp.where(rowi % tc == 0, pf_ref[...], rolled)
    xs = p + (prev - p) * mu_ref[...]
    r = xs[:, 0:512]
    k = xs[:, 512:1024]
    v = xs[:, 1024:1536]
    zwa = xs[:, 1536:1664]
    zg = xs[:, 1664:1792]
    log_w = -RW_DECAY * _sigmoid(w0_ref[...] + jnp.dot(jnp.tanh(zwa).astype(bf16), wup_ref[...],
                                                        preferred_element_type=f32))
    a = _sigmoid(a0_ref[...] + jnp.dot(zwa.astype(bf16), aup_ref[...], preferred_element_type=f32))
    g = jnp.dot(_sigmoid(zg).astype(bf16), gup_ref[...], preferred_element_type=f32)
    bd = bd_ref[...]
    kk = k * kk_w_ref[...]
    kk = kk / jnp.maximum(jnp.sqrt(_split_dot(kk * kk, bd)), 1e-12)
    kt = k * (1.0 + (a - 1.0) * ka_ref[...])
    kk_s[...] = kk
    dec_s[...] = jnp.exp(log_w)
    beta_s[...] = kk * a
    kt_s[...] = kt
    r_s[...] = r
    v_s[...] = v
    g_s[...] = g
    v0 = v.astype(bf16).astype(f32)
    v1 = (v - v0).astype(bf16).astype(f32)
    v0_s[...] = v0
    v1_s[...] = v1
    v2_s[...] = v - v0 - v1

    def unit_block(u, bb):
        if prompt_mode:
            return bb * tc + u * n_st, n_st, 0
        return u * UNIT_COLS, UNIT_COLS, bb * n_st

    def vt_body(u, _):
        for bb in range(n_sb):
            blk0, blk_n, roff = unit_block(u, bb)
            rows = pl.ds(pl.multiple_of(blk0, 8), blk_n)
            for part, ref in enumerate((v0_s, v1_s, v2_s)):
                stage_s[part * n_st:(part + 1) * n_st, :] = ref[rows, :][roff:roff + n_st]
            vts_s[u * n_sb + bb] = jnp.transpose(stage_s[...])
        return 0

    lax.fori_loop(0, n_units, vt_body, 0)

    lane = lax.broadcasted_iota(jnp.int32, (FOX_HD, LANES), 1) % FOX_HD
    pairs = [(bb, pr) for bb in range(n_sb) for pr in range(RW_PAIRS)]

    def unit_body(u, _):
        rowvecs, states, v_lhs = [], [], []
        for bb, pr in pairs:
            blk0, blk_n, roff = unit_block(u, bb)
            rows = pl.ds(pl.multiple_of(blk0, 8), blk_n)
            sl = slice(pr * LANES, (pr + 1) * LANES)
            rowvecs.append(tuple(ref[rows, sl][roff:roff + n_st] for ref in (kk_s, dec_s, beta_s, kt_s, r_s)))
            states.append(s_ref[u * n_sb + bb if not prompt_mode else bb, pr])
            vt = vts_s[u * n_sb + bb, sl, :]
            v_lhs.append(jnp.concatenate([vt[0:FOX_HD], vt[FOX_HD:LANES]], axis=1).astype(bf16))
        v_lhs = jnp.concatenate(v_lhs, axis=0)
        ots = [jnp.zeros((FOX_HD, LANES), f32) for _ in range(RW_PAIRS)]
        for j in range(n_st):
            jr = slice(j, j + 1)
            lhs = []
            for (kk_b, _, _, _, _), s in zip(rowvecs, states):
                pk = s * kk_b[jr]
                hi = pk.astype(bf16)
                lhs.append(jnp.concatenate([hi, (pk - hi.astype(f32)).astype(bf16)], axis=1))
            sa_all = jnp.dot(jnp.concatenate(lhs, axis=0), wsa_ref[...], preferred_element_type=f32)
            vb_all = jnp.dot(v_lhs, e_ref[j], preferred_element_type=f32)
            q_lhs = []
            for n, (_, w_b, be_b, kt_b, r_b) in enumerate(rowvecs):
                blk = slice(n * FOX_HD, (n + 1) * FOX_HD)
                states[n] = states[n] * w_b[jr] - sa_all[blk] * be_b[jr] + vb_all[blk] * kt_b[jr]
                q_lhs.append((states[n] * r_b[jr]).astype(bf16))
            res_all = jnp.dot(jnp.concatenate(q_lhs, axis=0), ob_ref[...], preferred_element_type=f32)
            for n, (bb, pr) in enumerate(pairs):
                ots[pr] = jnp.where(lane == bb * n_st + j, res_all[n * FOX_HD:(n + 1) * FOX_HD], ots[pr])
        for n, (bb, pr) in enumerate(pairs):
            s_ref[u * n_sb + bb if not prompt_mode else bb, pr] = states[n]
        for pr in range(RW_PAIRS):
            ots_s[u, pr * FOX_HD:(pr + 1) * FOX_HD, :] = ots[pr]
        return 0

    lax.fori_loop(0, n_units, unit_body, 0)

    def ot_body(u, _):
        t = jnp.transpose(ots_s[u])
        ta, tb = t[0:UNIT_COLS], t[FOX_HD:FOX_HD + UNIT_COLS]
        o16 = jnp.concatenate(
            [x[:, pr * FOX_HD:(pr + 1) * FOX_HD] for pr in range(RW_PAIRS) for x in (ta, tb)], axis=1)
        if prompt_mode:
            for bb in range(n_sb):
                o_s[pl.ds(pl.multiple_of(bb * tc + u * n_st, 8), n_st), :] = o16[bb * n_st:(bb + 1) * n_st]
        else:
            o_s[pl.ds(pl.multiple_of(u * UNIT_COLS, 8), UNIT_COLS), :] = o16
        return 0

    lax.fori_loop(0, n_units, ot_body, 0)

    o = o_s[...]
    inv = 1.0 / FOX_HD
    mean = _split_dot(o, bd) * inv
    d = o - mean
    var = _split_dot(d * d, bd) * inv
    o_gn = d * lax.rsqrt(var + GN_EPS) * lnw_ref[...] + lnb_ref[...]
    bonus = _split_dot(r_s[...] * kt_s[...] * rk_ref[...], bd) * v_s[...]
    out = (o_gn + bonus) * g_s[...]
    o_ref[...] = out.reshape(o_ref.shape).astype(o_ref.dtype)


def _rwkv_weights(rw_mu, rw_w0, rw_w_up, rw_a0, rw_a_up, rw_g_up, rw_k_k, rw_k_a, rw_r_k, rw_ln_w, rw_ln_b):
    row = lambda a: a.reshape(1, -1)
    wup = jnp.concatenate([rw_w_up, jnp.zeros_like(rw_a_up)], axis=0).astype(bf16)
    aup = jnp.concatenate([jnp.zeros_like(rw_w_up), rw_a_up], axis=0).astype(bf16)
    return (row(rw_mu), row(rw_w0), wup, row(rw_a0), aup, rw_g_up.astype(bf16), row(rw_k_k), row(rw_k_a),
            row(rw_r_k), row(rw_ln_w), row(rw_ln_b), _block_diag_ones(RW_W, FOX_HD))


def _rwkv_scratch(n_rows, n_sb):
    n_units = n_rows // UNIT_COLS
    rows = [pltpu.VMEM((n_rows, RW_W), f32) for _ in range(10)]
    return rows + [pltpu.VMEM((n_units * n_sb, RW_W, LANES), f32),
                   pltpu.VMEM((n_units, RW_PAIRS * FOX_HD, LANES), f32),
                   pltpu.VMEM((LANES, RW_W), f32), pltpu.VMEM((n_rows, RW_W), f32)]


def _rwkv_selectors(n_st):
    ob = _block_diag_ones(LANES, FOX_HD)
    wsa = jnp.concatenate([ob, ob], axis=0)
    r = lax.broadcasted_iota(jnp.int32, (n_st, 2 * LANES, LANES), 1)
    c = lax.broadcasted_iota(jnp.int32, (n_st, 2 * LANES, LANES), 2)
    j = lax.broadcasted_iota(jnp.int32, (n_st, 2 * LANES, LANES), 0)
    col = r % LANES
    e = ((col < 3 * n_st) & (col % n_st == j) & (r // LANES == c // FOX_HD)).astype(bf16)
    return wsa, ob, e


def _rwkv_prompt(prw, weights, *, tc):
    nb, ns, _ = prw.shape
    const = lambda a: pl.BlockSpec(a.shape, lambda j: (0,) * a.ndim)
    n_st = UNIT_COLS // nb
    weights = tuple(weights) + _rwkv_selectors(n_st)
    return pl.pallas_call(
        functools.partial(_rwkv_body, n_blk_b=nb, tc=tc, n_sb=nb, n_st=n_st, prompt_mode=True),
        grid=(ns // tc,),
        in_specs=[pl.BlockSpec((nb, tc, RW_IN), lambda j: (0, j, 0))] + [const(w) for w in weights],
        out_specs=(pl.BlockSpec((nb, tc, RW_W), lambda j: (0, j, 0)),
                   pl.BlockSpec((nb, RW_PAIRS, FOX_HD, LANES), lambda j: (0, 0, 0, 0))),
        out_shape=(jax.ShapeDtypeStruct((nb, ns, RW_W), bf16),
                   jax.ShapeDtypeStruct((nb, RW_PAIRS, FOX_HD, LANES), f32)),
        scratch_shapes=[pltpu.VMEM((nb, 8, RW_IN), f32)] + _rwkv_scratch(nb * tc, nb),
        compiler_params=_cparams(("arbitrary",)),
        name="rwkv_prompt",
    )(prw, *weights)


def _rwkv_sample(prw_rows, prev_first_rows, state, weights, *, seq, rows_per_blk):
    n_rows = prw_rows.shape[0]
    nb_blk = rows_per_blk // seq
    const = lambda a: pl.BlockSpec(a.shape, lambda j: (0,) * a.ndim)
    n_sb = UNIT_COLS // seq
    weights = tuple(weights) + _rwkv_selectors(seq)
    return pl.pallas_call(
        functools.partial(_rwkv_body, n_blk_b=nb_blk, tc=seq, n_sb=n_sb, n_st=seq, prompt_mode=False),
        grid=(n_rows // rows_per_blk,),
        in_specs=[pl.BlockSpec((rows_per_blk, RW_IN), lambda j: (j, 0)),
                  pl.BlockSpec((rows_per_blk, RW_IN), lambda j: (j, 0)),
                  pl.BlockSpec((nb_blk, RW_PAIRS, FOX_HD, LANES), lambda j: (j, 0, 0, 0))]
                 + [const(w) for w in weights],
        out_specs=(pl.BlockSpec((rows_per_blk, RW_W), lambda j: (j, 0)),
                   pl.BlockSpec((nb_blk, RW_PAIRS, FOX_HD, LANES), lambda j: (j, 0, 0, 0))),
        out_shape=(jax.ShapeDtypeStruct((n_rows, RW_W), bf16),
                   jax.ShapeDtypeStruct((n_rows // seq, RW_PAIRS, FOX_HD, LANES), f32)),
        scratch_shapes=_rwkv_scratch(rows_per_blk, n_sb),
        compiler_params=_cparams(("arbitrary",)),
        name="rwkv_sample",
    )(prw_rows, prev_first_rows, state, *weights)


def _pack_state(state):
    b = state.shape[0]
    return state.reshape(b, RW_PAIRS, 2, FOX_HD, FOX_HD).transpose(0, 1, 3, 2, 4).reshape(b, RW_PAIRS, FOX_HD, LANES)


def _unpack_state(packed):
    b = packed.shape[0]
    return packed.reshape(b, RW_PAIRS, FOX_HD, 2, FOX_HD).transpose(0, 1, 3, 2, 4).reshape(b, 2 * RW_PAIRS, FOX_HD, FOX_HD)


def _fox_sample_body(pt_ref, *refs, n_pg, n_q):
    del pt_ref
    q_ref, kn_ref, vn_ref, lfn_ref, tri_ref, hm_ref = refs[:6]
    k_refs = refs[6:6 + n_pg]
    v_refs = refs[6 + n_pg:6 + 2 * n_pg]
    lf_refs = refs[6 + 2 * n_pg:6 + 3 * n_pg]
    o_ref, m_s, l_s, acc_s, carry_s = refs[6 + 3 * n_pg:]
    jg = pl.program_id(1)
    n_rows = n_q * FOX_HEADS
    page = tri_ref.shape[0]

    @pl.when(jg == 0)
    def _():
        m_s[...] = jnp.full_like(m_s, NEG_INF)
        l_s[...] = jnp.zeros_like(l_s)
        acc_s[...] = jnp.zeros_like(acc_s)
        carry_s[...] = jnp.zeros_like(carry_s)

    hm = hm_ref[...]
    q4 = q_ref[0]
    qm = jnp.concatenate([jnp.broadcast_to(q4[i:i + 1], (FOX_HEADS, FOX_W)) for i in range(n_q)], axis=0) * hm
    tri = tri_ref[...]

    def cum_bias(lf):
        ct = _split_dot(lf, tri) + carry_s[...]
        carry_s[...] = jnp.broadcast_to(ct[:, page - 1:page], carry_s.shape)
        return jnp.concatenate([ct] * n_q, axis=0)

    def update(s_all, pv):
        m_old = m_s[...]
        m_new = jnp.maximum(m_old, jnp.max(s_all, axis=1, keepdims=True))
        alpha = jnp.exp(m_old - m_new)
        p = jnp.exp(s_all - m_new)
        l_s[...] = l_s[...] * alpha + jnp.sum(p, axis=1, keepdims=True)
        acc_s[...] = acc_s[...] * alpha + pv(p.astype(bf16))
        m_s[...] = m_new

    def pv_pages(p):
        acc = jnp.zeros((n_rows, FOX_W), f32)
        for g in range(n_pg):
            acc = acc + lax.dot_general(p[:, g * page:(g + 1) * page], v_refs[g][0].astype(bf16),
                                        (((1,), (1,)), ((), ())), preferred_element_type=f32)
        return acc

    s_list = []
    for g in range(n_pg):
        s = jnp.dot(qm, k_refs[g][0].astype(bf16), preferred_element_type=f32)
        s_list.append(s - cum_bias(lf_refs[g][0]))
    update(jnp.concatenate(s_list, axis=1), pv_pages)

    @pl.when(jg == pl.num_programs(1) - 1)
    def _():
        pad = jnp.zeros((page - n_q, FOX_W), bf16)
        kn = jnp.concatenate([kn_ref[0], pad], axis=0)
        vn = jnp.concatenate([vn_ref[0], pad], axis=0)
        s = lax.dot_general(qm, kn, (((1,), (1,)), ((), ())), preferred_element_type=f32)
        bias = cum_bias(lfn_ref[0])
        key = lax.broadcasted_iota(jnp.int32, (n_rows, page), 1)
        qpos = lax.broadcasted_iota(jnp.int32, (n_rows, page), 0) // FOX_HEADS
        update(jnp.where(key <= qpos, s - bias, NEG_INF),
               lambda p: jnp.dot(p, vn, preferred_element_type=f32))
        o = acc_s[...] / l_s[...] * hm.astype(f32)
        o_ref[0] = jnp.concatenate(
            [jnp.sum(o[i * FOX_HEADS:(i + 1) * FOX_HEADS], axis=0, keepdims=True) for i in range(n_q)],
            axis=0).astype(o_ref.dtype)


def _fox_sample(page_table, q, kn, vn, lfn_t, cache_kt, cache_vt, cache_lft, *, n_pg):
    nb, n_q, _ = q.shape
    n_pages = page_table.shape[1]
    page = cache_kt.shape[2]
    n_rows = n_q * FOX_HEADS
    tri = (lax.broadcasted_iota(jnp.int32, (page, page), 0)
           <= lax.broadcasted_iota(jnp.int32, (page, page), 1)).astype(bf16)
    hm = (lax.broadcasted_iota(jnp.int32, (n_rows, FOX_W), 0) % FOX_HEADS
          == lax.broadcasted_iota(jnp.int32, (n_rows, FOX_W), 1) // FOX_HD).astype(bf16)
    per_b = lambda shape: pl.BlockSpec((1,) + shape, lambda b, j, pt: (b, 0, 0))
    const = lambda a: pl.BlockSpec(a.shape, lambda b, j, pt: (0, 0))
    paged = lambda shape, g: pl.BlockSpec((1,) + shape, lambda b, j, pt: (pt[b, j * n_pg + g], 0, 0))
    in_specs = ([per_b((n_q, FOX_W))] * 3 + [per_b((FOX_HEADS, page)), const(tri), const(hm)]
                + [paged((FOX_W, page), g) for g in range(n_pg)]
                + [paged((FOX_W, page), g) for g in range(n_pg)]
                + [paged((FOX_HEADS, page), g) for g in range(n_pg)])
    grid_spec = pltpu.PrefetchScalarGridSpec(
        num_scalar_prefetch=1, grid=(nb, n_pages // n_pg), in_specs=in_specs,
        out_specs=pl.BlockSpec((1, n_q, FOX_W), lambda b, j, pt: (b, 0, 0)),
        scratch_shapes=[pltpu.VMEM((n_rows, 1), f32), pltpu.VMEM((n_rows, 1), f32),
                        pltpu.VMEM((n_rows, FOX_W), f32), pltpu.VMEM((FOX_HEADS, page), f32)])
    return pl.pallas_call(
        functools.partial(_fox_sample_body, n_pg=n_pg, n_q=n_q),
        grid_spec=grid_spec,
        out_shape=jax.ShapeDtypeStruct((nb, n_q, FOX_W), bf16),
        compiler_params=_cparams(("arbitrary", "arbitrary")),
        name="fox_sample",
    )(page_table, q, kn, vn, lfn_t, tri, hm, *([cache_kt] * n_pg), *([cache_vt] * n_pg), *([cache_lft] * n_pg))


def _mem_kv_body(m_ref, g_ref, w_ref, knw_ref, k_ref, v_ref):
    n = _rms(m_ref[0], g_ref[...]).astype(bf16)
    kv = jnp.dot(n, w_ref[...], preferred_element_type=f32)
    for h in range(MEM_HEADS):
        sl = slice(h * MEM_HD, (h + 1) * MEM_HD)
        k_ref[0, :, sl] = _rms(kv[:, sl], knw_ref[...])
    v_ref[0] = kv[:, MEM_W:]


def _mem_kv(mem, mem_norm_w, w_mem_kv, mem_kn_w):
    nb, nm, _ = mem.shape
    const = lambda shape: pl.BlockSpec(shape, lambda b: (0,) * len(shape))
    out = jax.ShapeDtypeStruct((nb, nm, MEM_W), f32)
    return pl.pallas_call(
        _mem_kv_body,
        grid=(nb,),
        in_specs=[pl.BlockSpec((1, nm, D_MODEL), lambda b: (b, 0, 0)), const((1, D_MODEL)),
                  const((D_MODEL, 2 * MEM_W)), const((1, MEM_HD))],
        out_specs=(pl.BlockSpec((1, nm, MEM_W), lambda b: (b, 0, 0)),) * 2,
        out_shape=(out, out),
        compiler_params=_cparams(("arbitrary",)),
        name="mem_kv",
    )(mem, mem_norm_w.reshape(1, D_MODEL), w_mem_kv.astype(bf16), mem_kn_w.reshape(1, MEM_HD))


def _mem_attend_body(q_ref, k_ref, v_ref, o_ref):
    q = q_ref[0]
    for h in range(MEM_HEADS):
        sl = slice(h * MEM_HD, (h + 1) * MEM_HD)
        k = k_ref[0, :, sl].astype(bf16)
        v = v_ref[0, :, sl].astype(bf16)
        s = lax.dot_general(q[:, sl], k, (((1,), (1,)), ((), ())), preferred_element_type=f32) * MEM_SCALE
        e = jnp.exp(s - jnp.max(s, axis=1, keepdims=True))
        p = e / jnp.sum(e, axis=1, keepdims=True)
        o_ref[0, :, sl] = jnp.dot(p.astype(bf16), v, preferred_element_type=f32).astype(o_ref.dtype)


def _mem_attend(q, mk, mv, *, tq):
    nb, ns, _ = q.shape
    nm = mk.shape[1]
    kv = pl.BlockSpec((1, nm, MEM_W), lambda b, j: (b, 0, 0))
    return pl.pallas_call(
        _mem_attend_body,
        grid=(nb, ns // tq),
        in_specs=[pl.BlockSpec((1, tq, MEM_W), lambda b, j: (b, j, 0)), kv, kv],
        out_specs=pl.BlockSpec((1, tq, MEM_W), lambda b, j: (b, j, 0)),
        out_shape=jax.ShapeDtypeStruct((nb, ns, MEM_W), bf16),
        compiler_params=_cparams(("arbitrary", "arbitrary")),
        name="mem_attend",
    )(q, mk, mv)


def _finish_body(x_ref, of_ref, or_ref, om_ref, g1_ref, wg_ref, wb_ref, wo_ref, g2_ref, h_ref, xn2_ref):
    x = x_ref[...]
    xn = _rms(x, g1_ref[...]).astype(bf16)
    merged = jnp.zeros_like(x)
    for n, br in enumerate((of_ref, or_ref, om_ref)):
        gate = _sigmoid(jnp.dot(xn, wg_ref[:, n * D_MODEL:(n + 1) * D_MODEL], preferred_element_type=f32))
        merged = merged + gate * jnp.dot(br[...], wb_ref[n], preferred_element_type=f32)
    h = x + jnp.dot(merged.astype(bf16), wo_ref[...], preferred_element_type=f32)
    h_ref[...] = h
    xn2_ref[...] = _rms(h, g2_ref[...]).astype(bf16)


def _finish(x, o_fox, o_rw, o_mem, norm1_w, w_gate, w_branch, w_out, norm2_w, *, tile):
    n = x.shape[0]
    const = lambda shape: pl.BlockSpec(shape, lambda j: (0,) * len(shape))
    tok = lambda w: pl.BlockSpec((tile, w), lambda j: (j, 0))
    return pl.pallas_call(
        _finish_body,
        grid=(n // tile,),
        in_specs=[tok(D_MODEL), tok(FOX_W), tok(RW_W), tok(MEM_W), const((1, D_MODEL)),
                  const((D_MODEL, 3 * D_MODEL)), const((3, FOX_W, D_MODEL)), const((D_MODEL, D_MODEL)),
                  const((1, D_MODEL))],
        out_specs=(tok(D_MODEL), tok(D_MODEL)),
        out_shape=(jax.ShapeDtypeStruct((n, D_MODEL), f32), jax.ShapeDtypeStruct((n, D_MODEL), bf16)),
        compiler_params=_cparams(("arbitrary",)),
        name="finish",
    )(x, o_fox, o_rw, o_mem, norm1_w.reshape(1, D_MODEL), w_gate, w_branch.astype(bf16), w_out.astype(bf16),
      norm2_w.reshape(1, D_MODEL))


_PEER_SLOTS = [(ra, rb) for ra in range(PEER_TOPK) for rb in range(PEER_TOPK) if (ra + 1) * (rb + 1) <= PEER_TOPK]
_PEER_SLOT_ROWS = -(-len(_PEER_SLOTS) // 8) * 8


def _top_rows(s, n_take):
    iota = lax.broadcasted_iota(jnp.int32, s.shape, 0).astype(f32)
    big = float(s.shape[0])
    vals, idxs = [], []
    for _ in range(n_take):
        m = jnp.max(s, axis=0, keepdims=True)
        idx = jnp.min(jnp.where(s == m, iota, big), axis=0, keepdims=True)
        vals.append(m)
        idxs.append(idx)
        s = jnp.where(iota == idx, -jnp.inf, s)
    return vals, idxs


def _route_body(x_ref, wq_ref, k1_ref, k2_ref, a_ref, b_ref, g_ref,
                q_s, cand_s, ea_s, eb_s, a_s, b_s, g_s, at_s, bt_s, gt_s):
    q_s[...] = lax.dot_general(wq_ref[...], x_ref[...], (((1,), (1,)), ((), ())),
                               preferred_element_type=f32).astype(bf16)
    cand_s[...] = jnp.full_like(cand_s, -jnp.inf)
    ea_s[...] = jnp.zeros_like(ea_s)
    eb_s[...] = jnp.zeros_like(eb_s)

    def head(h, _):
        base = pl.multiple_of(h * 2 * PEER_HALF, 2 * PEER_HALF)
        s1 = jnp.dot(k1_ref[h], q_s[pl.ds(base, PEER_HALF), :], preferred_element_type=f32)
        s2 = jnp.dot(k2_ref[h], q_s[pl.ds(base + PEER_HALF, PEER_HALF), :], preferred_element_type=f32)
        v1, i1 = _top_rows(s1, PEER_TOPK)
        v2, i2 = _top_rows(s2, PEER_TOPK)
        for slot, (ra, rb) in enumerate(_PEER_SLOTS):
            cand_s[slot:slot + 1, :] = v1[ra] + v2[rb]
            ea_s[slot:slot + 1, :] = i1[ra]
            eb_s[slot:slot + 1, :] = i2[rb]
        cand = cand_s[...]
        ea = ea_s[...]
        eb = eb_s[...]
        iota = lax.broadcasted_iota(jnp.int32, cand.shape, 0).astype(f32)
        scs = []
        for r in range(PEER_TOPK):
            m = jnp.max(cand, axis=0, keepdims=True)
            slot = jnp.min(jnp.where(cand == m, iota, float(_PEER_SLOT_ROWS)), axis=0, keepdims=True)
            hit = iota == slot
            a_s[r:r + 1, :] = jnp.sum(jnp.where(hit, ea, 0.0), axis=0, keepdims=True)
            b_s[r:r + 1, :] = jnp.sum(jnp.where(hit, eb, 0.0), axis=0, keepdims=True)
            cand = jnp.where(hit, -jnp.inf, cand)
            scs.append(m)
        es = [jnp.exp(sc - scs[0]) for sc in scs]
        z = es[0]
        for e in es[1:]:
            z = z + e
        for r in range(PEER_TOPK):
            g_s[r:r + 1, :] = es[r] / z
        rows = pl.ds(pl.multiple_of(h * PEER_TOPK, PEER_TOPK), PEER_TOPK)
        at_s[rows, :] = a_s[...]
        bt_s[rows, :] = b_s[...]
        gt_s[rows, :] = g_s[...]
        return 0

    lax.fori_loop(0, PEER_HEADS, head, 0)
    a_ref[...] = jnp.transpose(at_s[...])
    b_ref[...] = jnp.transpose(bt_s[...])
    g_ref[...] = jnp.transpose(gt_s[...])


def _route(xn2, wq_t, keys1, keys2, *, tile):
    n = xn2.shape[0]
    n_slots = PEER_HEADS * PEER_TOPK
    const = lambda shape: pl.BlockSpec(shape, lambda j: (0,) * len(shape))
    out = jax.ShapeDtypeStruct((n, n_slots), f32)
    ospec = pl.BlockSpec((tile, n_slots), lambda j: (j, 0))
    cand = lambda: pltpu.VMEM((_PEER_SLOT_ROWS, tile), f32)
    top = lambda: pltpu.VMEM((PEER_TOPK, tile), f32)
    full = lambda: pltpu.VMEM((n_slots, tile), f32)
    return pl.pallas_call(
        _route_body,
        grid=(n // tile,),
        in_specs=[pl.BlockSpec((tile, D_MODEL), lambda j: (j, 0)), const(wq_t.shape), const(keys1.shape),
                  const(keys2.shape)],
        out_specs=(ospec, ospec, ospec),
        out_shape=(out, out, out),
        scratch_shapes=[pltpu.VMEM((wq_t.shape[0], tile), bf16), cand(), cand(), cand(), top(), top(), top(),
                        full(), full(), full()],
        compiler_params=_cparams(("arbitrary",)),
        name="peer_route",
    )(xn2, wq_t, keys1, keys2)


G_PITCH = PEER_NK + 8
EXPERT_SUB = 512


def _gelu_tanh(x):
    return 0.5 * x * (1.0 + jnp.tanh(math.sqrt(2.0 / math.pi) * (x + 0.044715 * (x * x * x))))


def _peer_body(x_ref, a_ref, b_ref, g_ref, h_ref, u_ref, v_ref, y_ref, gs_ref, acc_ref):
    tile = x_ref.shape[0]
    c = pl.program_id(1)
    n_exp = u_ref.shape[0]

    @pl.when(c == 0)
    def _():
        acc_ref[...] = jnp.zeros_like(acc_ref)
        sub = lax.broadcasted_iota(jnp.int32, (PEER_NK, LANES), 0).astype(f32)

        def tok8(t8, _):
            rows = pl.ds(pl.multiple_of(t8 * 8, 8), 8)
            a8, b8, g8 = a_ref[rows, :], b_ref[rows, :], g_ref[rows, :]
            for j in range(8):
                hit_a = sub == a8[j:j + 1]
                gate = jnp.where(hit_a, g8[j:j + 1], 0.0)
                g_hi = gate.astype(bf16)
                g_lo = (gate - g_hi.astype(f32)).astype(bf16)
                bt = jnp.where(sub == b8[j:j + 1], 1.0, 0.0).astype(bf16)
                gt = lax.dot_general(jnp.concatenate([g_hi, g_lo], axis=1), jnp.concatenate([bt, bt], axis=1),
                                     (((1,), (1,)), ((), ())), preferred_element_type=f32)
                gs_ref[pl.ds(pl.multiple_of((t8 * 8 + j) * G_PITCH, 8), PEER_NK), :] = gt
            return 0

        lax.fori_loop(0, tile // 8, tok8, 0)

    x = x_ref[...]
    acc = acc_ref[...]
    for s in range(n_exp // EXPERT_SUB):
        e0 = s * EXPERT_SUB
        act = lax.dot_general(x, u_ref[e0:e0 + EXPERT_SUB, :], (((1,), (1,)), ((), ())),
                              preferred_element_type=f32)
        i1 = (c * n_exp + e0) // PEER_NK
        gates = jnp.concatenate(
            [gs_ref[pl.ds(i1 + k, tile, stride=G_PITCH), :] for k in range(EXPERT_SUB // PEER_NK)], axis=1)
        w = (gates * _gelu_tanh(act)).astype(bf16)
        acc = acc + jnp.dot(w, v_ref[e0:e0 + EXPERT_SUB, :], preferred_element_type=f32)
    acc_ref[...] = acc

    @pl.when(c == pl.num_programs(1) - 1)
    def _():
        y_ref[...] = h_ref[...] + acc


def _peer(xn2, a, b, g, h, u_bf, v_bf, *, tile, chunk):
    n = xn2.shape[0]
    n_slots = a.shape[1]
    n_experts = u_bf.shape[0]
    tok = lambda w: pl.BlockSpec((tile, w), lambda j, c: (j, 0))
    tab = pl.BlockSpec((chunk, D_MODEL), lambda j, c: (c, 0))
    return pl.pallas_call(
        _peer_body,
        grid=(n // tile, n_experts // chunk),
        in_specs=[tok(D_MODEL), tok(n_slots), tok(n_slots), tok(n_slots), tok(D_MODEL), tab, tab],
        out_specs=tok(D_MODEL),
        out_shape=jax.ShapeDtypeStruct((n, D_MODEL), f32),
        scratch_shapes=[pltpu.VMEM((tile * G_PITCH, PEER_NK), f32), pltpu.VMEM((tile, D_MODEL), f32)],
        compiler_params=_cparams(("arbitrary", "arbitrary")),
        name="peer_experts",
    )(xn2, a, b, g, h, u_bf, v_bf)


TOK_TILE = 256
ATT_BLK = 512
RW_CHUNK = 128
RW_SAMPLE_ROWS = 128
PAGES_PER_STEP = 16
EXPERT_CHUNK = 2048


def _merge_and_peer(x2, o_fox, o_rw, o_mem, shared):
    h, xn2 = _finish(x2, o_fox, o_rw, o_mem, shared["norm1_w"], shared["w_gate"], shared["w_branch"],
                     shared["w_out"], shared["norm2_w"], tile=TOK_TILE)
    a, b, g = _route(xn2, shared["wq_t"], shared["keys1"], shared["keys2"], tile=TOK_TILE)
    return _peer(xn2, a, b, g, h, shared["u_bf"], shared["v_bf"], tile=TOK_TILE, chunk=EXPERT_CHUNK)


def kernel(x_prompt, x_sample, cache_fox_k, cache_fox_v, cache_fox_logf, cache_mem_k, cache_mem_v, state_rwkv, state_rwkv_shift, page_table, mem_prompt, norm1_w, w_in, fox_b_f, fox_qn_w, fox_kn_w, rw_mu, rw_w0, rw_w_up, rw_a0, rw_a_up, rw_g_up, rw_k_k, rw_k_a, rw_r_k, rw_ln_w, rw_ln_b, mem_norm_w, w_mem_kv, mem_qn_w, mem_kn_w, w_branch, w_out, norm2_w, peer_w_q, peer_keys1, peer_keys2, peer_u, peer_v):
    bp, sp, _ = x_prompt.shape
    db, ds, _ = x_sample.shape
    w_pack, bf_pad, qnw, knw, mqw, w_gate = _prep_inproj_weights(w_in, fox_b_f, fox_qn_w, fox_kn_w, mem_qn_w)
    g1 = norm1_w.reshape(1, D_MODEL)
    rw_w = _rwkv_weights(rw_mu, rw_w0, rw_w_up, rw_a0, rw_a_up, rw_g_up, rw_k_k, rw_k_a, rw_r_k, rw_ln_w, rw_ln_b)
    shared = dict(norm1_w=norm1_w, w_gate=w_gate, w_branch=w_branch, w_out=w_out, norm2_w=norm2_w,
                  wq_t=peer_w_q.T.astype(bf16), keys1=peer_keys1.astype(bf16), keys2=peer_keys2.astype(bf16),
                  u_bf=peer_u.astype(bf16), v_bf=peer_v.astype(bf16))

    qh, kh, vh, k_p, v_p, logf_p, c_p, ct_p, prw_p, mq_p = _inproj(
        x_prompt, g1, w_pack, bf_pad, qnw, knw, mqw, head_major=True, tile=TOK_TILE)
    o_fox = _fox_prompt(qh, kh, vh, c_p, ct_p, blk=ATT_BLK)
    o_rw, st_p = _rwkv_prompt(prw_p, rw_w, tc=RW_CHUNK)
    mem_k_p, mem_v_p = _mem_kv(mem_prompt, mem_norm_w, w_mem_kv, mem_kn_w)
    o_mem = _mem_attend(mq_p, mem_k_p, mem_v_p, tq=ATT_BLK)
    n_p = bp * sp
    y_prompt = _merge_and_peer(x_prompt.reshape(n_p, D_MODEL), o_fox.reshape(n_p, FOX_W), o_rw.reshape(n_p, RW_W),
                               o_mem.reshape(n_p, MEM_W), shared).reshape(bp, sp, D_MODEL)

    n_s = db * ds
    qt, kt, vt, k_s, v_s, logf_s, _, _, prw_s, mq_s = _inproj(
        x_sample.reshape(1, n_s, D_MODEL), g1, w_pack, bf_pad, qnw, knw, mqw, head_major=False, tile=TOK_TILE)
    pool, page = cache_fox_k.shape[:2]
    logf_s = logf_s.reshape(db, ds, LANES)[:, :, :FOX_HEADS]
    lfn_t = jnp.pad(jnp.swapaxes(logf_s, 1, 2), ((0, 0), (0, 0), (0, page - ds)))
    pos_minor = lambda a: jnp.transpose(a, (0, 2, 3, 1)).reshape(pool, FOX_W, page)
    o_fox_s = _fox_sample(page_table, qt.reshape(db, ds, FOX_W), kt.reshape(db, ds, FOX_W), vt.reshape(db, ds, FOX_W),
                          lfn_t, pos_minor(cache_fox_k), pos_minor(cache_fox_v), jnp.swapaxes(cache_fox_logf, 1, 2),
                          n_pg=PAGES_PER_STEP)
    prw_rows = prw_s.reshape(n_s, RW_IN)
    o_rw_s, st_s = _rwkv_sample(prw_rows, jnp.repeat(state_rwkv_shift, ds, axis=0), _pack_state(state_rwkv), rw_w,
                                seq=ds, rows_per_blk=RW_SAMPLE_ROWS)
    nm = cache_mem_k.shape[1]
    o_mem_s = _mem_attend(mq_s.reshape(db, ds, MEM_W), cache_mem_k.reshape(db, nm, MEM_W),
                          cache_mem_v.reshape(db, nm, MEM_W), tq=ds)
    y_sample = _merge_and_peer(x_sample.reshape(n_s, D_MODEL), o_fox_s.reshape(n_s, FOX_W), o_rw_s,
                               o_mem_s.reshape(n_s, MEM_W), shared).reshape(db, ds, D_MODEL)

    heads = lambda a, b, s: a.reshape(b, s, FOX_HEADS, FOX_HD)
    return (y_prompt, y_sample,
            heads(k_p, bp, sp), heads(v_p, bp, sp), logf_p[:, :, :FOX_HEADS],
            _unpack_state(st_p), prw_p[:, -1],
            mem_k_p.reshape(bp, nm, MEM_HEADS, MEM_HD), mem_v_p.reshape(bp, nm, MEM_HEADS, MEM_HD),
            heads(k_s, db, ds), heads(v_s, db, ds), logf_s,
            _unpack_state(st_s), prw_rows.reshape(db, ds, RW_IN)[:, -1])
```

```python
import functools
import math

import jax
import jax.numpy as jnp
from jax import lax
from jax.experimental import pallas as pl
from jax.experimental.pallas import tpu as pltpu

f32 = jnp.float32
bf16 = jnp.bfloat16

D_MODEL = 1024
RMS_EPS = 1e-6
NEG_INF = -1e30
FOX_HEADS = 8
FOX_HD = 64
FOX_W = 512
FOX_SCALE = FOX_HD ** -0.5
LOG2E = math.log2(math.e)
RW_W = 512
RW_IN = 1792
RW_DECAY = math.exp(-0.5)
GN_EPS = 64e-5
MEM_HEADS = 4
MEM_HD = 128
MEM_W = 512
MEM_SCALE = MEM_HD ** -0.5
PEER_HEADS = 8
PEER_NK = 128
PEER_HALF = 128
PEER_TOPK = 16

LANES = 128
VMEM_LIMIT = 56 * 1024 * 1024

_PK_Q, _PK_K, _PK_V, _PK_RW, _PK_MQ, _PK_F, _PK_END = 0, 512, 1024, 1536, 3328, 3840, 3968


def _cparams(sem):
    return pltpu.CompilerParams(dimension_semantics=sem, vmem_limit_bytes=VMEM_LIMIT)


def _block_diag_ones(width, group):
    r = lax.broadcasted_iota(jnp.int32, (width, width), 0) // group
    c = lax.broadcasted_iota(jnp.int32, (width, width), 1) // group
    return (r == c).astype(bf16)


def _split_dot(x, w):
    hi = x.astype(bf16)
    lo = (x - hi.astype(f32)).astype(bf16)
    return (jnp.dot(hi, w, preferred_element_type=f32)
            + jnp.dot(lo, w, preferred_element_type=f32))


def _rms(x, g):
    return x * lax.rsqrt(jnp.mean(x * x, axis=-1, keepdims=True) + RMS_EPS) * g


def _log_sigmoid(x):
    return jnp.minimum(x, 0.0) - jnp.log1p(jnp.exp(-jnp.abs(x)))


def _sigmoid(x):
    return 1.0 / (1.0 + jnp.exp(-x))


def _inproj_body(x_ref, g_ref, w_ref, bf_ref, qnw_ref, knw_ref, mqw_ref, bd_ref, tri_ref, sc_ref, ones_ref,
                 q_ref, kb_ref, vb_ref, kp_ref, vp_ref, logf_ref, prw_ref, mq_ref,
                 carry_ref, *, head_major):
    @pl.when(pl.program_id(1) == 0)
    def _():
        carry_ref[...] = jnp.zeros_like(carry_ref)

    x = x_ref[0]
    xn = _rms(x, g_ref[...]).astype(bf16)
    p = jnp.dot(xn, w_ref[...], preferred_element_type=f32)
    q = p[:, _PK_Q:_PK_K]
    k = p[:, _PK_K:_PK_V]
    v = p[:, _PK_V:_PK_RW]
    prw_ref[0] = p[:, _PK_RW:_PK_MQ]
    bd = bd_ref[...]
    qn = q * lax.rsqrt(_split_dot(q * q, bd) * (1.0 / FOX_HD) + RMS_EPS) * qnw_ref[...]
    kn = k * lax.rsqrt(_split_dot(k * k, bd) * (1.0 / FOX_HD) + RMS_EPS) * knw_ref[...]
    kp_ref[0] = kn
    vp_ref[0] = v
    kb = kn.astype(bf16)
    vb = v.astype(bf16)
    lane = lax.broadcasted_iota(jnp.int32, (x.shape[0], LANES), 1)
    logf = jnp.where(lane < FOX_HEADS, _log_sigmoid(p[:, _PK_F:_PK_END] + bf_ref[...]), 0.0)
    logf_ref[0] = logf
    for h in range(MEM_HEADS):
        sl = slice(_PK_MQ + h * MEM_HD, _PK_MQ + (h + 1) * MEM_HD)
        mq_ref[0, :, h * MEM_HD:(h + 1) * MEM_HD] = _rms(p[:, sl], mqw_ref[...]).astype(bf16)
    if not head_major:
        q_ref[0] = (qn * FOX_SCALE).astype(bf16)
        kb_ref[0] = kb
        vb_ref[0] = vb
        return
    c = _split_dot_left(tri_ref[...], logf) + carry_ref[0:1, :]
    carry_ref[0:1, :] = c[x.shape[0] - 1:x.shape[0], :]
    c2 = c * LOG2E
    c_hi = c2.astype(bf16)
    rest = c2 - c_hi.astype(f32)
    c_mid = rest.astype(bf16)
    parts = (c_hi, c_mid, (rest - c_mid.astype(f32)).astype(bf16))
    extra_q = ones_ref[0:1, :]
    extra_k = ones_ref[1:2, :]
    for i, part in enumerate(parts):
        extra_q = extra_q + jnp.dot(part, sc_ref[i], preferred_element_type=f32)
        extra_k = extra_k + jnp.dot(part, sc_ref[len(parts) + i], preferred_element_type=f32)
    qs = (qn * (FOX_SCALE * LOG2E)).astype(bf16)
    for h in range(FOX_HEADS):
        pair = slice((h // 2) * LANES, (h // 2 + 1) * LANES)
        own = slice(h * LANES, (h + 1) * LANES)
        in_own_half = lane // FOX_HD == h % 2
        q_ref[0, h] = jnp.where(in_own_half, qs[:, pair], extra_q[:, own].astype(bf16))
        kb_ref[0, h] = jnp.where(in_own_half, kb[:, pair], extra_k[:, own].astype(bf16))
        vb_ref[0, h] = vb[:, h * FOX_HD:(h + 1) * FOX_HD]


def _split_dot_left(w, x):
    hi = x.astype(bf16)
    lo = (x - hi.astype(f32)).astype(bf16)
    return (jnp.dot(w, hi, preferred_element_type=f32)
            + jnp.dot(w, lo, preferred_element_type=f32))


def _inproj(x3, norm1_w, w_pack, bf_pad, qnw, knw, mqw, *, head_major, tile):
    nb, ns, _ = x3.shape
    nt = ns // tile
    bd = _block_diag_ones(FOX_W, FOX_HD)
    tri = (lax.broadcasted_iota(jnp.int32, (tile, tile), 0)
           >= lax.broadcasted_iota(jnp.int32, (tile, tile), 1)).astype(bf16)
    const = lambda shape: pl.BlockSpec(shape, lambda b, j: (0,) * len(shape))
    tok = lambda w: pl.BlockSpec((1, tile, w), lambda b, j: (b, j, 0))
    if head_major:
        hm = lambda w: (jax.ShapeDtypeStruct((nb, FOX_HEADS, ns, w), bf16),
                        pl.BlockSpec((1, FOX_HEADS, tile, w), lambda b, j: (b, 0, j, 0)))
        qkv = (hm(LANES), hm(LANES), hm(FOX_HD))
    else:
        qkv = ((jax.ShapeDtypeStruct((nb, ns, FOX_W), bf16), tok(FOX_W)),) * 3
    n_parts = 3
    lanes = lax.broadcasted_iota(jnp.int32, (2 * n_parts, LANES, FOX_HEADS * LANES), 2)
    rows = lax.broadcasted_iota(jnp.int32, (2 * n_parts, LANES, FOX_HEADS * LANES), 1)
    slot = lax.broadcasted_iota(jnp.int32, (2 * n_parts, LANES, FOX_HEADS * LANES), 0)
    head = lanes // LANES
    off = lanes % LANES - (1 - head % 2) * FOX_HD
    sc = jnp.where((rows == head) & (off == slot), jnp.where(slot < n_parts, 1.0, -1.0), 0.0).astype(bf16)
    off1 = off[0, 0:1, :]
    ones = jnp.concatenate([(off1 >= n_parts) & (off1 < 2 * n_parts), (off1 >= 0) & (off1 < n_parts)],
                           axis=0).astype(f32)
    out_shape = tuple(s for s, _ in qkv) + (
        jax.ShapeDtypeStruct((nb, ns, FOX_W), f32),
        jax.ShapeDtypeStruct((nb, ns, FOX_W), f32),
        jax.ShapeDtypeStruct((nb, ns, LANES), f32),
        jax.ShapeDtypeStruct((nb, ns, RW_IN), f32),
        jax.ShapeDtypeStruct((nb, ns, MEM_W), bf16),
    )
    out_specs = tuple(s for _, s in qkv) + (tok(FOX_W), tok(FOX_W), tok(LANES), tok(RW_IN), tok(MEM_W))
    return pl.pallas_call(
        functools.partial(_inproj_body, head_major=head_major),
        grid=(nb, nt),
        in_specs=[tok(D_MODEL), const((1, D_MODEL)), const((D_MODEL, _PK_END)), const((1, LANES)),
                  const((1, FOX_W)), const((1, FOX_W)), const((1, MEM_HD)), const((FOX_W, FOX_W)),
                  const((tile, tile)), const(sc.shape), const(ones.shape)],
        out_specs=out_specs,
        out_shape=out_shape,
        scratch_shapes=[pltpu.VMEM((8, LANES), f32)],
        compiler_params=_cparams(("arbitrary", "arbitrary")),
        name="inproj",
    )(x3, norm1_w, w_pack, bf_pad, qnw, knw, mqw, bd, tri, sc, ones)


def _prep_inproj_weights(w_in, fox_b_f, fox_qn_w, fox_kn_w, mem_qn_w):
    fox_in = 3 * FOX_W + FOX_HEADS
    w_fox = w_in[:, :fox_in]
    w_rw = w_in[:, fox_in:fox_in + RW_IN]
    w_mq = w_in[:, fox_in + RW_IN:fox_in + RW_IN + MEM_W]
    w_gate = w_in[:, fox_in + RW_IN + MEM_W:]
    w_f = jnp.pad(w_fox[:, 3 * FOX_W:], ((0, 0), (0, LANES - FOX_HEADS)))
    w_pack = jnp.concatenate([w_fox[:, :3 * FOX_W], w_rw, w_mq, w_f], axis=1).astype(bf16)
    bf_pad = jnp.pad(fox_b_f, (0, LANES - FOX_HEADS)).reshape(1, LANES)
    qnw = jnp.tile(fox_qn_w, FOX_HEADS).reshape(1, FOX_W)
    knw = jnp.tile(fox_kn_w, FOX_HEADS).reshape(1, FOX_W)
    return w_pack, bf_pad, qnw, knw, mem_qn_w.reshape(1, MEM_HD), w_gate.astype(bf16)


def _fox_prompt_body(q_ref, k_ref, v_ref, o_ref, *, blk):
    qi = pl.program_id(2)
    row = lax.broadcasted_iota(jnp.int32, (blk, blk), 0)
    col = lax.broadcasted_iota(jnp.int32, (blk, blk), 1)
    qs = [q_ref[0, hh] for hh in range(2)]

    def step(kj, carry, masked):
        start = pl.multiple_of(kj * blk, blk)
        new = []
        for hh, (m, l, acc) in enumerate(carry):
            k = k_ref[0, hh, pl.ds(start, blk), :]
            v = v_ref[0, hh, pl.ds(start, blk), :]
            s = lax.dot_general(qs[hh], k, (((1,), (1,)), ((), ())), preferred_element_type=f32)
            if masked:
                s = jnp.where(col <= row, s, NEG_INF)
            m_new = jnp.maximum(m, jnp.max(s, axis=1, keepdims=True))
            alpha = jnp.exp2(m - m_new)
            p = jnp.exp2(s - m_new)
            l = l * alpha + jnp.sum(p, axis=1, keepdims=True)
            acc = acc * alpha + jnp.dot(p.astype(bf16), v, preferred_element_type=f32)
            new.append((m_new, l, acc))
        return tuple(new)

    init = (jnp.full((blk, 1), NEG_INF, f32), jnp.zeros((blk, 1), f32), jnp.zeros((blk, FOX_HD), f32))
    carry = lax.fori_loop(0, qi, functools.partial(step, masked=False), (init, init))
    final = step(qi, carry, True)
    o_ref[0] = jnp.concatenate([acc / l for _, l, acc in final], axis=1).astype(o_ref.dtype)


def _fox_prompt(q, k, v, *, blk):
    nb, nh, ns, hd = v.shape
    pair = lambda w: pl.BlockSpec((1, 2, ns, w), lambda b, hp, qi: (b, hp, 0, 0))
    return pl.pallas_call(
        functools.partial(_fox_prompt_body, blk=blk),
        grid=(nb, nh // 2, ns // blk),
        in_specs=[pl.BlockSpec((1, 2, blk, q.shape[3]), lambda b, hp, qi: (b, hp, qi, 0)),
                  pair(k.shape[3]), pair(hd)],
        out_specs=pl.BlockSpec((1, blk, 2 * hd), lambda b, hp, qi: (b, qi, hp)),
        out_shape=jax.ShapeDtypeStruct((nb, ns, nh * hd), bf16),
        compiler_params=_cparams(("arbitrary", "arbitrary", "arbitrary")),
        name="fox_prompt",
    )(q, k, v)


UNIT_COLS = 16
RW_PAIRS = 4


def _rwkv_body(*refs, n_blk_b, tc, n_sb, n_st, prompt_mode):
    if prompt_mode:
        (p_ref, mu_ref, w0_ref, wup_ref, a0_ref, aup_ref, gup_ref, kk_w_ref, ka_ref, rk_ref, lnw_ref, lnb_ref,
         bd_ref, wsa_ref, ob_ref, e_ref, o_ref, s_ref, prev_ref, kk_s, dec_s, beta_s, kt_s, r_s, v_s, g_s,
         v0_s, v1_s, v2_s, vts_s, ots_s, stage_s, o_s) = refs
    else:
        (p_ref, pf_ref, sin_ref, mu_ref, w0_ref, wup_ref, a0_ref, aup_ref, gup_ref, kk_w_ref, ka_ref, rk_ref,
         lnw_ref, lnb_ref, bd_ref, wsa_ref, ob_ref, e_ref, o_ref, s_ref, kk_s, dec_s, beta_s, kt_s, r_s, v_s,
         g_s, v0_s, v1_s, v2_s, vts_s, ots_s, stage_s, o_s) = refs
    n_rows = n_blk_b * tc
    n_units = n_rows // UNIT_COLS
    step = pl.program_id(0)

    if prompt_mode:
        @pl.when(step == 0)
        def _():
            s_ref[...] = jnp.zeros_like(s_ref)
            prev_ref[...] = jnp.zeros_like(prev_ref)
    else:
        s_ref[...] = sin_ref[...]
    stage_s[...] = jnp.zeros_like(stage_s)

    p = p_ref[...].reshape(n_rows, RW_IN)
    rowi = lax.broadcasted_iota(jnp.int32, (n_rows, 1), 0)
    rolled = pltpu.roll(p, 1, axis=0)
    if prompt_mode:
        prev = rolled
        for bb in range(n_blk_b):
            prev = jnp.where(rowi == bb * tc, prev_ref[bb, 0:1, :], prev)
            prev_ref[bb, 0:1, :] = p[(bb + 1) * tc - 1:(bb + 1) * tc, :]
    else:
        prev = jnp.where(rowi % tc == 0, pf_ref[...], rolled)
    xs = p + (prev - p) * mu_ref[...]
    r = xs[:, 0:512]
    k = xs[:, 512:1024]
    v = xs[:, 1024:1536]
    zwa = xs[:, 1536:1664]
    zg = xs[:, 1664:1792]
    log_w = -RW_DECAY * _sigmoid(w0_ref[...] + jnp.dot(jnp.tanh(zwa).astype(bf16), wup_ref[...],
                                                        preferred_element_type=f32))
    a = _sigmoid(a0_ref[...] + jnp.dot(zwa.astype(bf16), aup_ref[...], preferred_element_type=f32))
    g = jnp.dot(_sigmoid(zg).astype(bf16), gup_ref[...], preferred_element_type=f32)
    bd = bd_ref[...]
    kk = k * kk_w_ref[...]
    kk = kk / jnp.maximum(jnp.sqrt(_split_dot(kk * kk, bd)), 1e-12)
    kt = k * (1.0 + (a - 1.0) * ka_ref[...])
    kk_s[...] = kk
    dec_s[...] = jnp.exp(log_w)
    beta_s[...] = kk * a
    kt_s[...] = kt
    r_s[...] = r
    v_s[...] = v
    g_s[...] = g
    v0 = v.astype(bf16).astype(f32)
    v1 = (v - v0).astype(bf16).astype(f32)
    v0_s[...] = v0
    v1_s[...] = v1
    v2_s[...] = v - v0 - v1

    def unit_block(u, bb):
        if prompt_mode:
            return bb * tc + u * n_st, n_st, 0
        return u * UNIT_COLS, UNIT_COLS, bb * n_st

    def vt_body(u, _):
        for bb in range(n_sb):
            blk0, blk_n, roff = unit_block(u, bb)
            rows = pl.ds(pl.multiple_of(blk0, 8), blk_n)
            for part, ref in enumerate((v0_s, v1_s, v2_s)):
                stage_s[part * n_st:(part + 1) * n_st, :] = ref[rows, :][roff:roff + n_st]
            vts_s[u * n_sb + bb] = jnp.transpose(stage_s[...])
        return 0

    lax.fori_loop(0, n_units, vt_body, 0)

    lane = lax.broadcasted_iota(jnp.int32, (FOX_HD, LANES), 1) % FOX_HD
    pairs = [(bb, pr) for bb in range(n_sb) for pr in range(RW_PAIRS)]

    def pack_pairs(tiles):
        return jnp.concatenate([jnp.concatenate(tiles[n:n + 2], axis=1) for n in range(0, len(tiles), 2)], axis=0)

    def unpack_pairs(res, n_tiles):
        return [res[(n // 2) * FOX_HD:(n // 2 + 1) * FOX_HD, (n % 2) * LANES:(n % 2 + 1) * LANES]
                for n in range(n_tiles)]

    def head_sums(tiles):
        return unpack_pairs(jnp.dot(pack_pairs(tiles), wsa_ref[...], preferred_element_type=f32), len(tiles))

    def unit_body(u, _):
        rowvecs, states, v_lhs = [], [], []
        for bb, pr in pairs:
            blk0, blk_n, roff = unit_block(u, bb)
            rows = pl.ds(pl.multiple_of(blk0, 8), blk_n)
            sl = slice(pr * LANES, (pr + 1) * LANES)
            rowvecs.append(tuple(ref[rows, sl][roff:roff + n_st] for ref in (kk_s, dec_s, beta_s, kt_s, r_s)))
            states.append(s_ref[u * n_sb + bb if not prompt_mode else bb, pr])
            vt = vts_s[u * n_sb + bb, sl, :]
            v_lhs.append((vt[0:FOX_HD] + pltpu.roll(vt[FOX_HD:LANES], 3 * n_st, axis=1)).astype(bf16))
        v_lhs = pack_pairs(v_lhs)
        ots = [jnp.zeros((FOX_HD, LANES), f32) for _ in range(RW_PAIRS)]
        for j in range(n_st):
            jr = slice(j, j + 1)
            pk = [(s * kk_b[jr]).astype(bf16) for (kk_b, _, _, _, _), s in zip(rowvecs, states)]
            sa_all = head_sums(pk)
            vb_all = unpack_pairs(jnp.dot(v_lhs, e_ref[j], preferred_element_type=f32), len(pairs))
            q_lhs = []
            for n, (_, w_b, be_b, kt_b, r_b) in enumerate(rowvecs):
                states[n] = states[n] * w_b[jr] - sa_all[n] * be_b[jr] + vb_all[n] * kt_b[jr]
                q_lhs.append((states[n] * r_b[jr]).astype(bf16))
            res_all = head_sums(q_lhs)
            for n, (bb, pr) in enumerate(pairs):
                ots[pr] = jnp.where(lane == bb * n_st + j, res_all[n], ots[pr])
        for n, (bb, pr) in enumerate(pairs):
            s_ref[u * n_sb + bb if not prompt_mode else bb, pr] = states[n]
        for pr in range(RW_PAIRS):
            ots_s[u, pr * FOX_HD:(pr + 1) * FOX_HD, :] = ots[pr]
        return 0

    lax.fori_loop(0, n_units, unit_body, 0)

    def ot_body(u, _):
        t = jnp.transpose(ots_s[u])
        ta, tb = t[0:UNIT_COLS], t[FOX_HD:FOX_HD + UNIT_COLS]
        o16 = jnp.concatenate(
            [x[:, pr * FOX_HD:(pr + 1) * FOX_HD] for pr in range(RW_PAIRS) for x in (ta, tb)], axis=1)
        if prompt_mode:
            for bb in range(n_sb):
                o_s[pl.ds(pl.multiple_of(bb * tc + u * n_st, 8), n_st), :] = o16[bb * n_st:(bb + 1) * n_st]
        else:
            o_s[pl.ds(pl.multiple_of(u * UNIT_COLS, 8), UNIT_COLS), :] = o16
        return 0

    lax.fori_loop(0, n_units, ot_body, 0)

    o = o_s[...]
    inv = 1.0 / FOX_HD
    mean = _split_dot(o, bd) * inv
    d = o - mean
    var = _split_dot(d * d, bd) * inv
    o_gn = d * lax.rsqrt(var + GN_EPS) * lnw_ref[...] + lnb_ref[...]
    bonus = _split_dot(r_s[...] * kt_s[...] * rk_ref[...], bd) * v_s[...]
    out = (o_gn + bonus) * g_s[...]
    o_ref[...] = out.reshape(o_ref.shape).astype(o_ref.dtype)


def _rwkv_weights(rw_mu, rw_w0, rw_w_up, rw_a0, rw_a_up, rw_g_up, rw_k_k, rw_k_a, rw_r_k, rw_ln_w, rw_ln_b):
    row = lambda a: a.reshape(1, -1)
    wup = jnp.concatenate([rw_w_up, jnp.zeros_like(rw_a_up)], axis=0).astype(bf16)
    aup = jnp.concatenate([jnp.zeros_like(rw_w_up), rw_a_up], axis=0).astype(bf16)
    return (row(rw_mu), row(rw_w0), wup, row(rw_a0), aup, rw_g_up.astype(bf16), row(rw_k_k), row(rw_k_a),
            row(rw_r_k), row(rw_ln_w), row(rw_ln_b), _block_diag_ones(RW_W, FOX_HD))


def _rwkv_scratch(n_rows, n_sb):
    n_units = n_rows // UNIT_COLS
    rows = [pltpu.VMEM((n_rows, RW_W), f32) for _ in range(10)]
    return rows + [pltpu.VMEM((n_units * n_sb, RW_W, LANES), f32),
                   pltpu.VMEM((n_units, RW_PAIRS * FOX_HD, LANES), f32),
                   pltpu.VMEM((LANES, RW_W), f32), pltpu.VMEM((n_rows, RW_W), f32)]


def _rwkv_selectors(n_st):
    ob = _block_diag_ones(LANES, FOX_HD)
    wsa = _block_diag_ones(2 * LANES, FOX_HD)
    shape = (n_st, 2 * LANES, 2 * LANES)
    r = lax.broadcasted_iota(jnp.int32, shape, 1)
    c = lax.broadcasted_iota(jnp.int32, shape, 2)
    j = lax.broadcasted_iota(jnp.int32, shape, 0)
    col = r % LANES
    e = ((r // LANES == c // LANES) & (col < 6 * n_st) & (col % n_st == j)
         & (col // (3 * n_st) == (c % LANES) // FOX_HD)).astype(bf16)
    return wsa, ob, e


def _rwkv_prompt(prw, weights, *, tc):
    nb, ns, _ = prw.shape
    const = lambda a: pl.BlockSpec(a.shape, lambda j: (0,) * a.ndim)
    n_st = UNIT_COLS // nb
    weights = tuple(weights) + _rwkv_selectors(n_st)
    return pl.pallas_call(
        functools.partial(_rwkv_body, n_blk_b=nb, tc=tc, n_sb=nb, n_st=n_st, prompt_mode=True),
        grid=(ns // tc,),
        in_specs=[pl.BlockSpec((nb, tc, RW_IN), lambda j: (0, j, 0))] + [const(w) for w in weights],
        out_specs=(pl.BlockSpec((nb, tc, RW_W), lambda j: (0, j, 0)),
                   pl.BlockSpec((nb, RW_PAIRS, FOX_HD, LANES), lambda j: (0, 0, 0, 0))),
        out_shape=(jax.ShapeDtypeStruct((nb, ns, RW_W), bf16),
                   jax.ShapeDtypeStruct((nb, RW_PAIRS, FOX_HD, LANES), f32)),
        scratch_shapes=[pltpu.VMEM((nb, 8, RW_IN), f32)] + _rwkv_scratch(nb * tc, nb),
        compiler_params=_cparams(("arbitrary",)),
        name="rwkv_prompt",
    )(prw, *weights)


def _rwkv_sample(prw_rows, prev_first_rows, state, weights, *, seq, rows_per_blk):
    n_rows = prw_rows.shape[0]
    nb_blk = rows_per_blk // seq
    const = lambda a: pl.BlockSpec(a.shape, lambda j: (0,) * a.ndim)
    n_sb = UNIT_COLS // seq
    weights = tuple(weights) + _rwkv_selectors(seq)
    return pl.pallas_call(
        functools.partial(_rwkv_body, n_blk_b=nb_blk, tc=seq, n_sb=n_sb, n_st=seq, prompt_mode=False),
        grid=(n_rows // rows_per_blk,),
        in_specs=[pl.BlockSpec((rows_per_blk, RW_IN), lambda j: (j, 0)),
                  pl.BlockSpec((rows_per_blk, RW_IN), lambda j: (j, 0)),
                  pl.BlockSpec((nb_blk, RW_PAIRS, FOX_HD, LANES), lambda j: (j, 0, 0, 0))]
                 + [const(w) for w in weights],
        out_specs=(pl.BlockSpec((rows_per_blk, RW_W), lambda j: (j, 0)),
                   pl.BlockSpec((nb_blk, RW_PAIRS, FOX_HD, LANES), lambda j: (j, 0, 0, 0))),
        out_shape=(jax.ShapeDtypeStruct((n_rows, RW_W), bf16),
                   jax.ShapeDtypeStruct((n_rows // seq, RW_PAIRS, FOX_HD, LANES), f32)),
        scratch_shapes=_rwkv_scratch(rows_per_blk, n_sb),
        compiler_params=_cparams(("arbitrary",)),
        name="rwkv_sample",
    )(prw_rows, prev_first_rows, state, *weights)


def _pack_state(state):
    b = state.shape[0]
    return state.reshape(b, RW_PAIRS, 2, FOX_HD, FOX_HD).transpose(0, 1, 3, 2, 4).reshape(b, RW_PAIRS, FOX_HD, LANES)


def _unpack_state(packed):
    b = packed.shape[0]
    return packed.reshape(b, RW_PAIRS, FOX_HD, 2, FOX_HD).transpose(0, 1, 3, 2, 4).reshape(b, 2 * RW_PAIRS, FOX_HD, FOX_HD)


def _fox_sample_body(pt_ref, *refs, n_pg, n_q):
    del pt_ref
    q_ref, kn_ref, vn_ref, lfn_ref, tri_ref, hm_ref = refs[:6]
    k_refs = refs[6:6 + n_pg]
    v_refs = refs[6 + n_pg:6 + 2 * n_pg]
    lf_refs = refs[6 + 2 * n_pg:6 + 3 * n_pg]
    o_ref, m_s, l_s, acc_s, carry_s = refs[6 + 3 * n_pg:]
    jg = pl.program_id(1)
    n_rows = n_q * FOX_HEADS
    page = tri_ref.shape[0]

    @pl.when(jg == 0)
    def _():
        m_s[...] = jnp.full_like(m_s, NEG_INF)
        l_s[...] = jnp.zeros_like(l_s)
        acc_s[...] = jnp.zeros_like(acc_s)
        carry_s[...] = jnp.zeros_like(carry_s)

    hm = hm_ref[...]
    q4 = q_ref[0]
    qm = jnp.concatenate([jnp.broadcast_to(q4[i:i + 1], (FOX_HEADS, FOX_W)) for i in range(n_q)], axis=0) * hm
    tri = tri_ref[...]

    def cum_bias(lf):
        ct = _split_dot(lf, tri) + carry_s[...]
        carry_s[...] = jnp.broadcast_to(ct[:, page - 1:page], carry_s.shape)
        return jnp.concatenate([ct] * n_q, axis=0)

    def update(s_all, pv):
        m_old = m_s[...]
        m_new = jnp.maximum(m_old, jnp.max(s_all, axis=1, keepdims=True))
        alpha = jnp.exp(m_old - m_new)
        p = jnp.exp(s_all - m_new)
        l_s[...] = l_s[...] * alpha + jnp.sum(p, axis=1, keepdims=True)
        acc_s[...] = acc_s[...] * alpha + pv(p.astype(bf16))
        m_s[...] = m_new

    def pv_pages(p):
        acc = jnp.zeros((n_rows, FOX_W), f32)
        for g in range(n_pg):
            acc = acc + lax.dot_general(p[:, g * page:(g + 1) * page], v_refs[g][0].astype(bf16),
                                        (((1,), (1,)), ((), ())), preferred_element_type=f32)
        return acc

    s_list = []
    for g in range(n_pg):
        s = jnp.dot(qm, k_refs[g][0].astype(bf16), preferred_element_type=f32)
        s_list.append(s - cum_bias(lf_refs[g][0]))
    update(jnp.concatenate(s_list, axis=1), pv_pages)

    @pl.when(jg == pl.num_programs(1) - 1)
    def _():
        pad = jnp.zeros((page - n_q, FOX_W), bf16)
        kn = jnp.concatenate([kn_ref[0], pad], axis=0)
        vn = jnp.concatenate([vn_ref[0], pad], axis=0)
        s = lax.dot_general(qm, kn, (((1,), (1,)), ((), ())), preferred_element_type=f32)
        bias = cum_bias(lfn_ref[0])
        key = lax.broadcasted_iota(jnp.int32, (n_rows, page), 1)
        qpos = lax.broadcasted_iota(jnp.int32, (n_rows, page), 0) // FOX_HEADS
        update(jnp.where(key <= qpos, s - bias, NEG_INF),
               lambda p: jnp.dot(p, vn, preferred_element_type=f32))
        o = acc_s[...] / l_s[...] * hm.astype(f32)
        o_ref[0] = jnp.concatenate(
            [jnp.sum(o[i * FOX_HEADS:(i + 1) * FOX_HEADS], axis=0, keepdims=True) for i in range(n_q)],
            axis=0).astype(o_ref.dtype)


def _fox_sample(page_table, q, kn, vn, lfn_t, cache_kt, cache_vt, cache_lft, *, n_pg):
    nb, n_q, _ = q.shape
    n_pages = page_table.shape[1]
    page = cache_kt.shape[2]
    n_rows = n_q * FOX_HEADS
    tri = (lax.broadcasted_iota(jnp.int32, (page, page), 0)
           <= lax.broadcasted_iota(jnp.int32, (page, page), 1)).astype(bf16)
    hm = (lax.broadcasted_iota(jnp.int32, (n_rows, FOX_W), 0) % FOX_HEADS
          == lax.broadcasted_iota(jnp.int32, (n_rows, FOX_W), 1) // FOX_HD).astype(bf16)
    per_b = lambda shape: pl.BlockSpec((1,) + shape, lambda b, j, pt: (b, 0, 0))
    const = lambda a: pl.BlockSpec(a.shape, lambda b, j, pt: (0, 0))
    paged = lambda shape, g: pl.BlockSpec((1,) + shape, lambda b, j, pt: (pt[b, j * n_pg + g], 0, 0))
    in_specs = ([per_b((n_q, FOX_W))] * 3 + [per_b((FOX_HEADS, page)), const(tri), const(hm)]
                + [paged((FOX_W, page), g) for g in range(n_pg)]
                + [paged((FOX_W, page), g) for g in range(n_pg)]
                + [paged((FOX_HEADS, page), g) for g in range(n_pg)])
    grid_spec = pltpu.PrefetchScalarGridSpec(
        num_scalar_prefetch=1, grid=(nb, n_pages // n_pg), in_specs=in_specs,
        out_specs=pl.BlockSpec((1, n_q, FOX_W), lambda b, j, pt: (b, 0, 0)),
        scratch_shapes=[pltpu.VMEM((n_rows, 1), f32), pltpu.VMEM((n_rows, 1), f32),
                        pltpu.VMEM((n_rows, FOX_W), f32), pltpu.VMEM((FOX_HEADS, page), f32)])
    return pl.pallas_call(
        functools.partial(_fox_sample_body, n_pg=n_pg, n_q=n_q),
        grid_spec=grid_spec,
        out_shape=jax.ShapeDtypeStruct((nb, n_q, FOX_W), bf16),
        compiler_params=_cparams(("arbitrary", "arbitrary")),
        name="fox_sample",
    )(page_table, q, kn, vn, lfn_t, tri, hm, *([cache_kt] * n_pg), *([cache_vt] * n_pg), *([cache_lft] * n_pg))


def _mem_kv_body(m_ref, g_ref, w_ref, knw_ref, k_ref, v_ref):
    n = _rms(m_ref[0], g_ref[...]).astype(bf16)
    kv = jnp.dot(n, w_ref[...], preferred_element_type=f32)
    for h in range(MEM_HEADS):
        sl = slice(h * MEM_HD, (h + 1) * MEM_HD)
        k_ref[0, :, sl] = _rms(kv[:, sl], knw_ref[...])
    v_ref[0] = kv[:, MEM_W:]


def _mem_kv(mem, mem_norm_w, w_mem_kv, mem_kn_w):
    nb, nm, _ = mem.shape
    const = lambda shape: pl.BlockSpec(shape, lambda b: (0,) * len(shape))
    out = jax.ShapeDtypeStruct((nb, nm, MEM_W), f32)
    return pl.pallas_call(
        _mem_kv_body,
        grid=(nb,),
        in_specs=[pl.BlockSpec((1, nm, D_MODEL), lambda b: (b, 0, 0)), const((1, D_MODEL)),
                  const((D_MODEL, 2 * MEM_W)), const((1, MEM_HD))],
        out_specs=(pl.BlockSpec((1, nm, MEM_W), lambda b: (b, 0, 0)),) * 2,
        out_shape=(out, out),
        compiler_params=_cparams(("arbitrary",)),
        name="mem_kv",
    )(mem, mem_norm_w.reshape(1, D_MODEL), w_mem_kv.astype(bf16), mem_kn_w.reshape(1, MEM_HD))


def _mem_attend_body(q_ref, k_ref, v_ref, o_ref):
    q = q_ref[0]
    for h in range(MEM_HEADS):
        sl = slice(h * MEM_HD, (h + 1) * MEM_HD)
        k = k_ref[0, :, sl].astype(bf16)
        v = v_ref[0, :, sl].astype(bf16)
        s = lax.dot_general(q[:, sl], k, (((1,), (1,)), ((), ())), preferred_element_type=f32) * MEM_SCALE
        e = jnp.exp(s - jnp.max(s, axis=1, keepdims=True))
        p = e / jnp.sum(e, axis=1, keepdims=True)
        o_ref[0, :, sl] = jnp.dot(p.astype(bf16), v, preferred_element_type=f32).astype(o_ref.dtype)


def _mem_attend(q, mk, mv, *, tq):
    nb, ns, _ = q.shape
    nm = mk.shape[1]
    kv = pl.BlockSpec((1, nm, MEM_W), lambda b, j: (b, 0, 0))
    return pl.pallas_call(
        _mem_attend_body,
        grid=(nb, ns // tq),
        in_specs=[pl.BlockSpec((1, tq, MEM_W), lambda b, j: (b, j, 0)), kv, kv],
        out_specs=pl.BlockSpec((1, tq, MEM_W), lambda b, j: (b, j, 0)),
        out_shape=jax.ShapeDtypeStruct((nb, ns, MEM_W), bf16),
        compiler_params=_cparams(("arbitrary", "arbitrary")),
        name="mem_attend",
    )(q, mk, mv)


def _finish_body(x_ref, of_ref, or_ref, om_ref, g1_ref, wg_ref, wb_ref, wo_ref, g2_ref, h_ref, xn2_ref):
    x = x_ref[...]
    xn = _rms(x, g1_ref[...]).astype(bf16)
    merged = jnp.zeros_like(x)
    for n, br in enumerate((of_ref, or_ref, om_ref)):
        gate = _sigmoid(jnp.dot(xn, wg_ref[:, n * D_MODEL:(n + 1) * D_MODEL], preferred_element_type=f32))
        merged = merged + gate * jnp.dot(br[...], wb_ref[n], preferred_element_type=f32)
    h = x + jnp.dot(merged.astype(bf16), wo_ref[...], preferred_element_type=f32)
    h_ref[...] = h
    xn2_ref[...] = _rms(h, g2_ref[...]).astype(bf16)


def _finish(x, o_fox, o_rw, o_mem, norm1_w, w_gate, w_branch, w_out, norm2_w, *, tile):
    n = x.shape[0]
    const = lambda shape: pl.BlockSpec(shape, lambda j: (0,) * len(shape))
    tok = lambda w: pl.BlockSpec((tile, w), lambda j: (j, 0))
    return pl.pallas_call(
        _finish_body,
        grid=(n // tile,),
        in_specs=[tok(D_MODEL), tok(FOX_W), tok(RW_W), tok(MEM_W), const((1, D_MODEL)),
                  const((D_MODEL, 3 * D_MODEL)), const((3, FOX_W, D_MODEL)), const((D_MODEL, D_MODEL)),
                  const((1, D_MODEL))],
        out_specs=(tok(D_MODEL), tok(D_MODEL)),
        out_shape=(jax.ShapeDtypeStruct((n, D_MODEL), f32), jax.ShapeDtypeStruct((n, D_MODEL), bf16)),
        compiler_params=_cparams(("arbitrary",)),
        name="finish",
    )(x, o_fox, o_rw, o_mem, norm1_w.reshape(1, D_MODEL), w_gate, w_branch.astype(bf16), w_out.astype(bf16),
      norm2_w.reshape(1, D_MODEL))


_PEER_SLOTS = [(ra, rb) for ra in range(PEER_TOPK) for rb in range(PEER_TOPK) if (ra + 1) * (rb + 1) <= PEER_TOPK]
_PEER_SLOT_ROWS = -(-len(_PEER_SLOTS) // 8) * 8


def _top_rows(s, n_take):
    iota = lax.broadcasted_iota(jnp.int32, s.shape, 0).astype(f32)
    big = float(s.shape[0])
    vals, idxs = [], []
    for _ in range(n_take):
        m = jnp.max(s, axis=0, keepdims=True)
        idx = jnp.min(jnp.where(s == m, iota, big), axis=0, keepdims=True)
        vals.append(m)
        idxs.append(idx)
        s = jnp.where(iota == idx, -jnp.inf, s)
    return vals, idxs


def _route_body(x_ref, wq_ref, k1_ref, k2_ref, a_ref, b_ref, g_ref,
                q_s, cand_s, ea_s, eb_s, a_s, b_s, g_s, at_s, bt_s, gt_s):
    q_s[...] = lax.dot_general(wq_ref[...], x_ref[...], (((1,), (1,)), ((), ())),
                               preferred_element_type=f32).astype(bf16)
    cand_s[...] = jnp.full_like(cand_s, -jnp.inf)
    ea_s[...] = jnp.zeros_like(ea_s)
    eb_s[...] = jnp.zeros_like(eb_s)

    def head(h, _):
        base = pl.multiple_of(h * 2 * PEER_HALF, 2 * PEER_HALF)
        s1 = jnp.dot(k1_ref[h], q_s[pl.ds(base, PEER_HALF), :], preferred_element_type=f32)
        s2 = jnp.dot(k2_ref[h], q_s[pl.ds(base + PEER_HALF, PEER_HALF), :], preferred_element_type=f32)
        v1, i1 = _top_rows(s1, PEER_TOPK)
        v2, i2 = _top_rows(s2, PEER_TOPK)
        for slot, (ra, rb) in enumerate(_PEER_SLOTS):
            cand_s[slot:slot + 1, :] = v1[ra] + v2[rb]
            ea_s[slot:slot + 1, :] = i1[ra]
            eb_s[slot:slot + 1, :] = i2[rb]
        cand = cand_s[...]
        ea = ea_s[...]
        eb = eb_s[...]
        iota = lax.broadcasted_iota(jnp.int32, cand.shape, 0).astype(f32)
        scs = []
        for r in range(PEER_TOPK):
            m = jnp.max(cand, axis=0, keepdims=True)
            slot = jnp.min(jnp.where(cand == m, iota, float(_PEER_SLOT_ROWS)), axis=0, keepdims=True)
            hit = iota == slot
            a_s[r:r + 1, :] = jnp.sum(jnp.where(hit, ea, 0.0), axis=0, keepdims=True)
            b_s[r:r + 1, :] = jnp.sum(jnp.where(hit, eb, 0.0), axis=0, keepdims=True)
            cand = jnp.where(hit, -jnp.inf, cand)
            scs.append(m)
        es = [jnp.exp(sc - scs[0]) for sc in scs]
        z = es[0]
        for e in es[1:]:
            z = z + e
        for r in range(PEER_TOPK):
            g_s[r:r + 1, :] = es[r] / z
        rows = pl.ds(pl.multiple_of(h * PEER_TOPK, PEER_TOPK), PEER_TOPK)
        at_s[rows, :] = a_s[...]
        bt_s[rows, :] = b_s[...]
        gt_s[rows, :] = g_s[...]
        return 0

    lax.fori_loop(0, PEER_HEADS, head, 0)
    a_ref[...] = jnp.transpose(at_s[...])
    b_ref[...] = jnp.transpose(bt_s[...])
    g_ref[...] = jnp.transpose(gt_s[...])


def _route(xn2, wq_t, keys1, keys2, *, tile):
    n = xn2.shape[0]
    n_slots = PEER_HEADS * PEER_TOPK
    const = lambda shape: pl.BlockSpec(shape, lambda j: (0,) * len(shape))
    out = jax.ShapeDtypeStruct((n, n_slots), f32)
    ospec = pl.BlockSpec((tile, n_slots), lambda j: (j, 0))
    cand = lambda: pltpu.VMEM((_PEER_SLOT_ROWS, tile), f32)
    top = lambda: pltpu.VMEM((PEER_TOPK, tile), f32)
    full = lambda: pltpu.VMEM((n_slots, tile), f32)
    return pl.pallas_call(
        _route_body,
        grid=(n // tile,),
        in_specs=[pl.BlockSpec((tile, D_MODEL), lambda j: (j, 0)), const(wq_t.shape), const(keys1.shape),
                  const(keys2.shape)],
        out_specs=(ospec, ospec, ospec),
        out_shape=(out, out, out),
        scratch_shapes=[pltpu.VMEM((wq_t.shape[0], tile), bf16), cand(), cand(), cand(), top(), top(), top(),
                        full(), full(), full()],
        compiler_params=_cparams(("arbitrary",)),
        name="peer_route",
    )(xn2, wq_t, keys1, keys2)


G_PITCH = PEER_NK + 8
EXPERT_SUB = 512


def _gelu_tanh(x):
    return 0.5 * x * (1.0 + jnp.tanh(math.sqrt(2.0 / math.pi) * (x + 0.044715 * (x * x * x))))


def _peer_body(x_ref, a_ref, b_ref, g_ref, h_ref, u_ref, v_ref, y_ref, gs_ref, acc_ref):
    tile = x_ref.shape[0]
    c = pl.program_id(1)
    n_exp = u_ref.shape[0]

    @pl.when(c == 0)
    def _():
        acc_ref[...] = jnp.zeros_like(acc_ref)
        sub = lax.broadcasted_iota(jnp.int32, (PEER_NK, LANES), 0).astype(f32)

        def tok8(t8, _):
            rows = pl.ds(pl.multiple_of(t8 * 8, 8), 8)
            a8, b8, g8 = a_ref[rows, :], b_ref[rows, :], g_ref[rows, :]
            for j in range(8):
                hit_a = sub == a8[j:j + 1]
                gate = jnp.where(hit_a, g8[j:j + 1], 0.0)
                g_hi = gate.astype(bf16)
                g_lo = (gate - g_hi.astype(f32)).astype(bf16)
                bt = jnp.where(sub == b8[j:j + 1], 1.0, 0.0).astype(bf16)
                gt = lax.dot_general(jnp.concatenate([g_hi, g_lo], axis=1), jnp.concatenate([bt, bt], axis=1),
                                     (((1,), (1,)), ((), ())), preferred_element_type=f32)
                gs_ref[pl.ds(pl.multiple_of((t8 * 8 + j) * G_PITCH, 8), PEER_NK), :] = gt
            return 0

        lax.fori_loop(0, tile // 8, tok8, 0)

    x = x_ref[...]
    acc = acc_ref[...]
    for s in range(n_exp // EXPERT_SUB):
        e0 = s * EXPERT_SUB
        act = lax.dot_general(x, u_ref[e0:e0 + EXPERT_SUB, :], (((1,), (1,)), ((), ())),
                              preferred_element_type=f32)
        i1 = (c * n_exp + e0) // PEER_NK
        gates = jnp.concatenate(
            [gs_ref[pl.ds(i1 + k, tile, stride=G_PITCH), :] for k in range(EXPERT_SUB // PEER_NK)], axis=1)
        w = (gates * _gelu_tanh(act)).astype(bf16)
        acc = acc + jnp.dot(w, v_ref[e0:e0 + EXPERT_SUB, :], preferred_element_type=f32)
    acc_ref[...] = acc

    @pl.when(c == pl.num_programs(1) - 1)
    def _():
        y_ref[...] = h_ref[...] + acc


def _peer(xn2, a, b, g, h, u_bf, v_bf, *, tile, chunk):
    n = xn2.shape[0]
    n_slots = a.shape[1]
    n_experts = u_bf.shape[0]
    tok = lambda w: pl.BlockSpec((tile, w), lambda j, c: (j, 0))
    tab = pl.BlockSpec((chunk, D_MODEL), lambda j, c: (c, 0))
    return pl.pallas_call(
        _peer_body,
        grid=(n // tile, n_experts // chunk),
        in_specs=[tok(D_MODEL), tok(n_slots), tok(n_slots), tok(n_slots), tok(D_MODEL), tab, tab],
        out_specs=tok(D_MODEL),
        out_shape=jax.ShapeDtypeStruct((n, D_MODEL), f32),
        scratch_shapes=[pltpu.VMEM((tile * G_PITCH, PEER_NK), f32), pltpu.VMEM((tile, D_MODEL), f32)],
        compiler_params=_cparams(("arbitrary", "arbitrary")),
        name="peer_experts",
    )(xn2, a, b, g, h, u_bf, v_bf)


TOK_TILE = 256
ATT_BLK = 512
RW_CHUNK = 128
RW_SAMPLE_ROWS = 128
PAGES_PER_STEP = 16
EXPERT_CHUNK = 2048


def _merge_and_peer(x2, o_fox, o_rw, o_mem, shared):
    h, xn2 = _finish(x2, o_fox, o_rw, o_mem, shared["norm1_w"], shared["w_gate"], shared["w_branch"],
                     shared["w_out"], shared["norm2_w"], tile=TOK_TILE)
    a, b, g = _route(xn2, shared["wq_t"], shared["keys1"], shared["keys2"], tile=TOK_TILE)
    return _peer(xn2, a, b, g, h, shared["u_bf"], shared["v_bf"], tile=TOK_TILE, chunk=EXPERT_CHUNK)


def kernel(x_prompt, x_sample, cache_fox_k, cache_fox_v, cache_fox_logf, cache_mem_k, cache_mem_v, state_rwkv, state_rwkv_shift, page_table, mem_prompt, norm1_w, w_in, fox_b_f, fox_qn_w, fox_kn_w, rw_mu, rw_w0, rw_w_up, rw_a0, rw_a_up, rw_g_up, rw_k_k, rw_k_a, rw_r_k, rw_ln_w, rw_ln_b, mem_norm_w, w_mem_kv, mem_qn_w, mem_kn_w, w_branch, w_out, norm2_w, peer_w_q, peer_keys1, peer_keys2, peer_u, peer_v):
    bp, sp, _ = x_prompt.shape
    db, ds, _ = x_sample.shape
    w_pack, bf_pad, qnw, knw, mqw, w_gate = _prep_inproj_weights(w_in, fox_b_f, fox_qn_w, fox_kn_w, mem_qn_w)
    g1 = norm1_w.reshape(1, D_MODEL)
    rw_w = _rwkv_weights(rw_mu, rw_w0, rw_w_up, rw_a0, rw_a_up, rw_g_up, rw_k_k, rw_k_a, rw_r_k, rw_ln_w, rw_ln_b)
    shared = dict(norm1_w=norm1_w, w_gate=w_gate, w_branch=w_branch, w_out=w_out, norm2_w=norm2_w,
                  wq_t=peer_w_q.T.astype(bf16), keys1=peer_keys1.astype(bf16), keys2=peer_keys2.astype(bf16),
                  u_bf=peer_u.astype(bf16), v_bf=peer_v.astype(bf16))

    qh, kh, vh, k_p, v_p, logf_p, prw_p, mq_p = _inproj(
        x_prompt, g1, w_pack, bf_pad, qnw, knw, mqw, head_major=True, tile=TOK_TILE)
    o_fox = _fox_prompt(qh, kh, vh, blk=ATT_BLK)
    o_rw, st_p = _rwkv_prompt(prw_p, rw_w, tc=RW_CHUNK)
    mem_k_p, mem_v_p = _mem_kv(mem_prompt, mem_norm_w, w_mem_kv, mem_kn_w)
    o_mem = _mem_attend(mq_p, mem_k_p, mem_v_p, tq=ATT_BLK)
    n_p = bp * sp
    y_prompt = _merge_and_peer(x_prompt.reshape(n_p, D_MODEL), o_fox.reshape(n_p, FOX_W), o_rw.reshape(n_p, RW_W),
                               o_mem.reshape(n_p, MEM_W), shared).reshape(bp, sp, D_MODEL)

    n_s = db * ds
    qt, kt, vt, k_s, v_s, logf_s, prw_s, mq_s = _inproj(
        x_sample.reshape(1, n_s, D_MODEL), g1, w_pack, bf_pad, qnw, knw, mqw, head_major=False, tile=TOK_TILE)
    pool, page = cache_fox_k.shape[:2]
    logf_s = logf_s.reshape(db, ds, LANES)[:, :, :FOX_HEADS]
    lfn_t = jnp.pad(jnp.swapaxes(logf_s, 1, 2), ((0, 0), (0, 0), (0, page - ds)))
    pos_minor = lambda a: jnp.transpose(a, (0, 2, 3, 1)).reshape(pool, FOX_W, page)
    o_fox_s = _fox_sample(page_table, qt.reshape(db, ds, FOX_W), kt.reshape(db, ds, FOX_W), vt.reshape(db, ds, FOX_W),
                          lfn_t, pos_minor(cache_fox_k), pos_minor(cache_fox_v), jnp.swapaxes(cache_fox_logf, 1, 2),
                          n_pg=PAGES_PER_STEP)
    prw_rows = prw_s.reshape(n_s, RW_IN)
    o_rw_s, st_s = _rwkv_sample(prw_rows, jnp.repeat(state_rwkv_shift, ds, axis=0), _pack_state(state_rwkv), rw_w,
                                seq=ds, rows_per_blk=RW_SAMPLE_ROWS)
    nm = cache_mem_k.shape[1]
    o_mem_s = _mem_attend(mq_s.reshape(db, ds, MEM_W), cache_mem_k.reshape(db, nm, MEM_W),
                          cache_mem_v.reshape(db, nm, MEM_W), tq=ds)
    y_sample = _merge_and_peer(x_sample.reshape(n_s, D_MODEL), o_fox_s.reshape(n_s, FOX_W), o_rw_s,
                               o_mem_s.reshape(n_s, MEM_W), shared).reshape(db, ds, D_MODEL)

    heads = lambda a, b, s: a.reshape(b, s, FOX_HEADS, FOX_HD)
    return (y_prompt, y_sample,
            heads(k_p, bp, sp), heads(v_p, bp, sp), logf_p[:, :, :FOX_HEADS],
            _unpack_state(st_p), prw_p[:, -1],
            mem_k_p.reshape(bp, nm, MEM_HEADS, MEM_HD), mem_v_p.reshape(bp, nm, MEM_HEADS, MEM_HD),
            heads(k_s, db, ds), heads(v_s, db, ds), logf_s,
            _unpack_state(st_s), prw_rows.reshape(db, ds, RW_IN)[:, -1])
```

```python
import functools
import math

import jax
import jax.numpy as jnp
from jax import lax
from jax.experimental import pallas as pl
from jax.experimental.pallas import tpu as pltpu

f32 = jnp.float32
bf16 = jnp.bfloat16

D_MODEL = 1024
RMS_EPS = 1e-6
NEG_INF = -1e30
FOX_HEADS = 8
FOX_HD = 64
FOX_W = 512
FOX_SCALE = FOX_HD ** -0.5
LOG2E = math.log2(math.e)
RW_W = 512
RW_IN = 1792
RW_DECAY = math.exp(-0.5)
GN_EPS = 64e-5
MEM_HEADS = 4
MEM_HD = 128
MEM_W = 512
MEM_SCALE = MEM_HD ** -0.5
PEER_HEADS = 8
PEER_NK = 128
PEER_HALF = 128
PEER_TOPK = 16

LANES = 128
VMEM_LIMIT = 56 * 1024 * 1024

_PK_Q, _PK_K, _PK_V, _PK_RW, _PK_MQ, _PK_F, _PK_END = 0, 512, 1024, 1536, 3328, 3840, 3968


def _cparams(sem):
    return pltpu.CompilerParams(dimension_semantics=sem, vmem_limit_bytes=VMEM_LIMIT)


def _block_diag_ones(width, group):
    r = lax.broadcasted_iota(jnp.int32, (width, width), 0) // group
    c = lax.broadcasted_iota(jnp.int32, (width, width), 1) // group
    return (r == c).astype(bf16)


def _split_dot(x, w):
    hi = x.astype(bf16)
    lo = (x - hi.astype(f32)).astype(bf16)
    return (jnp.dot(hi, w, preferred_element_type=f32)
            + jnp.dot(lo, w, preferred_element_type=f32))


def _rms(x, g):
    return x * lax.rsqrt(jnp.mean(x * x, axis=-1, keepdims=True) + RMS_EPS) * g


def _log_sigmoid(x):
    return jnp.minimum(x, 0.0) - jnp.log1p(jnp.exp(-jnp.abs(x)))


def _sigmoid(x):
    return 1.0 / (1.0 + jnp.exp(-x))


def _inproj_body(x_ref, g_ref, w_ref, bf_ref, qnw_ref, knw_ref, mqw_ref, bd_ref, tri_ref, sc_ref, ones_ref,
                 q_ref, kb_ref, vb_ref, kp_ref, vp_ref, logf_ref, prw_ref, mq_ref,
                 carry_ref, *, head_major):
    @pl.when(pl.program_id(1) == 0)
    def _():
        carry_ref[...] = jnp.zeros_like(carry_ref)

    x = x_ref[0]
    xn = _rms(x, g_ref[...]).astype(bf16)
    p = jnp.dot(xn, w_ref[...], preferred_element_type=f32)
    q = p[:, _PK_Q:_PK_K]
    k = p[:, _PK_K:_PK_V]
    v = p[:, _PK_V:_PK_RW]
    prw_ref[0] = p[:, _PK_RW:_PK_MQ]
    bd = bd_ref[...]
    qn = q * lax.rsqrt(_split_dot(q * q, bd) * (1.0 / FOX_HD) + RMS_EPS) * qnw_ref[...]
    kn = k * lax.rsqrt(_split_dot(k * k, bd) * (1.0 / FOX_HD) + RMS_EPS) * knw_ref[...]
    kp_ref[0] = kn
    vp_ref[0] = v
    kb = kn.astype(bf16)
    vb = v.astype(bf16)
    lane = lax.broadcasted_iota(jnp.int32, (x.shape[0], LANES), 1)
    logf = jnp.where(lane < FOX_HEADS, _log_sigmoid(p[:, _PK_F:_PK_END] + bf_ref[...]), 0.0)
    logf_ref[0] = logf
    for h in range(MEM_HEADS):
        sl = slice(_PK_MQ + h * MEM_HD, _PK_MQ + (h + 1) * MEM_HD)
        mq_ref[0, :, h * MEM_HD:(h + 1) * MEM_HD] = _rms(p[:, sl], mqw_ref[...]).astype(bf16)
    if not head_major:
        q_ref[0] = (qn * FOX_SCALE).astype(bf16)
        kb_ref[0] = kb
        vb_ref[0] = vb
        return
    c = _split_dot_left(tri_ref[...], logf) + carry_ref[0:1, :]
    carry_ref[0:1, :] = c[x.shape[0] - 1:x.shape[0], :]
    c2 = c * LOG2E
    c_hi = c2.astype(bf16)
    rest = c2 - c_hi.astype(f32)
    c_mid = rest.astype(bf16)
    parts = (c_hi, c_mid, (rest - c_mid.astype(f32)).astype(bf16))
    extra_q = ones_ref[0:1, :]
    extra_k = ones_ref[1:2, :]
    for i, part in enumerate(parts):
        extra_q = extra_q + jnp.dot(part, sc_ref[i], preferred_element_type=f32)
        extra_k = extra_k + jnp.dot(part, sc_ref[len(parts) + i], preferred_element_type=f32)
    qs = (qn * (FOX_SCALE * LOG2E)).astype(bf16)
    for h in range(FOX_HEADS):
        pair = slice((h // 2) * LANES, (h // 2 + 1) * LANES)
        own = slice(h * LANES, (h + 1) * LANES)
        in_own_half = lane // FOX_HD == h % 2
        q_ref[0, h] = jnp.where(in_own_half, qs[:, pair], extra_q[:, own].astype(bf16))
        kb_ref[0, h] = jnp.where(in_own_half, kb[:, pair], extra_k[:, own].astype(bf16))
        vb_ref[0, h] = vb[:, h * FOX_HD:(h + 1) * FOX_HD]


def _split_dot_left(w, x):
    hi = x.astype(bf16)
    lo = (x - hi.astype(f32)).astype(bf16)
    return (jnp.dot(w, hi, preferred_element_type=f32)
            + jnp.dot(w, lo, preferred_element_type=f32))


def _inproj(x3, norm1_w, w_pack, bf_pad, qnw, knw, mqw, *, head_major, tile):
    nb, ns, _ = x3.shape
    nt = ns // tile
    bd = _block_diag_ones(FOX_W, FOX_HD)
    tri = (lax.broadcasted_iota(jnp.int32, (tile, tile), 0)
           >= lax.broadcasted_iota(jnp.int32, (tile, tile), 1)).astype(bf16)
    const = lambda shape: pl.BlockSpec(shape, lambda b, j: (0,) * len(shape))
    tok = lambda w: pl.BlockSpec((1, tile, w), lambda b, j: (b, j, 0))
    if head_major:
        hm = lambda w: (jax.ShapeDtypeStruct((nb, FOX_HEADS, ns, w), bf16),
                        pl.BlockSpec((1, FOX_HEADS, tile, w), lambda b, j: (b, 0, j, 0)))
        qkv = (hm(LANES), hm(LANES), hm(FOX_HD))
    else:
        qkv = ((jax.ShapeDtypeStruct((nb, ns, FOX_W), bf16), tok(FOX_W)),) * 3
    n_parts = 3
    lanes = lax.broadcasted_iota(jnp.int32, (2 * n_parts, LANES, FOX_HEADS * LANES), 2)
    rows = lax.broadcasted_iota(jnp.int32, (2 * n_parts, LANES, FOX_HEADS * LANES), 1)
    slot = lax.broadcasted_iota(jnp.int32, (2 * n_parts, LANES, FOX_HEADS * LANES), 0)
    head = lanes // LANES
    off = lanes % LANES - (1 - head % 2) * FOX_HD
    sc = jnp.where((rows == head) & (off == slot), jnp.where(slot < n_parts, 1.0, -1.0), 0.0).astype(bf16)
    off1 = off[0, 0:1, :]
    ones = jnp.concatenate([(off1 >= n_parts) & (off1 < 2 * n_parts), (off1 >= 0) & (off1 < n_parts)],
                           axis=0).astype(f32)
    out_shape = tuple(s for s, _ in qkv) + (
        jax.ShapeDtypeStruct((nb, ns, FOX_W), f32),
        jax.ShapeDtypeStruct((nb, ns, FOX_W), f32),
        jax.ShapeDtypeStruct((nb, ns, LANES), f32),
        jax.ShapeDtypeStruct((nb, ns, RW_IN), f32),
        jax.ShapeDtypeStruct((nb, ns, MEM_W), bf16),
    )
    out_specs = tuple(s for _, s in qkv) + (tok(FOX_W), tok(FOX_W), tok(LANES), tok(RW_IN), tok(MEM_W))
    return pl.pallas_call(
        functools.partial(_inproj_body, head_major=head_major),
        grid=(nb, nt),
        in_specs=[tok(D_MODEL), const((1, D_MODEL)), const((D_MODEL, _PK_END)), const((1, LANES)),
                  const((1, FOX_W)), const((1, FOX_W)), const((1, MEM_HD)), const((FOX_W, FOX_W)),
                  const((tile, tile)), const(sc.shape), const(ones.shape)],
        out_specs=out_specs,
        out_shape=out_shape,
        scratch_shapes=[pltpu.VMEM((8, LANES), f32)],
        compiler_params=_cparams(("arbitrary", "arbitrary")),
        name="inproj",
    )(x3, norm1_w, w_pack, bf_pad, qnw, knw, mqw, bd, tri, sc, ones)


def _prep_inproj_weights(w_in, fox_b_f, fox_qn_w, fox_kn_w, mem_qn_w):
    fox_in = 3 * FOX_W + FOX_HEADS
    w_fox = w_in[:, :fox_in]
    w_rw = w_in[:, fox_in:fox_in + RW_IN]
    w_mq = w_in[:, fox_in + RW_IN:fox_in + RW_IN + MEM_W]
    w_gate = w_in[:, fox_in + RW_IN + MEM_W:]
    w_f = jnp.pad(w_fox[:, 3 * FOX_W:], ((0, 0), (0, LANES - FOX_HEADS)))
    w_pack = jnp.concatenate([w_fox[:, :3 * FOX_W], w_rw, w_mq, w_f], axis=1).astype(bf16)
    bf_pad = jnp.pad(fox_b_f, (0, LANES - FOX_HEADS)).reshape(1, LANES)
    qnw = jnp.tile(fox_qn_w, FOX_HEADS).reshape(1, FOX_W)
    knw = jnp.tile(fox_kn_w, FOX_HEADS).reshape(1, FOX_W)
    return w_pack, bf_pad, qnw, knw, mem_qn_w.reshape(1, MEM_HD), w_gate.astype(bf16)


def _fox_prompt_body(q_ref, k_ref, v_ref, o_ref, *, blk):
    qi = pl.program_id(2)
    row = lax.broadcasted_iota(jnp.int32, (blk, blk), 0)
    col = lax.broadcasted_iota(jnp.int32, (blk, blk), 1)
    qs = [q_ref[0, hh] for hh in range(2)]

    def step(kj, carry, masked):
        start = pl.multiple_of(kj * blk, blk)
        new = []
        for hh, (m, l, acc) in enumerate(carry):
            k = k_ref[0, hh, pl.ds(start, blk), :]
            v = v_ref[0, hh, pl.ds(start, blk), :]
            s = lax.dot_general(qs[hh], k, (((1,), (1,)), ((), ())), preferred_element_type=f32)
            if masked:
                s = jnp.where(col <= row, s, NEG_INF)
            m_new = jnp.maximum(m, jnp.max(s, axis=1, keepdims=True))
            alpha = jnp.exp2(m - m_new)
            p = jnp.exp2(s - m_new)
            l = l * alpha + jnp.sum(p, axis=1, keepdims=True)
            acc = acc * alpha + jnp.dot(p.astype(bf16), v, preferred_element_type=f32)
            new.append((m_new, l, acc))
        return tuple(new)

    init = (jnp.full((blk, 1), NEG_INF, f32), jnp.zeros((blk, 1), f32), jnp.zeros((blk, FOX_HD), f32))
    carry = lax.fori_loop(0, qi, functools.partial(step, masked=False), (init, init))
    final = step(qi, carry, True)
    o_ref[0] = jnp.concatenate([acc / l for _, l, acc in final], axis=1).astype(o_ref.dtype)


def _fox_prompt(q, k, v, *, blk):
    nb, nh, ns, hd = v.shape
    pair = lambda w: pl.BlockSpec((1, 2, ns, w), lambda b, hp, qi: (b, hp, 0, 0))
    return pl.pallas_call(
        functools.partial(_fox_prompt_body, blk=blk),
        grid=(nb, nh // 2, ns // blk),
        in_specs=[pl.BlockSpec((1, 2, blk, q.shape[3]), lambda b, hp, qi: (b, hp, qi, 0)),
                  pair(k.shape[3]), pair(hd)],
        out_specs=pl.BlockSpec((1, blk, 2 * hd), lambda b, hp, qi: (b, qi, hp)),
        out_shape=jax.ShapeDtypeStruct((nb, ns, nh * hd), bf16),
        compiler_params=_cparams(("arbitrary", "arbitrary", "arbitrary")),
        name="fox_prompt",
    )(q, k, v)


UNIT_COLS = 16
RW_PAIRS = 4


def _rwkv_body(*refs, n_blk_b, tc, n_sb, n_st, prompt_mode):
    if prompt_mode:
        (p_ref, mu_ref, w0_ref, wup_ref, a0_ref, aup_ref, gup_ref, kk_w_ref, ka_ref, rk_ref, lnw_ref, lnb_ref,
         bd_ref, wsa_ref, ob_ref, e_ref, o_ref, s_ref, prev_ref, kk_s, dec_s, beta_s, kt_s, r_s, v_s, g_s,
         v0_s, v1_s, v2_s, vts_s, ots_s, stage_s, o_s) = refs
    else:
        (p_ref, pf_ref, sin_ref, mu_ref, w0_ref, wup_ref, a0_ref, aup_ref, gup_ref, kk_w_ref, ka_ref, rk_ref,
         lnw_ref, lnb_ref, bd_ref, wsa_ref, ob_ref, e_ref, o_ref, s_ref, kk_s, dec_s, beta_s, kt_s, r_s, v_s,
         g_s, v0_s, v1_s, v2_s, vts_s, ots_s, stage_s, o_s) = refs
    n_rows = n_blk_b * tc
    n_units = n_rows // UNIT_COLS
    step = pl.program_id(0)

    if prompt_mode:
        @pl.when(step == 0)
        def _():
            s_ref[...] = jnp.zeros_like(s_ref)
            prev_ref[...] = jnp.zeros_like(prev_ref)
    else:
        s_ref[...] = sin_ref[...]
    stage_s[...] = jnp.zeros_like(stage_s)

    p = p_ref[...].reshape(n_rows, RW_IN)
    rowi = lax.broadcasted_iota(jnp.int32, (n_rows, 1), 0)
    rolled = pltpu.roll(p, 1, axis=0)
    if prompt_mode:
        prev = rolled
        for bb in range(n_blk_b):
            prev = jnp.where(rowi == bb * tc, prev_ref[bb, 0:1, :], prev)
            prev_ref[bb, 0:1, :] = p[(bb + 1) * tc - 1:(bb + 1) * tc, :]
    else:
        prev = jnp.where(rowi % tc == 0, pf_ref[...], rolled)
    xs = p + (prev - p) * mu_ref[...]
    r = xs[:, 0:512]
    k = xs[:, 512:1024]
    v = xs[:, 1024:1536]
    zwa = xs[:, 1536:1664]
    zg = xs[:, 1664:1792]
    log_w = -RW_DECAY * _sigmoid(w0_ref[...] + jnp.dot(jnp.tanh(zwa).astype(bf16), wup_ref[...],
                                                        preferred_element_type=f32))
    a = _sigmoid(a0_ref[...] + jnp.dot(zwa.astype(bf16), aup_ref[...], preferred_element_type=f32))
    g = jnp.dot(_sigmoid(zg).astype(bf16), gup_ref[...], preferred_element_type=f32)
    bd = bd_ref[...]
    kk = k * kk_w_ref[...]
    kk = kk / jnp.maximum(jnp.sqrt(_split_dot(kk * kk, bd)), 1e-12)
    kt = k * (1.0 + (a - 1.0) * ka_ref[...])
    kk_s[...] = kk
    dec_s[...] = jnp.exp(log_w)
    beta_s[...] = kk * a
    kt_s[...] = kt
    r_s[...] = r
    v_s[...] = v
    g_s[...] = g
    v0 = v.astype(bf16).astype(f32)
    v1 = (v - v0).astype(bf16).astype(f32)
    v0_s[...] = v0
    v1_s[...] = v1
    v2_s[...] = v - v0 - v1

    def unit_block(u, bb):
        if prompt_mode:
            return bb * tc + u * n_st, n_st, 0
        return u * UNIT_COLS, UNIT_COLS, bb * n_st

    def vt_body(u, _):
        for bb in range(n_sb):
            blk0, blk_n, roff = unit_block(u, bb)
            rows = pl.ds(pl.multiple_of(blk0, 8), blk_n)
            for part, ref in enumerate((v0_s, v1_s, v2_s)):
                stage_s[part * n_st:(part + 1) * n_st, :] = ref[rows, :][roff:roff + n_st]
            vts_s[u * n_sb + bb] = jnp.transpose(stage_s[...])
        return 0

    lax.fori_loop(0, n_units, vt_body, 0)

    lane = lax.broadcasted_iota(jnp.int32, (FOX_HD, LANES), 1) % FOX_HD
    pairs = [(bb, pr) for bb in range(n_sb) for pr in range(RW_PAIRS)]

    def pack_pairs(tiles):
        return jnp.concatenate([jnp.concatenate(tiles[n:n + 2], axis=1) for n in range(0, len(tiles), 2)], axis=0)

    def unpack_pairs(res, n_tiles):
        return [res[(n // 2) * FOX_HD:(n // 2 + 1) * FOX_HD, (n % 2) * LANES:(n % 2 + 1) * LANES]
                for n in range(n_tiles)]

    def head_sums(tiles):
        return unpack_pairs(jnp.dot(pack_pairs(tiles), wsa_ref[...], preferred_element_type=f32), len(tiles))

    def unit_body(u, _):
        rowvecs, states, v_lhs = [], [], []
        for bb, pr in pairs:
            blk0, blk_n, roff = unit_block(u, bb)
            rows = pl.ds(pl.multiple_of(blk0, 8), blk_n)
            sl = slice(pr * LANES, (pr + 1) * LANES)
            rowvecs.append(tuple(ref[rows, sl][roff:roff + n_st] for ref in (kk_s, dec_s, beta_s, kt_s, r_s)))
            states.append(s_ref[u * n_sb + bb if not prompt_mode else bb, pr])
            vt = vts_s[u * n_sb + bb, sl, :]
            v_lhs.append((vt[0:FOX_HD] + pltpu.roll(vt[FOX_HD:LANES], 3 * n_st, axis=1)).astype(bf16))
        v_lhs = pack_pairs(v_lhs)
        ots = [jnp.zeros((FOX_HD, LANES), f32) for _ in range(RW_PAIRS)]
        for j in range(n_st):
            jr = slice(j, j + 1)
            pk = [(s * kk_b[jr]).astype(bf16) for (kk_b, _, _, _, _), s in zip(rowvecs, states)]
            sa_all = head_sums(pk)
            vb_all = unpack_pairs(jnp.dot(v_lhs, e_ref[j], preferred_element_type=f32), len(pairs))
            q_lhs = []
            for n, (_, w_b, be_b, kt_b, r_b) in enumerate(rowvecs):
                states[n] = states[n] * w_b[jr] - sa_all[n] * be_b[jr] + vb_all[n] * kt_b[jr]
                q_lhs.append((states[n] * r_b[jr]).astype(bf16))
            res_all = head_sums(q_lhs)
            for n, (bb, pr) in enumerate(pairs):
                ots[pr] = jnp.where(lane == bb * n_st + j, res_all[n], ots[pr])
        for n, (bb, pr) in enumerate(pairs):
            s_ref[u * n_sb + bb if not prompt_mode else bb, pr] = states[n]
        for pr in range(RW_PAIRS):
            ots_s[u, pr * FOX_HD:(pr + 1) * FOX_HD, :] = ots[pr]
        return 0

    lax.fori_loop(0, n_units, unit_body, 0)

    def ot_body(u, _):
        t = jnp.transpose(ots_s[u])
        ta, tb = t[0:UNIT_COLS], t[FOX_HD:FOX_HD + UNIT_COLS]
        o16 = jnp.concatenate(
            [x[:, pr * FOX_HD:(pr + 1) * FOX_HD] for pr in range(RW_PAIRS) for x in (ta, tb)], axis=1)
        if prompt_mode:
            for bb in range(n_sb):
                o_s[pl.ds(pl.multiple_of(bb * tc + u * n_st, 8), n_st), :] = o16[bb * n_st:(bb + 1) * n_st]
        else:
            o_s[pl.ds(pl.multiple_of(u * UNIT_COLS, 8), UNIT_COLS), :] = o16
        return 0

    lax.fori_loop(0, n_units, ot_body, 0)

    o = o_s[...]
    inv = 1.0 / FOX_HD
    mean = _split_dot(o, bd) * inv
    d = o - mean
    var = _split_dot(d * d, bd) * inv
    o_gn = d * lax.rsqrt(var + GN_EPS) * lnw_ref[...] + lnb_ref[...]
    bonus = _split_dot(r_s[...] * kt_s[...] * rk_ref[...], bd) * v_s[...]
    out = (o_gn + bonus) * g_s[...]
    o_ref[...] = out.reshape(o_ref.shape).astype(o_ref.dtype)


def _rwkv_weights(rw_mu, rw_w0, rw_w_up, rw_a0, rw_a_up, rw_g_up, rw_k_k, rw_k_a, rw_r_k, rw_ln_w, rw_ln_b):
    row = lambda a: a.reshape(1, -1)
    wup = jnp.concatenate([rw_w_up, jnp.zeros_like(rw_a_up)], axis=0).astype(bf16)
    aup = jnp.concatenate([jnp.zeros_like(rw_w_up), rw_a_up], axis=0).astype(bf16)
    return (row(rw_mu), row(rw_w0), wup, row(rw_a0), aup, rw_g_up.astype(bf16), row(rw_k_k), row(rw_k_a),
            row(rw_r_k), row(rw_ln_w), row(rw_ln_b), _block_diag_ones(RW_W, FOX_HD))


def _rwkv_scratch(n_rows, n_sb):
    n_units = n_rows // UNIT_COLS
    rows = [pltpu.VMEM((n_rows, RW_W), f32) for _ in range(10)]
    return rows + [pltpu.VMEM((n_units * n_sb, RW_W, LANES), f32),
                   pltpu.VMEM((n_units, RW_PAIRS * FOX_HD, LANES), f32),
                   pltpu.VMEM((LANES, RW_W), f32), pltpu.VMEM((n_rows, RW_W), f32)]


def _rwkv_selectors(n_st):
    ob = _block_diag_ones(LANES, FOX_HD)
    wsa = _block_diag_ones(2 * LANES, FOX_HD)
    shape = (n_st, 2 * LANES, 2 * LANES)
    r = lax.broadcasted_iota(jnp.int32, shape, 1)
    c = lax.broadcasted_iota(jnp.int32, shape, 2)
    j = lax.broadcasted_iota(jnp.int32, shape, 0)
    col = r % LANES
    e = ((r // LANES == c // LANES) & (col < 6 * n_st) & (col % n_st == j)
         & (col // (3 * n_st) == (c % LANES) // FOX_HD)).astype(bf16)
    return wsa, ob, e


def _rwkv_prompt(prw, weights, *, tc):
    nb, ns, _ = prw.shape
    const = lambda a: pl.BlockSpec(a.shape, lambda j: (0,) * a.ndim)
    n_st = UNIT_COLS // nb
    weights = tuple(weights) + _rwkv_selectors(n_st)
    return pl.pallas_call(
        functools.partial(_rwkv_body, n_blk_b=nb, tc=tc, n_sb=nb, n_st=n_st, prompt_mode=True),
        grid=(ns // tc,),
        in_specs=[pl.BlockSpec((nb, tc, RW_IN), lambda j: (0, j, 0))] + [const(w) for w in weights],
        out_specs=(pl.BlockSpec((nb, tc, RW_W), lambda j: (0, j, 0)),
                   pl.BlockSpec((nb, RW_PAIRS, FOX_HD, LANES), lambda j: (0, 0, 0, 0))),
        out_shape=(jax.ShapeDtypeStruct((nb, ns, RW_W), bf16),
                   jax.ShapeDtypeStruct((nb, RW_PAIRS, FOX_HD, LANES), f32)),
        scratch_shapes=[pltpu.VMEM((nb, 8, RW_IN), f32)] + _rwkv_scratch(nb * tc, nb),
        compiler_params=_cparams(("arbitrary",)),
        name="rwkv_prompt",
    )(prw, *weights)


def _rwkv_sample(prw_rows, prev_first_rows, state, weights, *, seq, rows_per_blk):
    n_rows = prw_rows.shape[0]
    nb_blk = rows_per_blk // seq
    const = lambda a: pl.BlockSpec(a.shape, lambda j: (0,) * a.ndim)
    n_sb = UNIT_COLS // seq
    weights = tuple(weights) + _rwkv_selectors(seq)
    return pl.pallas_call(
        functools.partial(_rwkv_body, n_blk_b=nb_blk, tc=seq, n_sb=n_sb, n_st=seq, prompt_mode=False),
        grid=(n_rows // rows_per_blk,),
        in_specs=[pl.BlockSpec((rows_per_blk, RW_IN), lambda j: (j, 0)),
                  pl.BlockSpec((rows_per_blk, RW_IN), lambda j: (j, 0)),
                  pl.BlockSpec((nb_blk, RW_PAIRS, FOX_HD, LANES), lambda j: (j, 0, 0, 0))]
                 + [const(w) for w in weights],
        out_specs=(pl.BlockSpec((rows_per_blk, RW_W), lambda j: (j, 0)),
                   pl.BlockSpec((nb_blk, RW_PAIRS, FOX_HD, LANES), lambda j: (j, 0, 0, 0))),
        out_shape=(jax.ShapeDtypeStruct((n_rows, RW_W), bf16),
                   jax.ShapeDtypeStruct((n_rows // seq, RW_PAIRS, FOX_HD, LANES), f32)),
        scratch_shapes=_rwkv_scratch(rows_per_blk, n_sb),
        compiler_params=_cparams(("arbitrary",)),
        name="rwkv_sample",
    )(prw_rows, prev_first_rows, state, *weights)


def _pack_state(state):
    b = state.shape[0]
    return state.reshape(b, RW_PAIRS, 2, FOX_HD, FOX_HD).transpose(0, 1, 3, 2, 4).reshape(b, RW_PAIRS, FOX_HD, LANES)


def _unpack_state(packed):
    b = packed.shape[0]
    return packed.reshape(b, RW_PAIRS, FOX_HD, 2, FOX_HD).transpose(0, 1, 3, 2, 4).reshape(b, 2 * RW_PAIRS, FOX_HD, FOX_HD)


def _fox_sample_body(pt_ref, *refs, n_pg, n_q):
    del pt_ref
    q_ref, kn_ref, vn_ref, lfn_ref, tri_ref, hm_ref = refs[:6]
    k_refs = refs[6:6 + n_pg]
    v_refs = refs[6 + n_pg:6 + 2 * n_pg]
    lf_refs = refs[6 + 2 * n_pg:6 + 3 * n_pg]
    o_ref, m_s, l_s, acc_s, carry_s = refs[6 + 3 * n_pg:]
    jg = pl.program_id(1)
    n_rows = n_q * FOX_HEADS
    page = tri_ref.shape[0]

    @pl.when(jg == 0)
    def _():
        m_s[...] = jnp.full_like(m_s, NEG_INF)
        l_s[...] = jnp.zeros_like(l_s)
        acc_s[...] = jnp.zeros_like(acc_s)
        carry_s[...] = jnp.zeros_like(carry_s)

    hm = hm_ref[...]
    q4 = q_ref[0]
    qm = jnp.concatenate([jnp.broadcast_to(q4[i:i + 1], (FOX_HEADS, FOX_W)) for i in range(n_q)], axis=0) * hm
    tri = tri_ref[...]

    def cum_bias(lf):
        ct = _split_dot(lf, tri) + carry_s[...]
        carry_s[...] = jnp.broadcast_to(ct[:, page - 1:page], carry_s.shape)
        return jnp.concatenate([ct] * n_q, axis=0)

    def update(s_all, pv):
        m_old = m_s[...]
        m_new = jnp.maximum(m_old, jnp.max(s_all, axis=1, keepdims=True))
        alpha = jnp.exp(m_old - m_new)
        p = jnp.exp(s_all - m_new)
        l_s[...] = l_s[...] * alpha + jnp.sum(p, axis=1, keepdims=True)
        acc_s[...] = acc_s[...] * alpha + pv(p.astype(bf16))
        m_s[...] = m_new

    def pv_pages(p):
        acc = jnp.zeros((n_rows, FOX_W), f32)
        for g in range(n_pg):
            acc = acc + lax.dot_general(p[:, g * page:(g + 1) * page], v_refs[g][0].astype(bf16),
                                        (((1,), (1,)), ((), ())), preferred_element_type=f32)
        return acc

    s_list = []
    for g in range(n_pg):
        s = jnp.dot(qm, k_refs[g][0].astype(bf16), preferred_element_type=f32)
        s_list.append(s - cum_bias(lf_refs[g][0]))
    update(jnp.concatenate(s_list, axis=1), pv_pages)

    @pl.when(jg == pl.num_programs(1) - 1)
    def _():
        pad = jnp.zeros((page - n_q, FOX_W), bf16)
        kn = jnp.concatenate([kn_ref[0], pad], axis=0)
        vn = jnp.concatenate([vn_ref[0], pad], axis=0)
        s = lax.dot_general(qm, kn, (((1,), (1,)), ((), ())), preferred_element_type=f32)
        bias = cum_bias(lfn_ref[0])
        key = lax.broadcasted_iota(jnp.int32, (n_rows, page), 1)
        qpos = lax.broadcasted_iota(jnp.int32, (n_rows, page), 0) // FOX_HEADS
        update(jnp.where(key <= qpos, s - bias, NEG_INF),
               lambda p: jnp.dot(p, vn, preferred_element_type=f32))
        o = acc_s[...] / l_s[...] * hm.astype(f32)
        o_ref[0] = jnp.concatenate(
            [jnp.sum(o[i * FOX_HEADS:(i + 1) * FOX_HEADS], axis=0, keepdims=True) for i in range(n_q)],
            axis=0).astype(o_ref.dtype)


def _fox_sample(page_table, q, kn, vn, lfn_t, cache_kt, cache_vt, cache_lft, *, n_pg):
    nb, n_q, _ = q.shape
    n_pages = page_table.shape[1]
    page = cache_kt.shape[2]
    n_rows = n_q * FOX_HEADS
    tri = (lax.broadcasted_iota(jnp.int32, (page, page), 0)
           <= lax.broadcasted_iota(jnp.int32, (page, page), 1)).astype(bf16)
    hm = (lax.broadcasted_iota(jnp.int32, (n_rows, FOX_W), 0) % FOX_HEADS
          == lax.broadcasted_iota(jnp.int32, (n_rows, FOX_W), 1) // FOX_HD).astype(bf16)
    per_b = lambda shape: pl.BlockSpec((1,) + shape, lambda b, j, pt: (b, 0, 0))
    const = lambda a: pl.BlockSpec(a.shape, lambda b, j, pt: (0, 0))
    paged = lambda shape, g: pl.BlockSpec((1,) + shape, lambda b, j, pt: (pt[b, j * n_pg + g], 0, 0))
    in_specs = ([per_b((n_q, FOX_W))] * 3 + [per_b((FOX_HEADS, page)), const(tri), const(hm)]
                + [paged((FOX_W, page), g) for g in range(n_pg)]
                + [paged((FOX_W, page), g) for g in range(n_pg)]
                + [paged((FOX_HEADS, page), g) for g in range(n_pg)])
    grid_spec = pltpu.PrefetchScalarGridSpec(
        num_scalar_prefetch=1, grid=(nb, n_pages // n_pg), in_specs=in_specs,
        out_specs=pl.BlockSpec((1, n_q, FOX_W), lambda b, j, pt: (b, 0, 0)),
        scratch_shapes=[pltpu.VMEM((n_rows, 1), f32), pltpu.VMEM((n_rows, 1), f32),
                        pltpu.VMEM((n_rows, FOX_W), f32), pltpu.VMEM((FOX_HEADS, page), f32)])
    return pl.pallas_call(
        functools.partial(_fox_sample_body, n_pg=n_pg, n_q=n_q),
        grid_spec=grid_spec,
        out_shape=jax.ShapeDtypeStruct((nb, n_q, FOX_W), bf16),
        compiler_params=_cparams(("arbitrary", "arbitrary")),
        name="fox_sample",
    )(page_table, q, kn, vn, lfn_t, tri, hm, *([cache_kt] * n_pg), *([cache_vt] * n_pg), *([cache_lft] * n_pg))


def _mem_kv_body(m_ref, g_ref, w_ref, knw_ref, k_ref, v_ref):
    n = _rms(m_ref[0], g_ref[...]).astype(bf16)
    kv = jnp.dot(n, w_ref[...], preferred_element_type=f32)
    for h in range(MEM_HEADS):
        sl = slice(h * MEM_HD, (h + 1) * MEM_HD)
        k_ref[0, :, sl] = _rms(kv[:, sl], knw_ref[...])
    v_ref[0] = kv[:, MEM_W:]


def _mem_kv(mem, mem_norm_w, w_mem_kv, mem_kn_w):
    nb, nm, _ = mem.shape
    const = lambda shape: pl.BlockSpec(shape, lambda b: (0,) * len(shape))
    out = jax.ShapeDtypeStruct((nb, nm, MEM_W), f32)
    return pl.pallas_call(
        _mem_kv_body,
        grid=(nb,),
        in_specs=[pl.BlockSpec((1, nm, D_MODEL), lambda b: (b, 0, 0)), const((1, D_MODEL)),
                  const((D_MODEL, 2 * MEM_W)), const((1, MEM_HD))],
        out_specs=(pl.BlockSpec((1, nm, MEM_W), lambda b: (b, 0, 0)),) * 2,
        out_shape=(out, out),
        compiler_params=_cparams(("arbitrary",)),
        name="mem_kv",
    )(mem, mem_norm_w.reshape(1, D_MODEL), w_mem_kv.astype(bf16), mem_kn_w.reshape(1, MEM_HD))


def _mem_attend_body(q_ref, k_ref, v_ref, o_ref):
    q = q_ref[0]
    for h in range(MEM_HEADS):
        sl = slice(h * MEM_HD, (h + 1) * MEM_HD)
        k = k_ref[0, :, sl].astype(bf16)
        v = v_ref[0, :, sl].astype(bf16)
        s = lax.dot_general(q[:, sl], k, (((1,), (1,)), ((), ())), preferred_element_type=f32) * MEM_SCALE
        e = jnp.exp(s - jnp.max(s, axis=1, keepdims=True))
        p = e / jnp.sum(e, axis=1, keepdims=True)
        o_ref[0, :, sl] = jnp.dot(p.astype(bf16), v, preferred_element_type=f32).astype(o_ref.dtype)


def _mem_attend_rows_body(q_ref, k_ref, v_ref, o_ref):
    n_mem = k_ref.shape[1] // MEM_HEADS
    for i in range(q_ref.shape[0]):
        q = q_ref[i]
        for h in range(MEM_HEADS):
            sl = slice(h * MEM_HD, (h + 1) * MEM_HD)
            k = k_ref[i, pl.ds(h, n_mem, stride=MEM_HEADS), :].astype(bf16)
            v = v_ref[i, pl.ds(h, n_mem, stride=MEM_HEADS), :].astype(bf16)
            s = lax.dot_general(q[:, sl], k, (((1,), (1,)), ((), ())), preferred_element_type=f32) * MEM_SCALE
            e = jnp.exp(s - jnp.max(s, axis=1, keepdims=True))
            p = e / jnp.sum(e, axis=1, keepdims=True)
            o_ref[i, :, sl] = jnp.dot(p.astype(bf16), v, preferred_element_type=f32).astype(o_ref.dtype)


def _mem_attend_rows(q, mk_rows, mv_rows, *, nb_blk):
    nb, ns, _ = q.shape
    rows = mk_rows.shape[1]
    kv = pl.BlockSpec((nb_blk, rows, MEM_HD), lambda b: (b, 0, 0))
    qo = pl.BlockSpec((nb_blk, ns, MEM_W), lambda b: (b, 0, 0))
    return pl.pallas_call(
        _mem_attend_rows_body,
        grid=(nb // nb_blk,),
        in_specs=[qo, kv, kv],
        out_specs=qo,
        out_shape=jax.ShapeDtypeStruct((nb, ns, MEM_W), bf16),
        compiler_params=_cparams(("arbitrary",)),
        name="mem_attend_rows",
    )(q, mk_rows, mv_rows)


def _mem_attend(q, mk, mv, *, tq):
    nb, ns, _ = q.shape
    nm = mk.shape[1]
    kv = pl.BlockSpec((1, nm, MEM_W), lambda b, j: (b, 0, 0))
    return pl.pallas_call(
        _mem_attend_body,
        grid=(nb, ns // tq),
        in_specs=[pl.BlockSpec((1, tq, MEM_W), lambda b, j: (b, j, 0)), kv, kv],
        out_specs=pl.BlockSpec((1, tq, MEM_W), lambda b, j: (b, j, 0)),
        out_shape=jax.ShapeDtypeStruct((nb, ns, MEM_W), bf16),
        compiler_params=_cparams(("arbitrary", "arbitrary")),
        name="mem_attend",
    )(q, mk, mv)


def _finish_body(x_ref, of_ref, or_ref, om_ref, g1_ref, wg_ref, wb_ref, wo_ref, g2_ref, h_ref, xn2_ref):
    x = x_ref[...]
    xn = _rms(x, g1_ref[...]).astype(bf16)
    merged = jnp.zeros_like(x)
    for n, br in enumerate((of_ref, or_ref, om_ref)):
        gate = _sigmoid(jnp.dot(xn, wg_ref[:, n * D_MODEL:(n + 1) * D_MODEL], preferred_element_type=f32))
        merged = merged + gate * jnp.dot(br[...], wb_ref[n], preferred_element_type=f32)
    h = x + jnp.dot(merged.astype(bf16), wo_ref[...], preferred_element_type=f32)
    h_ref[...] = h
    xn2_ref[...] = _rms(h, g2_ref[...]).astype(bf16)


def _finish(x, o_fox, o_rw, o_mem, norm1_w, w_gate, w_branch, w_out, norm2_w, *, tile):
    n = x.shape[0]
    const = lambda shape: pl.BlockSpec(shape, lambda j: (0,) * len(shape))
    tok = lambda w: pl.BlockSpec((tile, w), lambda j: (j, 0))
    return pl.pallas_call(
        _finish_body,
        grid=(n // tile,),
        in_specs=[tok(D_MODEL), tok(FOX_W), tok(RW_W), tok(MEM_W), const((1, D_MODEL)),
                  const((D_MODEL, 3 * D_MODEL)), const((3, FOX_W, D_MODEL)), const((D_MODEL, D_MODEL)),
                  const((1, D_MODEL))],
        out_specs=(tok(D_MODEL), tok(D_MODEL)),
        out_shape=(jax.ShapeDtypeStruct((n, D_MODEL), f32), jax.ShapeDtypeStruct((n, D_MODEL), bf16)),
        compiler_params=_cparams(("arbitrary",)),
        name="finish",
    )(x, o_fox, o_rw, o_mem, norm1_w.reshape(1, D_MODEL), w_gate, w_branch.astype(bf16), w_out.astype(bf16),
      norm2_w.reshape(1, D_MODEL))


_PEER_SLOTS = [(ra, rb) for ra in range(PEER_TOPK) for rb in range(PEER_TOPK) if (ra + 1) * (rb + 1) <= PEER_TOPK]
_PEER_SLOT_ROWS = -(-len(_PEER_SLOTS) // 8) * 8


def _top_rows(s, n_take):
    iota = lax.broadcasted_iota(jnp.int32, s.shape, 0).astype(f32)
    big = float(s.shape[0])
    vals, idxs = [], []
    for _ in range(n_take):
        m = jnp.max(s, axis=0, keepdims=True)
        idx = jnp.min(jnp.where(s == m, iota, big), axis=0, keepdims=True)
        vals.append(m)
        idxs.append(idx)
        s = jnp.where(iota == idx, -jnp.inf, s)
    return vals, idxs


def _route_body(x_ref, wq_ref, k1_ref, k2_ref, a_ref, b_ref, g_ref,
                q_s, cand_s, ea_s, eb_s, a_s, b_s, g_s, at_s, bt_s, gt_s):
    q_s[...] = lax.dot_general(wq_ref[...], x_ref[...], (((1,), (1,)), ((), ())),
                               preferred_element_type=f32).astype(bf16)
    cand_s[...] = jnp.full_like(cand_s, -jnp.inf)
    ea_s[...] = jnp.zeros_like(ea_s)
    eb_s[...] = jnp.zeros_like(eb_s)

    def head(h, _):
        base = pl.multiple_of(h * 2 * PEER_HALF, 2 * PEER_HALF)
        s1 = jnp.dot(k1_ref[h], q_s[pl.ds(base, PEER_HALF), :], preferred_element_type=f32)
        s2 = jnp.dot(k2_ref[h], q_s[pl.ds(base + PEER_HALF, PEER_HALF), :], preferred_element_type=f32)
        v1, i1 = _top_rows(s1, PEER_TOPK)
        v2, i2 = _top_rows(s2, PEER_TOPK)
        for slot, (ra, rb) in enumerate(_PEER_SLOTS):
            cand_s[slot:slot + 1, :] = v1[ra] + v2[rb]
            ea_s[slot:slot + 1, :] = i1[ra]
            eb_s[slot:slot + 1, :] = i2[rb]
        cand = cand_s[...]
        ea = ea_s[...]
        eb = eb_s[...]
        iota = lax.broadcasted_iota(jnp.int32, cand.shape, 0).astype(f32)
        scs = []
        for r in range(PEER_TOPK):
            m = jnp.max(cand, axis=0, keepdims=True)
            slot = jnp.min(jnp.where(cand == m, iota, float(_PEER_SLOT_ROWS)), axis=0, keepdims=True)
            hit = iota == slot
            a_s[r:r + 1, :] = jnp.sum(jnp.where(hit, ea, 0.0), axis=0, keepdims=True)
            b_s[r:r + 1, :] = jnp.sum(jnp.where(hit, eb, 0.0), axis=0, keepdims=True)
            cand = jnp.where(hit, -jnp.inf, cand)
            scs.append(m)
        es = [jnp.exp(sc - scs[0]) for sc in scs]
        z = es[0]
        for e in es[1:]:
            z = z + e
        for r in range(PEER_TOPK):
            g_s[r:r + 1, :] = es[r] / z
        rows = pl.ds(pl.multiple_of(h * PEER_TOPK, PEER_TOPK), PEER_TOPK)
        at_s[rows, :] = a_s[...]
        bt_s[rows, :] = b_s[...]
        gt_s[rows, :] = g_s[...]
        return 0

    lax.fori_loop(0, PEER_HEADS, head, 0)
    a_ref[...] = jnp.transpose(at_s[...])
    b_ref[...] = jnp.transpose(bt_s[...])
    g_ref[...] = jnp.transpose(gt_s[...])


def _route(xn2, wq_t, keys1, keys2, *, tile):
    n = xn2.shape[0]
    n_slots = PEER_HEADS * PEER_TOPK
    const = lambda shape: pl.BlockSpec(shape, lambda j: (0,) * len(shape))
    out = jax.ShapeDtypeStruct((n, n_slots), f32)
    ospec = pl.BlockSpec((tile, n_slots), lambda j: (j, 0))
    cand = lambda: pltpu.VMEM((_PEER_SLOT_ROWS, tile), f32)
    top = lambda: pltpu.VMEM((PEER_TOPK, tile), f32)
    full = lambda: pltpu.VMEM((n_slots, tile), f32)
    return pl.pallas_call(
        _route_body,
        grid=(n // tile,),
        in_specs=[pl.BlockSpec((tile, D_MODEL), lambda j: (j, 0)), const(wq_t.shape), const(keys1.shape),
                  const(keys2.shape)],
        out_specs=(ospec, ospec, ospec),
        out_shape=(out, out, out),
        scratch_shapes=[pltpu.VMEM((wq_t.shape[0], tile), bf16), cand(), cand(), cand(), top(), top(), top(),
                        full(), full(), full()],
        compiler_params=_cparams(("arbitrary",)),
        name="peer_route",
    )(xn2, wq_t, keys1, keys2)


G_PITCH = PEER_NK + 8
EXPERT_SUB = 512


def _gelu_tanh(x):
    return 0.5 * x * (1.0 + jnp.tanh(math.sqrt(2.0 / math.pi) * (x + 0.044715 * (x * x * x))))


def _peer_body(x_ref, a_ref, b_ref, g_ref, h_ref, u_ref, v_ref, y_ref, gs_ref, acc_ref):
    tile = x_ref.shape[0]
    c = pl.program_id(1)
    n_exp = u_ref.shape[0]

    @pl.when(c == 0)
    def _():
        acc_ref[...] = jnp.zeros_like(acc_ref)
        sub = lax.broadcasted_iota(jnp.int32, (PEER_NK, LANES), 0).astype(f32)

        def tok8(t8, _):
            rows = pl.ds(pl.multiple_of(t8 * 8, 8), 8)
            a8, b8, g8 = a_ref[rows, :], b_ref[rows, :], g_ref[rows, :]
            for j in range(8):
                hit_a = sub == a8[j:j + 1]
                gate = jnp.where(hit_a, g8[j:j + 1], 0.0)
                g_hi = gate.astype(bf16)
                g_lo = (gate - g_hi.astype(f32)).astype(bf16)
                bt = jnp.where(sub == b8[j:j + 1], 1.0, 0.0).astype(bf16)
                gt = lax.dot_general(jnp.concatenate([g_hi, g_lo], axis=1), jnp.concatenate([bt, bt], axis=1),
                                     (((1,), (1,)), ((), ())), preferred_element_type=f32)
                gs_ref[pl.ds(pl.multiple_of((t8 * 8 + j) * G_PITCH, 8), PEER_NK), :] = gt
            return 0

        lax.fori_loop(0, tile // 8, tok8, 0)

    x = x_ref[...]
    acc = acc_ref[...]
    for s in range(n_exp // EXPERT_SUB):
        e0 = s * EXPERT_SUB
        act = lax.dot_general(x, u_ref[e0:e0 + EXPERT_SUB, :], (((1,), (1,)), ((), ())),
                              preferred_element_type=f32)
        i1 = (c * n_exp + e0) // PEER_NK
        gates = jnp.concatenate(
            [gs_ref[pl.ds(i1 + k, tile, stride=G_PITCH), :] for k in range(EXPERT_SUB // PEER_NK)], axis=1)
        w = (gates * _gelu_tanh(act)).astype(bf16)
        acc = acc + jnp.dot(w, v_ref[e0:e0 + EXPERT_SUB, :], preferred_element_type=f32)
    acc_ref[...] = acc

    @pl.when(c == pl.num_programs(1) - 1)
    def _():
        y_ref[...] = h_ref[...] + acc


def _peer(xn2, a, b, g, h, u_bf, v_bf, *, tile, chunk):
    n = xn2.shape[0]
    n_slots = a.shape[1]
    n_experts = u_bf.shape[0]
    tok = lambda w: pl.BlockSpec((tile, w), lambda j, c: (j, 0))
    tab = pl.BlockSpec((chunk, D_MODEL), lambda j, c: (c, 0))
    return pl.pallas_call(
        _peer_body,
        grid=(n // tile, n_experts // chunk),
        in_specs=[tok(D_MODEL), tok(n_slots), tok(n_slots), tok(n_slots), tok(D_MODEL), tab, tab],
        out_specs=tok(D_MODEL),
        out_shape=jax.ShapeDtypeStruct((n, D_MODEL), f32),
        scratch_shapes=[pltpu.VMEM((tile * G_PITCH, PEER_NK), f32), pltpu.VMEM((tile, D_MODEL), f32)],
        compiler_params=_cparams(("arbitrary", "arbitrary")),
        name="peer_experts",
    )(xn2, a, b, g, h, u_bf, v_bf)


TOK_TILE = 256
ATT_BLK = 512
RW_CHUNK = 128
RW_SAMPLE_ROWS = 128
PAGES_PER_STEP = 16
EXPERT_CHUNK = 2048
MEM_SAMPLE_BATCHES = 4


def _merge_and_peer(x2, o_fox, o_rw, o_mem, shared):
    h, xn2 = _finish(x2, o_fox, o_rw, o_mem, shared["norm1_w"], shared["w_gate"], shared["w_branch"],
                     shared["w_out"], shared["norm2_w"], tile=TOK_TILE)
    a, b, g = _route(xn2, shared["wq_t"], shared["keys1"], shared["keys2"], tile=TOK_TILE)
    return _peer(xn2, a, b, g, h, shared["u_bf"], shared["v_bf"], tile=TOK_TILE, chunk=EXPERT_CHUNK)


def kernel(x_prompt, x_sample, cache_fox_k, cache_fox_v, cache_fox_logf, cache_mem_k, cache_mem_v, state_rwkv, state_rwkv_shift, page_table, mem_prompt, norm1_w, w_in, fox_b_f, fox_qn_w, fox_kn_w, rw_mu, rw_w0, rw_w_up, rw_a0, rw_a_up, rw_g_up, rw_k_k, rw_k_a, rw_r_k, rw_ln_w, rw_ln_b, mem_norm_w, w_mem_kv, mem_qn_w, mem_kn_w, w_branch, w_out, norm2_w, peer_w_q, peer_keys1, peer_keys2, peer_u, peer_v):
    bp, sp, _ = x_prompt.shape
    db, ds, _ = x_sample.shape
    w_pack, bf_pad, qnw, knw, mqw, w_gate = _prep_inproj_weights(w_in, fox_b_f, fox_qn_w, fox_kn_w, mem_qn_w)
    g1 = norm1_w.reshape(1, D_MODEL)
    rw_w = _rwkv_weights(rw_mu, rw_w0, rw_w_up, rw_a0, rw_a_up, rw_g_up, rw_k_k, rw_k_a, rw_r_k, rw_ln_w, rw_ln_b)
    shared = dict(norm1_w=norm1_w, w_gate=w_gate, w_branch=w_branch, w_out=w_out, norm2_w=norm2_w,
                  wq_t=peer_w_q.T.astype(bf16), keys1=peer_keys1.astype(bf16), keys2=peer_keys2.astype(bf16),
                  u_bf=peer_u.astype(bf16), v_bf=peer_v.astype(bf16))

    qh, kh, vh, k_p, v_p, logf_p, prw_p, mq_p = _inproj(
        x_prompt, g1, w_pack, bf_pad, qnw, knw, mqw, head_major=True, tile=TOK_TILE)
    o_fox = _fox_prompt(qh, kh, vh, blk=ATT_BLK)
    o_rw, st_p = _rwkv_prompt(prw_p, rw_w, tc=RW_CHUNK)
    mem_k_p, mem_v_p = _mem_kv(mem_prompt, mem_norm_w, w_mem_kv, mem_kn_w)
    o_mem = _mem_attend(mq_p, mem_k_p, mem_v_p, tq=ATT_BLK)
    n_p = bp * sp
    y_prompt = _merge_and_peer(x_prompt.reshape(n_p, D_MODEL), o_fox.reshape(n_p, FOX_W), o_rw.reshape(n_p, RW_W),
                               o_mem.reshape(n_p, MEM_W), shared).reshape(bp, sp, D_MODEL)

    n_s = db * ds
    qt, kt, vt, k_s, v_s, logf_s, prw_s, mq_s = _inproj(
        x_sample.reshape(1, n_s, D_MODEL), g1, w_pack, bf_pad, qnw, knw, mqw, head_major=False, tile=TOK_TILE)
    pool, page = cache_fox_k.shape[:2]
    logf_s = logf_s.reshape(db, ds, LANES)[:, :, :FOX_HEADS]
    lfn_t = jnp.pad(jnp.swapaxes(logf_s, 1, 2), ((0, 0), (0, 0), (0, page - ds)))
    pos_minor = lambda a: jnp.transpose(a, (0, 2, 3, 1)).reshape(pool, FOX_W, page)
    o_fox_s = _fox_sample(page_table, qt.reshape(db, ds, FOX_W), kt.reshape(db, ds, FOX_W), vt.reshape(db, ds, FOX_W),
                          lfn_t, pos_minor(cache_fox_k), pos_minor(cache_fox_v), jnp.swapaxes(cache_fox_logf, 1, 2),
                          n_pg=PAGES_PER_STEP)
    prw_rows = prw_s.reshape(n_s, RW_IN)
    o_rw_s, st_s = _rwkv_sample(prw_rows, jnp.repeat(state_rwkv_shift, ds, axis=0), _pack_state(state_rwkv), rw_w,
                                seq=ds, rows_per_blk=RW_SAMPLE_ROWS)
    nm = cache_mem_k.shape[1]
    o_mem_s = _mem_attend_rows(mq_s.reshape(db, ds, MEM_W), cache_mem_k.reshape(db, nm * MEM_HEADS, MEM_HD),
                               cache_mem_v.reshape(db, nm * MEM_HEADS, MEM_HD), nb_blk=MEM_SAMPLE_BATCHES)
    y_sample = _merge_and_peer(x_sample.reshape(n_s, D_MODEL), o_fox_s.reshape(n_s, FOX_W), o_rw_s,
                               o_mem_s.reshape(n_s, MEM_W), shared).reshape(db, ds, D_MODEL)

    heads = lambda a, b, s: a.reshape(b, s, FOX_HEADS, FOX_HD)
    return (y_prompt, y_sample,
            heads(k_p, bp, sp), heads(v_p, bp, sp), logf_p[:, :, :FOX_HEADS],
            _unpack_state(st_p), prw_p[:, -1],
            mem_k_p.reshape(bp, nm, MEM_HEADS, MEM_HD), mem_v_p.reshape(bp, nm, MEM_HEADS, MEM_HD),
            heads(k_s, db, ds), heads(v_s, db, ds), logf_s,
            _unpack_state(st_s), prw_rows.reshape(db, ds, RW_IN)[:, -1])
```

```python
import functools
import math

import jax
import jax.numpy as jnp
from jax import lax
from jax.experimental import pallas as pl
from jax.experimental.pallas import tpu as pltpu

f32 = jnp.float32
bf16 = jnp.bfloat16

D_MODEL = 1024
RMS_EPS = 1e-6
NEG_INF = -1e30
FOX_HEADS = 8
FOX_HD = 64
FOX_W = 512
FOX_SCALE = FOX_HD ** -0.5
LOG2E = math.log2(math.e)
RW_W = 512
RW_IN = 1792
RW_DECAY = math.exp(-0.5)
GN_EPS = 64e-5
MEM_HEADS = 4
MEM_HD = 128
MEM_W = 512
MEM_SCALE = MEM_HD ** -0.5
PEER_HEADS = 8
PEER_NK = 128
PEER_HALF = 128
PEER_TOPK = 16

LANES = 128
VMEM_LIMIT = 56 * 1024 * 1024

_PK_Q, _PK_K, _PK_V, _PK_RW, _PK_MQ, _PK_F, _PK_END = 0, 512, 1024, 1536, 3328, 3840, 3968


def _cparams(sem):
    return pltpu.CompilerParams(dimension_semantics=sem, vmem_limit_bytes=VMEM_LIMIT)


def _block_diag_ones(width, group):
    r = lax.broadcasted_iota(jnp.int32, (width, width), 0) // group
    c = lax.broadcasted_iota(jnp.int32, (width, width), 1) // group
    return (r == c).astype(bf16)


def _split_dot(x, w):
    hi = x.astype(bf16)
    lo = (x - hi.astype(f32)).astype(bf16)
    return (jnp.dot(hi, w, preferred_element_type=f32)
            + jnp.dot(lo, w, preferred_element_type=f32))


def _rms(x, g):
    return x * lax.rsqrt(jnp.mean(x * x, axis=-1, keepdims=True) + RMS_EPS) * g


def _log_sigmoid(x):
    return jnp.minimum(x, 0.0) - jnp.log1p(jnp.exp(-jnp.abs(x)))


def _sigmoid(x):
    return 1.0 / (1.0 + jnp.exp(-x))


def _inproj_body(x_ref, g_ref, w_ref, bf_ref, qnw_ref, knw_ref, mqw_ref, bd_ref, tri_ref, sc_ref, ones_ref,
                 q_ref, kb_ref, vb_ref, kp_ref, vp_ref, logf_ref, prw_ref, mq_ref,
                 carry_ref, *, head_major):
    @pl.when(pl.program_id(1) == 0)
    def _():
        carry_ref[...] = jnp.zeros_like(carry_ref)

    x = x_ref[0]
    xn = _rms(x, g_ref[...]).astype(bf16)
    p = jnp.dot(xn, w_ref[...], preferred_element_type=f32)
    q = p[:, _PK_Q:_PK_K]
    k = p[:, _PK_K:_PK_V]
    v = p[:, _PK_V:_PK_RW]
    prw_ref[0] = p[:, _PK_RW:_PK_MQ]
    bd = bd_ref[...]
    qn = q * lax.rsqrt(_split_dot(q * q, bd) * (1.0 / FOX_HD) + RMS_EPS) * qnw_ref[...]
    kn = k * lax.rsqrt(_split_dot(k * k, bd) * (1.0 / FOX_HD) + RMS_EPS) * knw_ref[...]
    kp_ref[0] = kn
    vp_ref[0] = v
    kb = kn.astype(bf16)
    vb = v.astype(bf16)
    lane = lax.broadcasted_iota(jnp.int32, (x.shape[0], LANES), 1)
    logf = jnp.where(lane < FOX_HEADS, _log_sigmoid(p[:, _PK_F:_PK_END] + bf_ref[...]), 0.0)
    logf_ref[0] = logf
    for h in range(MEM_HEADS):
        sl = slice(_PK_MQ + h * MEM_HD, _PK_MQ + (h + 1) * MEM_HD)
        mq_ref[0, :, h * MEM_HD:(h + 1) * MEM_HD] = _rms(p[:, sl], mqw_ref[...]).astype(bf16)
    if not head_major:
        q_ref[0] = (qn * FOX_SCALE).astype(bf16)
        kb_ref[0] = kb
        vb_ref[0] = vb
        return
    c = _split_dot_left(tri_ref[...], logf) + carry_ref[0:1, :]
    carry_ref[0:1, :] = c[x.shape[0] - 1:x.shape[0], :]
    c2 = c * LOG2E
    c_hi = c2.astype(bf16)
    rest = c2 - c_hi.astype(f32)
    c_mid = rest.astype(bf16)
    parts = (c_hi, c_mid, (rest - c_mid.astype(f32)).astype(bf16))
    extra_q = ones_ref[0:1, :]
    extra_k = ones_ref[1:2, :]
    for i, part in enumerate(parts):
        extra_q = extra_q + jnp.dot(part, sc_ref[i], preferred_element_type=f32)
        extra_k = extra_k + jnp.dot(part, sc_ref[len(parts) + i], preferred_element_type=f32)
    qs = (qn * (FOX_SCALE * LOG2E)).astype(bf16)
    for h in range(FOX_HEADS):
        pair = slice((h // 2) * LANES, (h // 2 + 1) * LANES)
        own = slice(h * LANES, (h + 1) * LANES)
        in_own_half = lane // FOX_HD == h % 2
        q_ref[0, h] = jnp.where(in_own_half, qs[:, pair], extra_q[:, own].astype(bf16))
        kb_ref[0, h] = jnp.where(in_own_half, kb[:, pair], extra_k[:, own].astype(bf16))
        vb_ref[0, h] = vb[:, h * FOX_HD:(h + 1) * FOX_HD]


def _split_dot_left(w, x):
    hi = x.astype(bf16)
    lo = (x - hi.astype(f32)).astype(bf16)
    return (jnp.dot(w, hi, preferred_element_type=f32)
            + jnp.dot(w, lo, preferred_element_type=f32))


def _inproj(x3, norm1_w, w_pack, bf_pad, qnw, knw, mqw, *, head_major, tile):
    nb, ns, _ = x3.shape
    nt = ns // tile
    bd = _block_diag_ones(FOX_W, FOX_HD)
    tri = (lax.broadcasted_iota(jnp.int32, (tile, tile), 0)
           >= lax.broadcasted_iota(jnp.int32, (tile, tile), 1)).astype(bf16)
    const = lambda shape: pl.BlockSpec(shape, lambda b, j: (0,) * len(shape))
    tok = lambda w: pl.BlockSpec((1, tile, w), lambda b, j: (b, j, 0))
    if head_major:
        hm = lambda w: (jax.ShapeDtypeStruct((nb, FOX_HEADS, ns, w), bf16),
                        pl.BlockSpec((1, FOX_HEADS, tile, w), lambda b, j: (b, 0, j, 0)))
        qkv = (hm(LANES), hm(LANES), hm(FOX_HD))
    else:
        qkv = ((jax.ShapeDtypeStruct((nb, ns, FOX_W), bf16), tok(FOX_W)),) * 3
    n_parts = 3
    lanes = lax.broadcasted_iota(jnp.int32, (2 * n_parts, LANES, FOX_HEADS * LANES), 2)
    rows = lax.broadcasted_iota(jnp.int32, (2 * n_parts, LANES, FOX_HEADS * LANES), 1)
    slot = lax.broadcasted_iota(jnp.int32, (2 * n_parts, LANES, FOX_HEADS * LANES), 0)
    head = lanes // LANES
    off = lanes % LANES - (1 - head % 2) * FOX_HD
    sc = jnp.where((rows == head) & (off == slot), jnp.where(slot < n_parts, 1.0, -1.0), 0.0).astype(bf16)
    off1 = off[0, 0:1, :]
    ones = jnp.concatenate([(off1 >= n_parts) & (off1 < 2 * n_parts), (off1 >= 0) & (off1 < n_parts)],
                           axis=0).astype(f32)
    out_shape = tuple(s for s, _ in qkv) + (
        jax.ShapeDtypeStruct((nb, ns, FOX_W), f32),
        jax.ShapeDtypeStruct((nb, ns, FOX_W), f32),
        jax.ShapeDtypeStruct((nb, ns, LANES), f32),
        jax.ShapeDtypeStruct((nb, ns, RW_IN), f32),
        jax.ShapeDtypeStruct((nb, ns, MEM_W), bf16),
    )
    out_specs = tuple(s for _, s in qkv) + (tok(FOX_W), tok(FOX_W), tok(LANES), tok(RW_IN), tok(MEM_W))
    return pl.pallas_call(
        functools.partial(_inproj_body, head_major=head_major),
        grid=(nb, nt),
        in_specs=[tok(D_MODEL), const((1, D_MODEL)), const((D_MODEL, _PK_END)), const((1, LANES)),
                  const((1, FOX_W)), const((1, FOX_W)), const((1, MEM_HD)), const((FOX_W, FOX_W)),
                  const((tile, tile)), const(sc.shape), const(ones.shape)],
        out_specs=out_specs,
        out_shape=out_shape,
        scratch_shapes=[pltpu.VMEM((8, LANES), f32)],
        compiler_params=_cparams(("arbitrary", "arbitrary")),
        name="inproj",
    )(x3, norm1_w, w_pack, bf_pad, qnw, knw, mqw, bd, tri, sc, ones)


def _prep_inproj_weights(w_in, fox_b_f, fox_qn_w, fox_kn_w, mem_qn_w):
    fox_in = 3 * FOX_W + FOX_HEADS
    w_fox = w_in[:, :fox_in]
    w_rw = w_in[:, fox_in:fox_in + RW_IN]
    w_mq = w_in[:, fox_in + RW_IN:fox_in + RW_IN + MEM_W]
    w_gate = w_in[:, fox_in + RW_IN + MEM_W:]
    w_f = jnp.pad(w_fox[:, 3 * FOX_W:], ((0, 0), (0, LANES - FOX_HEADS)))
    w_pack = jnp.concatenate([w_fox[:, :3 * FOX_W], w_rw, w_mq, w_f], axis=1).astype(bf16)
    bf_pad = jnp.pad(fox_b_f, (0, LANES - FOX_HEADS)).reshape(1, LANES)
    qnw = jnp.tile(fox_qn_w, FOX_HEADS).reshape(1, FOX_W)
    knw = jnp.tile(fox_kn_w, FOX_HEADS).reshape(1, FOX_W)
    return w_pack, bf_pad, qnw, knw, mem_qn_w.reshape(1, MEM_HD), w_gate.astype(bf16)


def _fox_prompt_body(q_ref, k_ref, v_ref, o_ref, *, blk):
    qi = pl.program_id(2)
    row = lax.broadcasted_iota(jnp.int32, (blk, blk), 0)
    col = lax.broadcasted_iota(jnp.int32, (blk, blk), 1)
    qs = [q_ref[0, hh] for hh in range(2)]

    def step(kj, carry, masked):
        start = pl.multiple_of(kj * blk, blk)
        new = []
        for hh, (m, l, acc) in enumerate(carry):
            k = k_ref[0, hh, pl.ds(start, blk), :]
            v = v_ref[0, hh, pl.ds(start, blk), :]
            s = lax.dot_general(qs[hh], k, (((1,), (1,)), ((), ())), preferred_element_type=f32)
            if masked:
                s = jnp.where(col <= row, s, NEG_INF)
            m_new = jnp.maximum(m, jnp.max(s, axis=1, keepdims=True))
            alpha = jnp.exp2(m - m_new)
            p = jnp.exp2(s - m_new)
            l = l * alpha + jnp.sum(p, axis=1, keepdims=True)
            acc = acc * alpha + jnp.dot(p.astype(bf16), v, preferred_element_type=f32)
            new.append((m_new, l, acc))
        return tuple(new)

    init = (jnp.full((blk, 1), NEG_INF, f32), jnp.zeros((blk, 1), f32), jnp.zeros((blk, FOX_HD), f32))
    carry = lax.fori_loop(0, qi, functools.partial(step, masked=False), (init, init))
    final = step(qi, carry, True)
    o_ref[0] = jnp.concatenate([acc / l for _, l, acc in final], axis=1).astype(o_ref.dtype)


def _fox_prompt(q, k, v, *, blk):
    nb, nh, ns, hd = v.shape
    pair = lambda w: pl.BlockSpec((1, 2, ns, w), lambda b, hp, qi: (b, hp, 0, 0))
    return pl.pallas_call(
        functools.partial(_fox_prompt_body, blk=blk),
        grid=(nb, nh // 2, ns // blk),
        in_specs=[pl.BlockSpec((1, 2, blk, q.shape[3]), lambda b, hp, qi: (b, hp, qi, 0)),
                  pair(k.shape[3]), pair(hd)],
        out_specs=pl.BlockSpec((1, blk, 2 * hd), lambda b, hp, qi: (b, qi, hp)),
        out_shape=jax.ShapeDtypeStruct((nb, ns, nh * hd), bf16),
        compiler_params=_cparams(("arbitrary", "arbitrary", "arbitrary")),
        name="fox_prompt",
    )(q, k, v)


UNIT_COLS = 16
RW_PAIRS = 4


def _rwkv_body(*refs, n_blk_b, tc, n_sb, n_st, prompt_mode):
    if prompt_mode:
        (p_ref, mu_ref, w0_ref, wup_ref, a0_ref, aup_ref, gup_ref, kk_w_ref, ka_ref, rk_ref, lnw_ref, lnb_ref,
         bd_ref, wsa_ref, ob_ref, e_ref, o_ref, s_ref, prev_ref, kk_s, dec_s, beta_s, kt_s, r_s, v_s, g_s,
         v0_s, v1_s, v2_s, vts_s, ots_s, stage_s, o_s) = refs
    else:
        (p_ref, pf_ref, sin_ref, mu_ref, w0_ref, wup_ref, a0_ref, aup_ref, gup_ref, kk_w_ref, ka_ref, rk_ref,
         lnw_ref, lnb_ref, bd_ref, wsa_ref, ob_ref, e_ref, o_ref, s_ref, kk_s, dec_s, beta_s, kt_s, r_s, v_s,
         g_s, v0_s, v1_s, v2_s, vts_s, ots_s, stage_s, o_s) = refs
    n_rows = n_blk_b * tc
    n_units = n_rows // UNIT_COLS
    step = pl.program_id(0)

    if prompt_mode:
        @pl.when(step == 0)
        def _():
            s_ref[...] = jnp.zeros_like(s_ref)
            prev_ref[...] = jnp.zeros_like(prev_ref)
    else:
        s_ref[...] = sin_ref[...]
    stage_s[...] = jnp.zeros_like(stage_s)

    p = p_ref[...].reshape(n_rows, RW_IN)
    rowi = lax.broadcasted_iota(jnp.int32, (n_rows, 1), 0)
    rolled = pltpu.roll(p, 1, axis=0)
    if prompt_mode:
        prev = rolled
        for bb in range(n_blk_b):
            prev = jnp.where(rowi == bb * tc, prev_ref[bb, 0:1, :], prev)
            prev_ref[bb, 0:1, :] = p[(bb + 1) * tc - 1:(bb + 1) * tc, :]
    else:
        prev = jnp.where(rowi % tc == 0, pf_ref[...], rolled)
    xs = p + (prev - p) * mu_ref[...]
    r = xs[:, 0:512]
    k = xs[:, 512:1024]
    v = xs[:, 1024:1536]
    zwa = xs[:, 1536:1664]
    zg = xs[:, 1664:1792]
    log_w = -RW_DECAY * _sigmoid(w0_ref[...] + jnp.dot(jnp.tanh(zwa).astype(bf16), wup_ref[...],
                                                        preferred_element_type=f32))
    a = _sigmoid(a0_ref[...] + jnp.dot(zwa.astype(bf16), aup_ref[...], preferred_element_type=f32))
    g = jnp.dot(_sigmoid(zg).astype(bf16), gup_ref[...], preferred_element_type=f32)
    bd = bd_ref[...]
    kk = k * kk_w_ref[...]
    kk = kk / jnp.maximum(jnp.sqrt(_split_dot(kk * kk, bd)), 1e-12)
    kt = k * (1.0 + (a - 1.0) * ka_ref[...])
    kk_s[...] = kk
    dec_s[...] = jnp.exp(log_w)
    beta_s[...] = kk * a
    kt_s[...] = kt
    r_s[...] = r
    v_s[...] = v
    g_s[...] = g
    v0 = v.astype(bf16).astype(f32)
    v1 = (v - v0).astype(bf16).astype(f32)
    v0_s[...] = v0
    v1_s[...] = v1
    v2_s[...] = v - v0 - v1

    def unit_block(u, bb):
        if prompt_mode:
            return bb * tc + u * n_st, n_st, 0
        return u * UNIT_COLS, UNIT_COLS, bb * n_st

    def vt_body(u, _):
        for bb in range(n_sb):
            blk0, blk_n, roff = unit_block(u, bb)
            rows = pl.ds(pl.multiple_of(blk0, 8), blk_n)
            for part, ref in enumerate((v0_s, v1_s, v2_s)):
                stage_s[part * n_st:(part + 1) * n_st, :] = ref[rows, :][roff:roff + n_st]
            vts_s[u * n_sb + bb] = jnp.transpose(stage_s[...])
        return 0

    lax.fori_loop(0, n_units, vt_body, 0)

    lane = lax.broadcasted_iota(jnp.int32, (FOX_HD, LANES), 1) % FOX_HD
    pairs = [(bb, pr) for bb in range(n_sb) for pr in range(RW_PAIRS)]

    def pack_pairs(tiles):
        return jnp.concatenate([jnp.concatenate(tiles[n:n + 2], axis=1) for n in range(0, len(tiles), 2)], axis=0)

    def unpack_pairs(res, n_tiles):
        return [res[(n // 2) * FOX_HD:(n // 2 + 1) * FOX_HD, (n % 2) * LANES:(n % 2 + 1) * LANES]
                for n in range(n_tiles)]

    def head_sums(tiles):
        return unpack_pairs(jnp.dot(pack_pairs(tiles), wsa_ref[...], preferred_element_type=f32), len(tiles))

    def unit_body(u, _):
        rowvecs, states, v_lhs = [], [], []
        for bb, pr in pairs:
            blk0, blk_n, roff = unit_block(u, bb)
            rows = pl.ds(pl.multiple_of(blk0, 8), blk_n)
            sl = slice(pr * LANES, (pr + 1) * LANES)
            rowvecs.append(tuple(ref[rows, sl][roff:roff + n_st] for ref in (kk_s, dec_s, beta_s, kt_s, r_s)))
            states.append(s_ref[u * n_sb + bb if not prompt_mode else bb, pr])
            vt = vts_s[u * n_sb + bb, sl, :]
            v_lhs.append((vt[0:FOX_HD] + pltpu.roll(vt[FOX_HD:LANES], 3 * n_st, axis=1)).astype(bf16))
        v_lhs = pack_pairs(v_lhs)
        ots = [jnp.zeros((FOX_HD, LANES), f32) for _ in range(RW_PAIRS)]
        for j in range(n_st):
            jr = slice(j, j + 1)
            pk = [(s * kk_b[jr]).astype(bf16) for (kk_b, _, _, _, _), s in zip(rowvecs, states)]
            sa_all = head_sums(pk)
            vb_all = unpack_pairs(jnp.dot(v_lhs, e_ref[j], preferred_element_type=f32), len(pairs))
            q_lhs = []
            for n, (_, w_b, be_b, kt_b, r_b) in enumerate(rowvecs):
                states[n] = states[n] * w_b[jr] - sa_all[n] * be_b[jr] + vb_all[n] * kt_b[jr]
                q_lhs.append((states[n] * r_b[jr]).astype(bf16))
            res_all = head_sums(q_lhs)
            for n, (bb, pr) in enumerate(pairs):
                ots[pr] = jnp.where(lane == bb * n_st + j, res_all[n], ots[pr])
        for n, (bb, pr) in enumerate(pairs):
            s_ref[u * n_sb + bb if not prompt_mode else bb, pr] = states[n]
        for pr in range(RW_PAIRS):
            ots_s[u, pr * FOX_HD:(pr + 1) * FOX_HD, :] = ots[pr]
        return 0

    lax.fori_loop(0, n_units, unit_body, 0)

    def ot_body(u, _):
        t = jnp.transpose(ots_s[u])
        ta, tb = t[0:UNIT_COLS], t[FOX_HD:FOX_HD + UNIT_COLS]
        o16 = jnp.concatenate(
            [x[:, pr * FOX_HD:(pr + 1) * FOX_HD] for pr in range(RW_PAIRS) for x in (ta, tb)], axis=1)
        if prompt_mode:
            for bb in range(n_sb):
                o_s[pl.ds(pl.multiple_of(bb * tc + u * n_st, 8), n_st), :] = o16[bb * n_st:(bb + 1) * n_st]
        else:
            o_s[pl.ds(pl.multiple_of(u * UNIT_COLS, 8), UNIT_COLS), :] = o16
        return 0

    lax.fori_loop(0, n_units, ot_body, 0)

    o = o_s[...]
    inv = 1.0 / FOX_HD
    mean = _split_dot(o, bd) * inv
    d = o - mean
    var = _split_dot(d * d, bd) * inv
    o_gn = d * lax.rsqrt(var + GN_EPS) * lnw_ref[...] + lnb_ref[...]
    bonus = _split_dot(r_s[...] * kt_s[...] * rk_ref[...], bd) * v_s[...]
    out = (o_gn + bonus) * g_s[...]
    o_ref[...] = out.reshape(o_ref.shape).astype(o_ref.dtype)


def _rwkv_weights(rw_mu, rw_w0, rw_w_up, rw_a0, rw_a_up, rw_g_up, rw_k_k, rw_k_a, rw_r_k, rw_ln_w, rw_ln_b):
    row = lambda a: a.reshape(1, -1)
    wup = jnp.concatenate([rw_w_up, jnp.zeros_like(rw_a_up)], axis=0).astype(bf16)
    aup = jnp.concatenate([jnp.zeros_like(rw_w_up), rw_a_up], axis=0).astype(bf16)
    return (row(rw_mu), row(rw_w0), wup, row(rw_a0), aup, rw_g_up.astype(bf16), row(rw_k_k), row(rw_k_a),
            row(rw_r_k), row(rw_ln_w), row(rw_ln_b), _block_diag_ones(RW_W, FOX_HD))


def _rwkv_scratch(n_rows, n_sb):
    n_units = n_rows // UNIT_COLS
    rows = [pltpu.VMEM((n_rows, RW_W), f32) for _ in range(10)]
    return rows + [pltpu.VMEM((n_units * n_sb, RW_W, LANES), f32),
                   pltpu.VMEM((n_units, RW_PAIRS * FOX_HD, LANES), f32),
                   pltpu.VMEM((LANES, RW_W), f32), pltpu.VMEM((n_rows, RW_W), f32)]


def _rwkv_selectors(n_st):
    ob = _block_diag_ones(LANES, FOX_HD)
    wsa = _block_diag_ones(2 * LANES, FOX_HD)
    shape = (n_st, 2 * LANES, 2 * LANES)
    r = lax.broadcasted_iota(jnp.int32, shape, 1)
    c = lax.broadcasted_iota(jnp.int32, shape, 2)
    j = lax.broadcasted_iota(jnp.int32, shape, 0)
    col = r % LANES
    e = ((r // LANES == c // LANES) & (col < 6 * n_st) & (col % n_st == j)
         & (col // (3 * n_st) == (c % LANES) // FOX_HD)).astype(bf16)
    return wsa, ob, e


def _rwkv_prompt(prw, weights, *, tc):
    nb, ns, _ = prw.shape
    const = lambda a: pl.BlockSpec(a.shape, lambda j: (0,) * a.ndim)
    n_st = UNIT_COLS // nb
    weights = tuple(weights) + _rwkv_selectors(n_st)
    return pl.pallas_call(
        functools.partial(_rwkv_body, n_blk_b=nb, tc=tc, n_sb=nb, n_st=n_st, prompt_mode=True),
        grid=(ns // tc,),
        in_specs=[pl.BlockSpec((nb, tc, RW_IN), lambda j: (0, j, 0))] + [const(w) for w in weights],
        out_specs=(pl.BlockSpec((nb, tc, RW_W), lambda j: (0, j, 0)),
                   pl.BlockSpec((nb, RW_PAIRS, FOX_HD, LANES), lambda j: (0, 0, 0, 0))),
        out_shape=(jax.ShapeDtypeStruct((nb, ns, RW_W), bf16),
                   jax.ShapeDtypeStruct((nb, RW_PAIRS, FOX_HD, LANES), f32)),
        scratch_shapes=[pltpu.VMEM((nb, 8, RW_IN), f32)] + _rwkv_scratch(nb * tc, nb),
        compiler_params=_cparams(("arbitrary",)),
        name="rwkv_prompt",
    )(prw, *weights)


def _rwkv_sample(prw_rows, prev_first_rows, state, weights, *, seq, rows_per_blk):
    n_rows = prw_rows.shape[0]
    nb_blk = rows_per_blk // seq
    const = lambda a: pl.BlockSpec(a.shape, lambda j: (0,) * a.ndim)
    n_sb = UNIT_COLS // seq
    weights = tuple(weights) + _rwkv_selectors(seq)
    return pl.pallas_call(
        functools.partial(_rwkv_body, n_blk_b=nb_blk, tc=seq, n_sb=n_sb, n_st=seq, prompt_mode=False),
        grid=(n_rows // rows_per_blk,),
        in_specs=[pl.BlockSpec((rows_per_blk, RW_IN), lambda j: (j, 0)),
                  pl.BlockSpec((rows_per_blk, RW_IN), lambda j: (j, 0)),
                  pl.BlockSpec((nb_blk, RW_PAIRS, FOX_HD, LANES), lambda j: (j, 0, 0, 0))]
                 + [const(w) for w in weights],
        out_specs=(pl.BlockSpec((rows_per_blk, RW_W), lambda j: (j, 0)),
                   pl.BlockSpec((nb_blk, RW_PAIRS, FOX_HD, LANES), lambda j: (j, 0, 0, 0))),
        out_shape=(jax.ShapeDtypeStruct((n_rows, RW_W), bf16),
                   jax.ShapeDtypeStruct((n_rows // seq, RW_PAIRS, FOX_HD, LANES), f32)),
        scratch_shapes=_rwkv_scratch(rows_per_blk, n_sb),
        compiler_params=_cparams(("arbitrary",)),
        name="rwkv_sample",
    )(prw_rows, prev_first_rows, state, *weights)


def _pack_state(state):
    b = state.shape[0]
    return state.reshape(b, RW_PAIRS, 2, FOX_HD, FOX_HD).transpose(0, 1, 3, 2, 4).reshape(b, RW_PAIRS, FOX_HD, LANES)


def _unpack_state(packed):
    b = packed.shape[0]
    return packed.reshape(b, RW_PAIRS, FOX_HD, 2, FOX_HD).transpose(0, 1, 3, 2, 4).reshape(b, 2 * RW_PAIRS, FOX_HD, FOX_HD)


def _fox_sample_body(pt_ref, *refs, n_pg, n_q):
    del pt_ref
    q_ref, kn_ref, vn_ref, lfn_ref, tri_ref, hm_ref = refs[:6]
    k_refs = refs[6:6 + n_pg]
    v_refs = refs[6 + n_pg:6 + 2 * n_pg]
    lf_refs = refs[6 + 2 * n_pg:6 + 3 * n_pg]
    o_ref, m_s, l_s, acc_s, carry_s = refs[6 + 3 * n_pg:]
    jg = pl.program_id(1)
    n_rows = n_q * FOX_HEADS
    page = tri_ref.shape[0]

    @pl.when(jg == 0)
    def _():
        m_s[...] = jnp.full_like(m_s, NEG_INF)
        l_s[...] = jnp.zeros_like(l_s)
        acc_s[...] = jnp.zeros_like(acc_s)
        carry_s[...] = jnp.zeros_like(carry_s)

    hm = hm_ref[...]
    q4 = q_ref[0]
    qm = jnp.concatenate([jnp.broadcast_to(q4[i:i + 1], (FOX_HEADS, FOX_W)) for i in range(n_q)], axis=0) * hm
    tri = tri_ref[...]

    def cum_bias(lf):
        ct = _split_dot(lf, tri) + carry_s[...]
        carry_s[...] = jnp.broadcast_to(ct[:, page - 1:page], carry_s.shape)
        return jnp.concatenate([ct] * n_q, axis=0)

    def update(s_all, pv):
        m_old = m_s[...]
        m_new = jnp.maximum(m_old, jnp.max(s_all, axis=1, keepdims=True))
        alpha = jnp.exp(m_old - m_new)
        p = jnp.exp(s_all - m_new)
        l_s[...] = l_s[...] * alpha + jnp.sum(p, axis=1, keepdims=True)
        acc_s[...] = acc_s[...] * alpha + pv(p.astype(bf16))
        m_s[...] = m_new

    def pv_pages(p):
        acc = jnp.zeros((n_rows, FOX_W), f32)
        for g in range(n_pg):
            acc = acc + lax.dot_general(p[:, g * page:(g + 1) * page], v_refs[g][0].astype(bf16),
                                        (((1,), (1,)), ((), ())), preferred_element_type=f32)
        return acc

    s_list = []
    for g in range(n_pg):
        s = jnp.dot(qm, k_refs[g][0].astype(bf16), preferred_element_type=f32)
        s_list.append(s - cum_bias(lf_refs[g][0]))
    update(jnp.concatenate(s_list, axis=1), pv_pages)

    @pl.when(jg == pl.num_programs(1) - 1)
    def _():
        pad = jnp.zeros((page - n_q, FOX_W), bf16)
        kn = jnp.concatenate([kn_ref[0], pad], axis=0)
        vn = jnp.concatenate([vn_ref[0], pad], axis=0)
        s = lax.dot_general(qm, kn, (((1,), (1,)), ((), ())), preferred_element_type=f32)
        bias = cum_bias(lfn_ref[0])
        key = lax.broadcasted_iota(jnp.int32, (n_rows, page), 1)
        qpos = lax.broadcasted_iota(jnp.int32, (n_rows, page), 0) // FOX_HEADS
        update(jnp.where(key <= qpos, s - bias, NEG_INF),
               lambda p: jnp.dot(p, vn, preferred_element_type=f32))
        o = acc_s[...] / l_s[...] * hm.astype(f32)
        o_ref[0] = jnp.concatenate(
            [jnp.sum(o[i * FOX_HEADS:(i + 1) * FOX_HEADS], axis=0, keepdims=True) for i in range(n_q)],
            axis=0).astype(o_ref.dtype)


def _fox_sample(page_table, q, kn, vn, lfn_t, cache_kt, cache_vt, cache_lft, *, n_pg):
    nb, n_q, _ = q.shape
    n_pages = page_table.shape[1]
    page = cache_kt.shape[2]
    n_rows = n_q * FOX_HEADS
    tri = (lax.broadcasted_iota(jnp.int32, (page, page), 0)
           <= lax.broadcasted_iota(jnp.int32, (page, page), 1)).astype(bf16)
    hm = (lax.broadcasted_iota(jnp.int32, (n_rows, FOX_W), 0) % FOX_HEADS
          == lax.broadcasted_iota(jnp.int32, (n_rows, FOX_W), 1) // FOX_HD).astype(bf16)
    per_b = lambda shape: pl.BlockSpec((1,) + shape, lambda b, j, pt: (b, 0, 0))
    const = lambda a: pl.BlockSpec(a.shape, lambda b, j, pt: (0, 0))
    paged = lambda shape, g: pl.BlockSpec((1,) + shape, lambda b, j, pt: (pt[b, j * n_pg + g], 0, 0))
    in_specs = ([per_b((n_q, FOX_W))] * 3 + [per_b((FOX_HEADS, page)), const(tri), const(hm)]
                + [paged((FOX_W, page), g) for g in range(n_pg)]
                + [paged((FOX_W, page), g) for g in range(n_pg)]
                + [paged((FOX_HEADS, page), g) for g in range(n_pg)])
    grid_spec = pltpu.PrefetchScalarGridSpec(
        num_scalar_prefetch=1, grid=(nb, n_pages // n_pg), in_specs=in_specs,
        out_specs=pl.BlockSpec((1, n_q, FOX_W), lambda b, j, pt: (b, 0, 0)),
        scratch_shapes=[pltpu.VMEM((n_rows, 1), f32), pltpu.VMEM((n_rows, 1), f32),
                        pltpu.VMEM((n_rows, FOX_W), f32), pltpu.VMEM((FOX_HEADS, page), f32)])
    return pl.pallas_call(
        functools.partial(_fox_sample_body, n_pg=n_pg, n_q=n_q),
        grid_spec=grid_spec,
        out_shape=jax.ShapeDtypeStruct((nb, n_q, FOX_W), bf16),
        compiler_params=_cparams(("arbitrary", "arbitrary")),
        name="fox_sample",
    )(page_table, q, kn, vn, lfn_t, tri, hm, *([cache_kt] * n_pg), *([cache_vt] * n_pg), *([cache_lft] * n_pg))


def _mem_kv_body(m_ref, g_ref, w_ref, knw_ref, k_ref, v_ref):
    n = _rms(m_ref[0], g_ref[...]).astype(bf16)
    kv = jnp.dot(n, w_ref[...], preferred_element_type=f32)
    for h in range(MEM_HEADS):
        sl = slice(h * MEM_HD, (h + 1) * MEM_HD)
        k_ref[0, :, sl] = _rms(kv[:, sl], knw_ref[...])
    v_ref[0] = kv[:, MEM_W:]


def _mem_kv(mem, mem_norm_w, w_mem_kv, mem_kn_w):
    nb, nm, _ = mem.shape
    const = lambda shape: pl.BlockSpec(shape, lambda b: (0,) * len(shape))
    out = jax.ShapeDtypeStruct((nb, nm, MEM_W), f32)
    return pl.pallas_call(
        _mem_kv_body,
        grid=(nb,),
        in_specs=[pl.BlockSpec((1, nm, D_MODEL), lambda b: (b, 0, 0)), const((1, D_MODEL)),
                  const((D_MODEL, 2 * MEM_W)), const((1, MEM_HD))],
        out_specs=(pl.BlockSpec((1, nm, MEM_W), lambda b: (b, 0, 0)),) * 2,
        out_shape=(out, out),
        compiler_params=_cparams(("arbitrary",)),
        name="mem_kv",
    )(mem, mem_norm_w.reshape(1, D_MODEL), w_mem_kv.astype(bf16), mem_kn_w.reshape(1, MEM_HD))


def _mem_attend_body(q_ref, k_ref, v_ref, o_ref):
    q = q_ref[0]
    for h in range(MEM_HEADS):
        sl = slice(h * MEM_HD, (h + 1) * MEM_HD)
        k = k_ref[0, :, sl].astype(bf16)
        v = v_ref[0, :, sl].astype(bf16)
        s = lax.dot_general(q[:, sl], k, (((1,), (1,)), ((), ())), preferred_element_type=f32) * MEM_SCALE
        e = jnp.exp(s - jnp.max(s, axis=1, keepdims=True))
        p = e / jnp.sum(e, axis=1, keepdims=True)
        o_ref[0, :, sl] = jnp.dot(p.astype(bf16), v, preferred_element_type=f32).astype(o_ref.dtype)


def _mem_attend_rows_body(q_ref, k_ref, v_ref, o_ref):
    n_mem = k_ref.shape[1] // MEM_HEADS
    for i in range(q_ref.shape[0]):
        q = q_ref[i]
        for h in range(MEM_HEADS):
            sl = slice(h * MEM_HD, (h + 1) * MEM_HD)
            k = k_ref[i, pl.ds(h, n_mem, stride=MEM_HEADS), :].astype(bf16)
            v = v_ref[i, pl.ds(h, n_mem, stride=MEM_HEADS), :].astype(bf16)
            s = lax.dot_general(q[:, sl], k, (((1,), (1,)), ((), ())), preferred_element_type=f32) * MEM_SCALE
            e = jnp.exp(s - jnp.max(s, axis=1, keepdims=True))
            p = e / jnp.sum(e, axis=1, keepdims=True)
            o_ref[i, :, sl] = jnp.dot(p.astype(bf16), v, preferred_element_type=f32).astype(o_ref.dtype)


def _mem_attend_rows(q, mk_rows, mv_rows, *, nb_blk):
    nb, ns, _ = q.shape
    rows = mk_rows.shape[1]
    kv = pl.BlockSpec((nb_blk, rows, MEM_HD), lambda b: (b, 0, 0))
    qo = pl.BlockSpec((nb_blk, ns, MEM_W), lambda b: (b, 0, 0))
    return pl.pallas_call(
        _mem_attend_rows_body,
        grid=(nb // nb_blk,),
        in_specs=[qo, kv, kv],
        out_specs=qo,
        out_shape=jax.ShapeDtypeStruct((nb, ns, MEM_W), bf16),
        compiler_params=_cparams(("arbitrary",)),
        name="mem_attend_rows",
    )(q, mk_rows, mv_rows)


def _mem_attend(q, mk, mv, *, tq):
    nb, ns, _ = q.shape
    nm = mk.shape[1]
    kv = pl.BlockSpec((1, nm, MEM_W), lambda b, j: (b, 0, 0))
    return pl.pallas_call(
        _mem_attend_body,
        grid=(nb, ns // tq),
        in_specs=[pl.BlockSpec((1, tq, MEM_W), lambda b, j: (b, j, 0)), kv, kv],
        out_specs=pl.BlockSpec((1, tq, MEM_W), lambda b, j: (b, j, 0)),
        out_shape=jax.ShapeDtypeStruct((nb, ns, MEM_W), bf16),
        compiler_params=_cparams(("arbitrary", "arbitrary")),
        name="mem_attend",
    )(q, mk, mv)


def _finish_body(x_ref, of_ref, or_ref, om_ref, g1_ref, wg_ref, wb_ref, wo_ref, g2_ref, h_ref, xn2_ref):
    x = x_ref[...]
    xn = _rms(x, g1_ref[...]).astype(bf16)
    merged = jnp.zeros_like(x)
    for n, br in enumerate((of_ref, or_ref, om_ref)):
        gate = _sigmoid(jnp.dot(xn, wg_ref[:, n * D_MODEL:(n + 1) * D_MODEL], preferred_element_type=f32))
        merged = merged + gate * jnp.dot(br[...], wb_ref[n], preferred_element_type=f32)
    h = x + jnp.dot(merged.astype(bf16), wo_ref[...], preferred_element_type=f32)
    h_ref[...] = h
    xn2_ref[...] = _rms(h, g2_ref[...]).astype(bf16)


def _finish(x, o_fox, o_rw, o_mem, norm1_w, w_gate, w_branch, w_out, norm2_w, *, tile):
    n = x.shape[0]
    const = lambda shape: pl.BlockSpec(shape, lambda j: (0,) * len(shape))
    tok = lambda w: pl.BlockSpec((tile, w), lambda j: (j, 0))
    return pl.pallas_call(
        _finish_body,
        grid=(n // tile,),
        in_specs=[tok(D_MODEL), tok(FOX_W), tok(RW_W), tok(MEM_W), const((1, D_MODEL)),
                  const((D_MODEL, 3 * D_MODEL)), const((3, FOX_W, D_MODEL)), const((D_MODEL, D_MODEL)),
                  const((1, D_MODEL))],
        out_specs=(tok(D_MODEL), tok(D_MODEL)),
        out_shape=(jax.ShapeDtypeStruct((n, D_MODEL), f32), jax.ShapeDtypeStruct((n, D_MODEL), bf16)),
        compiler_params=_cparams(("arbitrary",)),
        name="finish",
    )(x, o_fox, o_rw, o_mem, norm1_w.reshape(1, D_MODEL), w_gate, w_branch.astype(bf16), w_out.astype(bf16),
      norm2_w.reshape(1, D_MODEL))


_PEER_SLOTS = [(ra, rb) for ra in range(PEER_TOPK) for rb in range(PEER_TOPK) if (ra + 1) * (rb + 1) <= PEER_TOPK]
_PEER_SLOT_ROWS = -(-len(_PEER_SLOTS) // 8) * 8


def _top_rows(s, n_take):
    iota = lax.broadcasted_iota(jnp.int32, s.shape, 0).astype(f32)
    big = float(s.shape[0])
    vals, idxs = [], []
    for _ in range(n_take):
        m = jnp.max(s, axis=0, keepdims=True)
        idx = jnp.min(jnp.where(s == m, iota, big), axis=0, keepdims=True)
        vals.append(m)
        idxs.append(idx)
        s = jnp.where(iota == idx, -jnp.inf, s)
    return vals, idxs


def _route_body(x_ref, wq_ref, k1_ref, k2_ref, a_ref, b_ref, g_ref,
                q_s, cand_s, ea_s, eb_s, a_s, b_s, g_s, at_s, bt_s, gt_s):
    q_s[...] = lax.dot_general(wq_ref[...], x_ref[...], (((1,), (1,)), ((), ())),
                               preferred_element_type=f32).astype(bf16)
    cand_s[...] = jnp.full_like(cand_s, -jnp.inf)
    ea_s[...] = jnp.zeros_like(ea_s)
    eb_s[...] = jnp.zeros_like(eb_s)

    def head(h, _):
        base = pl.multiple_of(h * 2 * PEER_HALF, 2 * PEER_HALF)
        s1 = jnp.dot(k1_ref[h], q_s[pl.ds(base, PEER_HALF), :], preferred_element_type=f32)
        s2 = jnp.dot(k2_ref[h], q_s[pl.ds(base + PEER_HALF, PEER_HALF), :], preferred_element_type=f32)
        v1, i1 = _top_rows(s1, PEER_TOPK)
        v2, i2 = _top_rows(s2, PEER_TOPK)
        for slot, (ra, rb) in enumerate(_PEER_SLOTS):
            cand_s[slot:slot + 1, :] = v1[ra] + v2[rb]
            ea_s[slot:slot + 1, :] = i1[ra]
            eb_s[slot:slot + 1, :] = i2[rb]
        cand = cand_s[...]
        ea = ea_s[...]
        eb = eb_s[...]
        iota = lax.broadcasted_iota(jnp.int32, cand.shape, 0).astype(f32)
        scs = []
        for r in range(PEER_TOPK):
            m = jnp.max(cand, axis=0, keepdims=True)
            slot = jnp.min(jnp.where(cand == m, iota, float(_PEER_SLOT_ROWS)), axis=0, keepdims=True)
            hit = iota == slot
            a_s[r:r + 1, :] = jnp.sum(jnp.where(hit, ea, 0.0), axis=0, keepdims=True)
            b_s[r:r + 1, :] = jnp.sum(jnp.where(hit, eb, 0.0), axis=0, keepdims=True)
            cand = jnp.where(hit, -jnp.inf, cand)
            scs.append(m)
        es = [jnp.exp(sc - scs[0]) for sc in scs]
        z = es[0]
        for e in es[1:]:
            z = z + e
        for r in range(PEER_TOPK):
            g_s[r:r + 1, :] = es[r] / z
        rows = pl.ds(pl.multiple_of(h * PEER_TOPK, PEER_TOPK), PEER_TOPK)
        at_s[rows, :] = a_s[...]
        bt_s[rows, :] = b_s[...]
        gt_s[rows, :] = g_s[...]
        return 0

    lax.fori_loop(0, PEER_HEADS, head, 0)
    a_ref[...] = jnp.transpose(at_s[...])
    b_ref[...] = jnp.transpose(bt_s[...])
    g_ref[...] = jnp.transpose(gt_s[...])


def _route(xn2, wq_t, keys1, keys2, *, tile):
    n = xn2.shape[0]
    n_slots = PEER_HEADS * PEER_TOPK
    const = lambda shape: pl.BlockSpec(shape, lambda j: (0,) * len(shape))
    out = jax.ShapeDtypeStruct((n, n_slots), f32)
    ospec = pl.BlockSpec((tile, n_slots), lambda j: (j, 0))
    cand = lambda: pltpu.VMEM((_PEER_SLOT_ROWS, tile), f32)
    top = lambda: pltpu.VMEM((PEER_TOPK, tile), f32)
    full = lambda: pltpu.VMEM((n_slots, tile), f32)
    return pl.pallas_call(
        _route_body,
        grid=(n // tile,),
        in_specs=[pl.BlockSpec((tile, D_MODEL), lambda j: (j, 0)), const(wq_t.shape), const(keys1.shape),
                  const(keys2.shape)],
        out_specs=(ospec, ospec, ospec),
        out_shape=(out, out, out),
        scratch_shapes=[pltpu.VMEM((wq_t.shape[0], tile), bf16), cand(), cand(), cand(), top(), top(), top(),
                        full(), full(), full()],
        compiler_params=_cparams(("arbitrary",)),
        name="peer_route",
    )(xn2, wq_t, keys1, keys2)


G_PITCH = PEER_NK + 8
EXPERT_SUB = 512


def _gelu_tanh(x):
    return 0.5 * x * (1.0 + jnp.tanh(math.sqrt(2.0 / math.pi) * (x + 0.044715 * (x * x * x))))


def _peer_body(x_ref, a_ref, b_ref, g_ref, h_ref, u_ref, v_ref, y_ref, gs_ref, acc_ref):
    tile = x_ref.shape[0]
    c = pl.program_id(1)
    n_exp = u_ref.shape[0]

    @pl.when(c == 0)
    def _():
        acc_ref[...] = jnp.zeros_like(acc_ref)
        sub = lax.broadcasted_iota(jnp.int32, (PEER_NK, LANES), 0).astype(f32)

        def tok8(t8, _):
            rows = pl.ds(pl.multiple_of(t8 * 8, 8), 8)
            a8, b8, g8 = a_ref[rows, :], b_ref[rows, :], g_ref[rows, :]
            for j in range(8):
                hit_a = sub == a8[j:j + 1]
                gate = jnp.where(hit_a, g8[j:j + 1], 0.0)
                g_hi = gate.astype(bf16)
                g_lo = (gate - g_hi.astype(f32)).astype(bf16)
                bt = jnp.where(sub == b8[j:j + 1], 1.0, 0.0).astype(bf16)
                gt = lax.dot_general(jnp.concatenate([g_hi, g_lo], axis=1), jnp.concatenate([bt, bt], axis=1),
                                     (((1,), (1,)), ((), ())), preferred_element_type=f32)
                gs_ref[pl.ds(pl.multiple_of((t8 * 8 + j) * G_PITCH, 8), PEER_NK), :] = gt
            return 0

        lax.fori_loop(0, tile // 8, tok8, 0)

    x = x_ref[...]
    acc = acc_ref[...]
    for s in range(n_exp // EXPERT_SUB):
        e0 = s * EXPERT_SUB
        act = lax.dot_general(x, u_ref[e0:e0 + EXPERT_SUB, :], (((1,), (1,)), ((), ())),
                              preferred_element_type=f32)
        i1 = (c * n_exp + e0) // PEER_NK
        gates = jnp.concatenate(
            [gs_ref[pl.ds(i1 + k, tile, stride=G_PITCH), :] for k in range(EXPERT_SUB // PEER_NK)], axis=1)
        w = (gates * _gelu_tanh(act)).astype(bf16)
        acc = acc + jnp.dot(w, v_ref[e0:e0 + EXPERT_SUB, :], preferred_element_type=f32)
    acc_ref[...] = acc

    @pl.when(c == pl.num_programs(1) - 1)
    def _():
        y_ref[...] = h_ref[...] + acc


def _peer(xn2, a, b, g, h, u_bf, v_bf, *, tile, chunk):
    n = xn2.shape[0]
    n_slots = a.shape[1]
    n_experts = u_bf.shape[0]
    tok = lambda w: pl.BlockSpec((tile, w), lambda j, c: (j, 0))
    tab = pl.BlockSpec((chunk, D_MODEL), lambda j, c: (c, 0))
    return pl.pallas_call(
        _peer_body,
        grid=(n // tile, n_experts // chunk),
        in_specs=[tok(D_MODEL), tok(n_slots), tok(n_slots), tok(n_slots), tok(D_MODEL), tab, tab],
        out_specs=tok(D_MODEL),
        out_shape=jax.ShapeDtypeStruct((n, D_MODEL), f32),
        scratch_shapes=[pltpu.VMEM((tile * G_PITCH, PEER_NK), f32), pltpu.VMEM((tile, D_MODEL), f32)],
        compiler_params=_cparams(("arbitrary", "arbitrary")),
        name="peer_experts",
    )(xn2, a, b, g, h, u_bf, v_bf)


TOK_TILE = 256
ATT_BLK = 1024
RW_CHUNK = 128
RW_SAMPLE_ROWS = 128
PAGES_PER_STEP = 32
EXPERT_CHUNK = 2048
MEM_SAMPLE_BATCHES = 4


def _merge_and_peer(x2, o_fox, o_rw, o_mem, shared):
    h, xn2 = _finish(x2, o_fox, o_rw, o_mem, shared["norm1_w"], shared["w_gate"], shared["w_branch"],
                     shared["w_out"], shared["norm2_w"], tile=TOK_TILE)
    a, b, g = _route(xn2, shared["wq_t"], shared["keys1"], shared["keys2"], tile=TOK_TILE)
    return _peer(xn2, a, b, g, h, shared["u_bf"], shared["v_bf"], tile=TOK_TILE, chunk=EXPERT_CHUNK)


def kernel(x_prompt, x_sample, cache_fox_k, cache_fox_v, cache_fox_logf, cache_mem_k, cache_mem_v, state_rwkv, state_rwkv_shift, page_table, mem_prompt, norm1_w, w_in, fox_b_f, fox_qn_w, fox_kn_w, rw_mu, rw_w0, rw_w_up, rw_a0, rw_a_up, rw_g_up, rw_k_k, rw_k_a, rw_r_k, rw_ln_w, rw_ln_b, mem_norm_w, w_mem_kv, mem_qn_w, mem_kn_w, w_branch, w_out, norm2_w, peer_w_q, peer_keys1, peer_keys2, peer_u, peer_v):
    bp, sp, _ = x_prompt.shape
    db, ds, _ = x_sample.shape
    w_pack, bf_pad, qnw, knw, mqw, w_gate = _prep_inproj_weights(w_in, fox_b_f, fox_qn_w, fox_kn_w, mem_qn_w)
    g1 = norm1_w.reshape(1, D_MODEL)
    rw_w = _rwkv_weights(rw_mu, rw_w0, rw_w_up, rw_a0, rw_a_up, rw_g_up, rw_k_k, rw_k_a, rw_r_k, rw_ln_w, rw_ln_b)
    shared = dict(norm1_w=norm1_w, w_gate=w_gate, w_branch=w_branch, w_out=w_out, norm2_w=norm2_w,
                  wq_t=peer_w_q.T.astype(bf16), keys1=peer_keys1.astype(bf16), keys2=peer_keys2.astype(bf16),
                  u_bf=peer_u.astype(bf16), v_bf=peer_v.astype(bf16))

    qh, kh, vh, k_p, v_p, logf_p, prw_p, mq_p = _inproj(
        x_prompt, g1, w_pack, bf_pad, qnw, knw, mqw, head_major=True, tile=TOK_TILE)
    o_fox = _fox_prompt(qh, kh, vh, blk=ATT_BLK)
    o_rw, st_p = _rwkv_prompt(prw_p, rw_w, tc=RW_CHUNK)
    mem_k_p, mem_v_p = _mem_kv(mem_prompt, mem_norm_w, w_mem_kv, mem_kn_w)
    o_mem = _mem_attend(mq_p, mem_k_p, mem_v_p, tq=ATT_BLK)
    n_p = bp * sp
    y_prompt = _merge_and_peer(x_prompt.reshape(n_p, D_MODEL), o_fox.reshape(n_p, FOX_W), o_rw.reshape(n_p, RW_W),
                               o_mem.reshape(n_p, MEM_W), shared).reshape(bp, sp, D_MODEL)

    n_s = db * ds
    qt, kt, vt, k_s, v_s, logf_s, prw_s, mq_s = _inproj(
        x_sample.reshape(1, n_s, D_MODEL), g1, w_pack, bf_pad, qnw, knw, mqw, head_major=False, tile=TOK_TILE)
    pool, page = cache_fox_k.shape[:2]
    logf_s = logf_s.reshape(db, ds, LANES)[:, :, :FOX_HEADS]
    lfn_t = jnp.pad(jnp.swapaxes(logf_s, 1, 2), ((0, 0), (0, 0), (0, page - ds)))
    pos_minor = lambda a: jnp.transpose(a, (0, 2, 3, 1)).reshape(pool, FOX_W, page)
    o_fox_s = _fox_sample(page_table, qt.reshape(db, ds, FOX_W), kt.reshape(db, ds, FOX_W), vt.reshape(db, ds, FOX_W),
                          lfn_t, pos_minor(cache_fox_k), pos_minor(cache_fox_v), jnp.swapaxes(cache_fox_logf, 1, 2),
                          n_pg=PAGES_PER_STEP)
    prw_rows = prw_s.reshape(n_s, RW_IN)
    o_rw_s, st_s = _rwkv_sample(prw_rows, jnp.repeat(state_rwkv_shift, ds, axis=0), _pack_state(state_rwkv), rw_w,
                                seq=ds, rows_per_blk=RW_SAMPLE_ROWS)
    nm = cache_mem_k.shape[1]
    o_mem_s = _mem_attend_rows(mq_s.reshape(db, ds, MEM_W), cache_mem_k.reshape(db, nm * MEM_HEADS, MEM_HD),
                               cache_mem_v.reshape(db, nm * MEM_HEADS, MEM_HD), nb_blk=MEM_SAMPLE_BATCHES)
    y_sample = _merge_and_peer(x_sample.reshape(n_s, D_MODEL), o_fox_s.reshape(n_s, FOX_W), o_rw_s,
                               o_mem_s.reshape(n_s, MEM_W), shared).reshape(db, ds, D_MODEL)

    heads = lambda a, b, s: a.reshape(b, s, FOX_HEADS, FOX_HD)
    return (y_prompt, y_sample,
            heads(k_p, bp, sp), heads(v_p, bp, sp), logf_p[:, :, :FOX_HEADS],
            _unpack_state(st_p), prw_p[:, -1],
            mem_k_p.reshape(bp, nm, MEM_HEADS, MEM_HD), mem_v_p.reshape(bp, nm, MEM_HEADS, MEM_HD),
            heads(k_s, db, ds), heads(v_s, db, ds), logf_s,
            _unpack_state(st_s), prw_rows.reshape(db, ds, RW_IN)[:, -1])
```

```python
import functools
import math

import jax
import jax.numpy as jnp
from jax import lax
from jax.experimental import pallas as pl
from jax.experimental.pallas import tpu as pltpu

f32 = jnp.float32
bf16 = jnp.bfloat16

D_MODEL = 1024
RMS_EPS = 1e-6
NEG_INF = -1e30
FOX_HEADS = 8
FOX_HD = 64
FOX_W = 512
FOX_SCALE = FOX_HD ** -0.5
LOG2E = math.log2(math.e)
RW_W = 512
RW_IN = 1792
RW_DECAY = math.exp(-0.5)
GN_EPS = 64e-5
MEM_HEADS = 4
MEM_HD = 128
MEM_W = 512
MEM_SCALE = MEM_HD ** -0.5
PEER_HEADS = 8
PEER_NK = 128
PEER_HALF = 128
PEER_TOPK = 16

LANES = 128
VMEM_LIMIT = 56 * 1024 * 1024

_PK_Q, _PK_K, _PK_V, _PK_RW, _PK_MQ, _PK_F, _PK_END = 0, 512, 1024, 1536, 3328, 3840, 3968


def _cparams(sem):
    return pltpu.CompilerParams(dimension_semantics=sem, vmem_limit_bytes=VMEM_LIMIT)


def _block_diag_ones(width, group):
    r = lax.broadcasted_iota(jnp.int32, (width, width), 0) // group
    c = lax.broadcasted_iota(jnp.int32, (width, width), 1) // group
    return (r == c).astype(bf16)


def _split_dot(x, w):
    hi = x.astype(bf16)
    lo = (x - hi.astype(f32)).astype(bf16)
    return (jnp.dot(hi, w, preferred_element_type=f32)
            + jnp.dot(lo, w, preferred_element_type=f32))


def _rms(x, g):
    return x * lax.rsqrt(jnp.mean(x * x, axis=-1, keepdims=True) + RMS_EPS) * g


def _log_sigmoid(x):
    return jnp.minimum(x, 0.0) - jnp.log1p(jnp.exp(-jnp.abs(x)))


def _sigmoid(x):
    return 1.0 / (1.0 + jnp.exp(-x))


def _inproj_body(x_ref, g_ref, w_ref, bf_ref, qnw_ref, knw_ref, mqw_ref, bd_ref, tri_ref, sc_ref, ones_ref,
                 q_ref, kb_ref, vb_ref, kp_ref, vp_ref, logf_ref, prw_ref, mq_ref,
                 carry_ref, *, head_major):
    @pl.when(pl.program_id(1) == 0)
    def _():
        carry_ref[...] = jnp.zeros_like(carry_ref)

    x = x_ref[0]
    xn = _rms(x, g_ref[...]).astype(bf16)
    p = jnp.dot(xn, w_ref[...], preferred_element_type=f32)
    q = p[:, _PK_Q:_PK_K]
    k = p[:, _PK_K:_PK_V]
    v = p[:, _PK_V:_PK_RW]
    prw_ref[0] = p[:, _PK_RW:_PK_MQ]
    bd = bd_ref[...]
    qn = q * lax.rsqrt(_split_dot(q * q, bd) * (1.0 / FOX_HD) + RMS_EPS) * qnw_ref[...]
    kn = k * lax.rsqrt(_split_dot(k * k, bd) * (1.0 / FOX_HD) + RMS_EPS) * knw_ref[...]
    kp_ref[0] = kn
    vp_ref[0] = v
    kb = kn.astype(bf16)
    vb = v.astype(bf16)
    lane = lax.broadcasted_iota(jnp.int32, (x.shape[0], LANES), 1)
    logf = jnp.where(lane < FOX_HEADS, _log_sigmoid(p[:, _PK_F:_PK_END] + bf_ref[...]), 0.0)
    logf_ref[0] = logf
    for h in range(MEM_HEADS):
        sl = slice(_PK_MQ + h * MEM_HD, _PK_MQ + (h + 1) * MEM_HD)
        mq_ref[0, :, h * MEM_HD:(h + 1) * MEM_HD] = _rms(p[:, sl], mqw_ref[...]).astype(bf16)
    if not head_major:
        q_ref[0] = (qn * FOX_SCALE).astype(bf16)
        kb_ref[0] = kb
        vb_ref[0] = vb
        return
    c = _split_dot_left(tri_ref[...], logf) + carry_ref[0:1, :]
    carry_ref[0:1, :] = c[x.shape[0] - 1:x.shape[0], :]
    c2 = c * LOG2E
    c_hi = c2.astype(bf16)
    rest = c2 - c_hi.astype(f32)
    c_mid = rest.astype(bf16)
    parts = (c_hi, c_mid, (rest - c_mid.astype(f32)).astype(bf16))
    extra_q = ones_ref[0:1, :]
    extra_k = ones_ref[1:2, :]
    for i, part in enumerate(parts):
        extra_q = extra_q + jnp.dot(part, sc_ref[i], preferred_element_type=f32)
        extra_k = extra_k + jnp.dot(part, sc_ref[len(parts) + i], preferred_element_type=f32)
    qs = (qn * (FOX_SCALE * LOG2E)).astype(bf16)
    for h in range(FOX_HEADS):
        pair = slice((h // 2) * LANES, (h // 2 + 1) * LANES)
        own = slice(h * LANES, (h + 1) * LANES)
        in_own_half = lane // FOX_HD == h % 2
        q_ref[0, h] = jnp.where(in_own_half, qs[:, pair], extra_q[:, own].astype(bf16))
        kb_ref[0, h] = jnp.where(in_own_half, kb[:, pair], extra_k[:, own].astype(bf16))
        vb_ref[0, h] = vb[:, h * FOX_HD:(h + 1) * FOX_HD]


def _split_dot_left(w, x):
    hi = x.astype(bf16)
    lo = (x - hi.astype(f32)).astype(bf16)
    return (jnp.dot(w, hi, preferred_element_type=f32)
            + jnp.dot(w, lo, preferred_element_type=f32))


def _inproj(x3, norm1_w, w_pack, bf_pad, qnw, knw, mqw, *, head_major, tile):
    nb, ns, _ = x3.shape
    nt = ns // tile
    bd = _block_diag_ones(FOX_W, FOX_HD)
    tri = (lax.broadcasted_iota(jnp.int32, (tile, tile), 0)
           >= lax.broadcasted_iota(jnp.int32, (tile, tile), 1)).astype(bf16)
    const = lambda shape: pl.BlockSpec(shape, lambda b, j: (0,) * len(shape))
    tok = lambda w: pl.BlockSpec((1, tile, w), lambda b, j: (b, j, 0))
    if head_major:
        hm = lambda w: (jax.ShapeDtypeStruct((nb, FOX_HEADS, ns, w), bf16),
                        pl.BlockSpec((1, FOX_HEADS, tile, w), lambda b, j: (b, 0, j, 0)))
        qkv = (hm(LANES), hm(LANES), hm(FOX_HD))
    else:
        qkv = ((jax.ShapeDtypeStruct((nb, ns, FOX_W), bf16), tok(FOX_W)),) * 3
    n_parts = 3
    lanes = lax.broadcasted_iota(jnp.int32, (2 * n_parts, LANES, FOX_HEADS * LANES), 2)
    rows = lax.broadcasted_iota(jnp.int32, (2 * n_parts, LANES, FOX_HEADS * LANES), 1)
    slot = lax.broadcasted_iota(jnp.int32, (2 * n_parts, LANES, FOX_HEADS * LANES), 0)
    head = lanes // LANES
    off = lanes % LANES - (1 - head % 2) * FOX_HD
    sc = jnp.where((rows == head) & (off == slot), jnp.where(slot < n_parts, 1.0, -1.0), 0.0).astype(bf16)
    off1 = off[0, 0:1, :]
    ones = jnp.concatenate([(off1 >= n_parts) & (off1 < 2 * n_parts), (off1 >= 0) & (off1 < n_parts)],
                           axis=0).astype(f32)
    out_shape = tuple(s for s, _ in qkv) + (
        jax.ShapeDtypeStruct((nb, ns, FOX_W), f32),
        jax.ShapeDtypeStruct((nb, ns, FOX_W), f32),
        jax.ShapeDtypeStruct((nb, ns, LANES), f32),
        jax.ShapeDtypeStruct((nb, ns, RW_IN), f32),
        jax.ShapeDtypeStruct((nb, ns, MEM_W), bf16),
    )
    out_specs = tuple(s for _, s in qkv) + (tok(FOX_W), tok(FOX_W), tok(LANES), tok(RW_IN), tok(MEM_W))
    return pl.pallas_call(
        functools.partial(_inproj_body, head_major=head_major),
        grid=(nb, nt),
        in_specs=[tok(D_MODEL), const((1, D_MODEL)), const((D_MODEL, _PK_END)), const((1, LANES)),
                  const((1, FOX_W)), const((1, FOX_W)), const((1, MEM_HD)), const((FOX_W, FOX_W)),
                  const((tile, tile)), const(sc.shape), const(ones.shape)],
        out_specs=out_specs,
        out_shape=out_shape,
        scratch_shapes=[pltpu.VMEM((8, LANES), f32)],
        compiler_params=_cparams(("arbitrary", "arbitrary")),
        name="inproj",
    )(x3, norm1_w, w_pack, bf_pad, qnw, knw, mqw, bd, tri, sc, ones)


def _prep_inproj_weights(w_in, fox_b_f, fox_qn_w, fox_kn_w, mem_qn_w):
    fox_in = 3 * FOX_W + FOX_HEADS
    w_fox = w_in[:, :fox_in]
    w_rw = w_in[:, fox_in:fox_in + RW_IN]
    w_mq = w_in[:, fox_in + RW_IN:fox_in + RW_IN + MEM_W]
    w_gate = w_in[:, fox_in + RW_IN + MEM_W:]
    w_f = jnp.pad(w_fox[:, 3 * FOX_W:], ((0, 0), (0, LANES - FOX_HEADS)))
    w_pack = jnp.concatenate([w_fox[:, :3 * FOX_W], w_rw, w_mq, w_f], axis=1).astype(bf16)
    bf_pad = jnp.pad(fox_b_f, (0, LANES - FOX_HEADS)).reshape(1, LANES)
    qnw = jnp.tile(fox_qn_w, FOX_HEADS).reshape(1, FOX_W)
    knw = jnp.tile(fox_kn_w, FOX_HEADS).reshape(1, FOX_W)
    return w_pack, bf_pad, qnw, knw, mem_qn_w.reshape(1, MEM_HD), w_gate.astype(bf16)


def _fox_prompt_body(q_ref, k_ref, v_ref, o_ref, *, blk):
    qi = pl.program_id(2)
    row = lax.broadcasted_iota(jnp.int32, (blk, blk), 0)
    col = lax.broadcasted_iota(jnp.int32, (blk, blk), 1)
    qs = [q_ref[0, hh] for hh in range(2)]

    def step(kj, carry, masked):
        start = pl.multiple_of(kj * blk, blk)
        new = []
        for hh, (m, l, acc) in enumerate(carry):
            k = k_ref[0, hh, pl.ds(start, blk), :]
            v = v_ref[0, hh, pl.ds(start, blk), :]
            s = lax.dot_general(qs[hh], k, (((1,), (1,)), ((), ())), preferred_element_type=f32)
            if masked:
                s = jnp.where(col <= row, s, NEG_INF)
            m_new = jnp.maximum(m, jnp.max(s, axis=1, keepdims=True))
            alpha = jnp.exp2(m - m_new)
            p = jnp.exp2(s - m_new)
            l = l * alpha + jnp.sum(p, axis=1, keepdims=True)
            acc = acc * alpha + jnp.dot(p.astype(bf16), v, preferred_element_type=f32)
            new.append((m_new, l, acc))
        return tuple(new)

    init = (jnp.full((blk, 1), NEG_INF, f32), jnp.zeros((blk, 1), f32), jnp.zeros((blk, FOX_HD), f32))
    carry = lax.fori_loop(0, qi, functools.partial(step, masked=False), (init, init))
    final = step(qi, carry, True)
    o_ref[0] = jnp.concatenate([acc / l for _, l, acc in final], axis=1).astype(o_ref.dtype)


def _fox_prompt(q, k, v, *, blk):
    nb, nh, ns, hd = v.shape
    pair = lambda w: pl.BlockSpec((1, 2, ns, w), lambda b, hp, qi: (b, hp, 0, 0))
    return pl.pallas_call(
        functools.partial(_fox_prompt_body, blk=blk),
        grid=(nb, nh // 2, ns // blk),
        in_specs=[pl.BlockSpec((1, 2, blk, q.shape[3]), lambda b, hp, qi: (b, hp, qi, 0)),
                  pair(k.shape[3]), pair(hd)],
        out_specs=pl.BlockSpec((1, blk, 2 * hd), lambda b, hp, qi: (b, qi, hp)),
        out_shape=jax.ShapeDtypeStruct((nb, ns, nh * hd), bf16),
        compiler_params=_cparams(("arbitrary", "arbitrary", "arbitrary")),
        name="fox_prompt",
    )(q, k, v)


UNIT_COLS = 16
RW_PAIRS = 4


def _rwkv_body(*refs, n_blk_b, tc, n_sb, n_st, prompt_mode):
    if prompt_mode:
        (p_ref, mu_ref, w0_ref, wup_ref, a0_ref, aup_ref, gup_ref, kk_w_ref, ka_ref, rk_ref, lnw_ref, lnb_ref,
         bd_ref, wsa_ref, ob_ref, e_ref, o_ref, s_ref, prev_ref, kk_s, dec_s, beta_s, kt_s, r_s, v_s, g_s,
         v0_s, v1_s, v2_s, vts_s, ots_s, stage_s, o_s) = refs
    else:
        (p_ref, pf_ref, sin_ref, mu_ref, w0_ref, wup_ref, a0_ref, aup_ref, gup_ref, kk_w_ref, ka_ref, rk_ref,
         lnw_ref, lnb_ref, bd_ref, wsa_ref, ob_ref, e_ref, o_ref, s_ref, kk_s, dec_s, beta_s, kt_s, r_s, v_s,
         g_s, v0_s, v1_s, v2_s, vts_s, ots_s, stage_s, o_s) = refs
    n_rows = n_blk_b * tc
    n_units = n_rows // UNIT_COLS
    step = pl.program_id(0)

    if prompt_mode:
        @pl.when(step == 0)
        def _():
            s_ref[...] = jnp.zeros_like(s_ref)
            prev_ref[...] = jnp.zeros_like(prev_ref)
    else:
        s_ref[...] = sin_ref[...]
    stage_s[...] = jnp.zeros_like(stage_s)

    p = p_ref[...].reshape(n_rows, RW_IN)
    rowi = lax.broadcasted_iota(jnp.int32, (n_rows, 1), 0)
    rolled = pltpu.roll(p, 1, axis=0)
    if prompt_mode:
        prev = rolled
        for bb in range(n_blk_b):
            prev = jnp.where(rowi == bb * tc, prev_ref[bb, 0:1, :], prev)
            prev_ref[bb, 0:1, :] = p[(bb + 1) * tc - 1:(bb + 1) * tc, :]
    else:
        prev = jnp.where(rowi % tc == 0, pf_ref[...], rolled)
    xs = p + (prev - p) * mu_ref[...]
    r = xs[:, 0:512]
    k = xs[:, 512:1024]
    v = xs[:, 1024:1536]
    zwa = xs[:, 1536:1664]
    zg = xs[:, 1664:1792]
    log_w = -RW_DECAY * _sigmoid(w0_ref[...] + jnp.dot(jnp.tanh(zwa).astype(bf16), wup_ref[...],
                                                        preferred_element_type=f32))
    a = _sigmoid(a0_ref[...] + jnp.dot(zwa.astype(bf16), aup_ref[...], preferred_element_type=f32))
    g = jnp.dot(_sigmoid(zg).astype(bf16), gup_ref[...], preferred_element_type=f32)
    bd = bd_ref[...]
    kk = k * kk_w_ref[...]
    kk = kk / jnp.maximum(jnp.sqrt(_split_dot(kk * kk, bd)), 1e-12)
    kt = k * (1.0 + (a - 1.0) * ka_ref[...])
    kk_s[...] = kk
    dec_s[...] = jnp.exp(log_w)
    beta_s[...] = kk * a
    kt_s[...] = kt
    r_s[...] = r
    v_s[...] = v
    g_s[...] = g
    v0 = v.astype(bf16).astype(f32)
    v1 = (v - v0).astype(bf16).astype(f32)
    v0_s[...] = v0
    v1_s[...] = v1
    v2_s[...] = v - v0 - v1

    def unit_block(u, bb):
        if prompt_mode:
            return bb * tc + u * n_st, n_st, 0
        return u * UNIT_COLS, UNIT_COLS, bb * n_st

    def vt_body(u, _):
        for bb in range(n_sb):
            blk0, blk_n, roff = unit_block(u, bb)
            rows = pl.ds(pl.multiple_of(blk0, 8), blk_n)
            for part, ref in enumerate((v0_s, v1_s, v2_s)):
                stage_s[part * n_st:(part + 1) * n_st, :] = ref[rows, :][roff:roff + n_st]
            vts_s[u * n_sb + bb] = jnp.transpose(stage_s[...])
        return 0

    lax.fori_loop(0, n_units, vt_body, 0)

    lane = lax.broadcasted_iota(jnp.int32, (FOX_HD, LANES), 1) % FOX_HD
    pairs = [(bb, pr) for bb in range(n_sb) for pr in range(RW_PAIRS)]

    def pack_pairs(tiles):
        return jnp.concatenate([jnp.concatenate(tiles[n:n + 2], axis=1) for n in range(0, len(tiles), 2)], axis=0)

    def unpack_pairs(res, n_tiles):
        return [res[(n // 2) * FOX_HD:(n // 2 + 1) * FOX_HD, (n % 2) * LANES:(n % 2 + 1) * LANES]
                for n in range(n_tiles)]

    def head_sums(tiles):
        return unpack_pairs(jnp.dot(pack_pairs(tiles), wsa_ref[...], preferred_element_type=f32), len(tiles))

    def unit_body(u, _):
        rowvecs, states, v_lhs = [], [], []
        for bb, pr in pairs:
            blk0, blk_n, roff = unit_block(u, bb)
            rows = pl.ds(pl.multiple_of(blk0, 8), blk_n)
            sl = slice(pr * LANES, (pr + 1) * LANES)
            rowvecs.append(tuple(ref[rows, sl][roff:roff + n_st] for ref in (kk_s, dec_s, beta_s, kt_s, r_s)))
            states.append(s_ref[u * n_sb + bb if not prompt_mode else bb, pr])
            vt = vts_s[u * n_sb + bb, sl, :]
            v_lhs.append((vt[0:FOX_HD] + pltpu.roll(vt[FOX_HD:LANES], 3 * n_st, axis=1)).astype(bf16))
        v_lhs = pack_pairs(v_lhs)
        ots = [jnp.zeros((FOX_HD, LANES), f32) for _ in range(RW_PAIRS)]
        for j in range(n_st):
            jr = slice(j, j + 1)
            pk = [(s * kk_b[jr]).astype(bf16) for (kk_b, _, _, _, _), s in zip(rowvecs, states)]
            sa_all = head_sums(pk)
            vb_all = unpack_pairs(jnp.dot(v_lhs, e_ref[j], preferred_element_type=f32), len(pairs))
            q_lhs = []
            for n, (_, w_b, be_b, kt_b, r_b) in enumerate(rowvecs):
                states[n] = states[n] * w_b[jr] - sa_all[n] * be_b[jr] + vb_all[n] * kt_b[jr]
                q_lhs.append((states[n] * r_b[jr]).astype(bf16))
            res_all = head_sums(q_lhs)
            for n, (bb, pr) in enumerate(pairs):
                ots[pr] = jnp.where(lane == bb * n_st + j, res_all[n], ots[pr])
        for n, (bb, pr) in enumerate(pairs):
            s_ref[u * n_sb + bb if not prompt_mode else bb, pr] = states[n]
        for pr in range(RW_PAIRS):
            ots_s[u, pr * FOX_HD:(pr + 1) * FOX_HD, :] = ots[pr]
        return 0

    lax.fori_loop(0, n_units, unit_body, 0)

    def ot_body(u, _):
        t = jnp.transpose(ots_s[u])
        ta, tb = t[0:UNIT_COLS], t[FOX_HD:FOX_HD + UNIT_COLS]
        o16 = jnp.concatenate(
            [x[:, pr * FOX_HD:(pr + 1) * FOX_HD] for pr in range(RW_PAIRS) for x in (ta, tb)], axis=1)
        if prompt_mode:
            for bb in range(n_sb):
                o_s[pl.ds(pl.multiple_of(bb * tc + u * n_st, 8), n_st), :] = o16[bb * n_st:(bb + 1) * n_st]
        else:
            o_s[pl.ds(pl.multiple_of(u * UNIT_COLS, 8), UNIT_COLS), :] = o16
        return 0

    lax.fori_loop(0, n_units, ot_body, 0)

    o = o_s[...]
    inv = 1.0 / FOX_HD
    mean = _split_dot(o, bd) * inv
    d = o - mean
    var = _split_dot(d * d, bd) * inv
    o_gn = d * lax.rsqrt(var + GN_EPS) * lnw_ref[...] + lnb_ref[...]
    bonus = _split_dot(r_s[...] * kt_s[...] * rk_ref[...], bd) * v_s[...]
    out = (o_gn + bonus) * g_s[...]
    o_ref[...] = out.reshape(o_ref.shape).astype(o_ref.dtype)


def _rwkv_weights(rw_mu, rw_w0, rw_w_up, rw_a0, rw_a_up, rw_g_up, rw_k_k, rw_k_a, rw_r_k, rw_ln_w, rw_ln_b):
    row = lambda a: a.reshape(1, -1)
    wup = jnp.concatenate([rw_w_up, jnp.zeros_like(rw_a_up)], axis=0).astype(bf16)
    aup = jnp.concatenate([jnp.zeros_like(rw_w_up), rw_a_up], axis=0).astype(bf16)
    return (row(rw_mu), row(rw_w0), wup, row(rw_a0), aup, rw_g_up.astype(bf16), row(rw_k_k), row(rw_k_a),
            row(rw_r_k), row(rw_ln_w), row(rw_ln_b), _block_diag_ones(RW_W, FOX_HD))


def _rwkv_scratch(n_rows, n_sb):
    n_units = n_rows // UNIT_COLS
    rows = [pltpu.VMEM((n_rows, RW_W), f32) for _ in range(10)]
    return rows + [pltpu.VMEM((n_units * n_sb, RW_W, LANES), f32),
                   pltpu.VMEM((n_units, RW_PAIRS * FOX_HD, LANES), f32),
                   pltpu.VMEM((LANES, RW_W), f32), pltpu.VMEM((n_rows, RW_W), f32)]


def _rwkv_selectors(n_st):
    ob = _block_diag_ones(LANES, FOX_HD)
    wsa = _block_diag_ones(2 * LANES, FOX_HD)
    shape = (n_st, 2 * LANES, 2 * LANES)
    r = lax.broadcasted_iota(jnp.int32, shape, 1)
    c = lax.broadcasted_iota(jnp.int32, shape, 2)
    j = lax.broadcasted_iota(jnp.int32, shape, 0)
    col = r % LANES
    e = ((r // LANES == c // LANES) & (col < 6 * n_st) & (col % n_st == j)
         & (col // (3 * n_st) == (c % LANES) // FOX_HD)).astype(bf16)
    return wsa, ob, e


def _rwkv_prompt(prw, weights, *, tc):
    nb, ns, _ = prw.shape
    const = lambda a: pl.BlockSpec(a.shape, lambda j: (0,) * a.ndim)
    n_st = UNIT_COLS // nb
    weights = tuple(weights) + _rwkv_selectors(n_st)
    return pl.pallas_call(
        functools.partial(_rwkv_body, n_blk_b=nb, tc=tc, n_sb=nb, n_st=n_st, prompt_mode=True),
        grid=(ns // tc,),
        in_specs=[pl.BlockSpec((nb, tc, RW_IN), lambda j: (0, j, 0))] + [const(w) for w in weights],
        out_specs=(pl.BlockSpec((nb, tc, RW_W), lambda j: (0, j, 0)),
                   pl.BlockSpec((nb, RW_PAIRS, FOX_HD, LANES), lambda j: (0, 0, 0, 0))),
        out_shape=(jax.ShapeDtypeStruct((nb, ns, RW_W), bf16),
                   jax.ShapeDtypeStruct((nb, RW_PAIRS, FOX_HD, LANES), f32)),
        scratch_shapes=[pltpu.VMEM((nb, 8, RW_IN), f32)] + _rwkv_scratch(nb * tc, nb),
        compiler_params=_cparams(("arbitrary",)),
        name="rwkv_prompt",
    )(prw, *weights)


def _rwkv_sample(prw_rows, prev_first_rows, state, weights, *, seq, rows_per_blk):
    n_rows = prw_rows.shape[0]
    nb_blk = rows_per_blk // seq
    const = lambda a: pl.BlockSpec(a.shape, lambda j: (0,) * a.ndim)
    n_sb = UNIT_COLS // seq
    weights = tuple(weights) + _rwkv_selectors(seq)
    return pl.pallas_call(
        functools.partial(_rwkv_body, n_blk_b=nb_blk, tc=seq, n_sb=n_sb, n_st=seq, prompt_mode=False),
        grid=(n_rows // rows_per_blk,),
        in_specs=[pl.BlockSpec((rows_per_blk, RW_IN), lambda j: (j, 0)),
                  pl.BlockSpec((rows_per_blk, RW_IN), lambda j: (j, 0)),
                  pl.BlockSpec((nb_blk, RW_PAIRS, FOX_HD, LANES), lambda j: (j, 0, 0, 0))]
                 + [const(w) for w in weights],
        out_specs=(pl.BlockSpec((rows_per_blk, RW_W), lambda j: (j, 0)),
                   pl.BlockSpec((nb_blk, RW_PAIRS, FOX_HD, LANES), lambda j: (j, 0, 0, 0))),
        out_shape=(jax.ShapeDtypeStruct((n_rows, RW_W), bf16),
                   jax.ShapeDtypeStruct((n_rows // seq, RW_PAIRS, FOX_HD, LANES), f32)),
        scratch_shapes=_rwkv_scratch(rows_per_blk, n_sb),
        compiler_params=_cparams(("arbitrary",)),
        name="rwkv_sample",
    )(prw_rows, prev_first_rows, state, *weights)


def _pack_state(state):
    b = state.shape[0]
    return state.reshape(b, RW_PAIRS, 2, FOX_HD, FOX_HD).transpose(0, 1, 3, 2, 4).reshape(b, RW_PAIRS, FOX_HD, LANES)


def _unpack_state(packed):
    b = packed.shape[0]
    return packed.reshape(b, RW_PAIRS, FOX_HD, 2, FOX_HD).transpose(0, 1, 3, 2, 4).reshape(b, 2 * RW_PAIRS, FOX_HD, FOX_HD)


def _fox_sample_body(pt_ref, *refs, n_pg, n_q):
    del pt_ref
    q_ref, kn_ref, vn_ref, lfn_ref, tri_ref, hm_ref = refs[:6]
    k_refs = refs[6:6 + n_pg]
    v_refs = refs[6 + n_pg:6 + 2 * n_pg]
    lf_refs = refs[6 + 2 * n_pg:6 + 3 * n_pg]
    o_ref, m_s, l_s, acc_s, carry_s = refs[6 + 3 * n_pg:]
    jg = pl.program_id(1)
    n_rows = n_q * FOX_HEADS
    page = tri_ref.shape[0]

    @pl.when(jg == 0)
    def _():
        m_s[...] = jnp.full_like(m_s, NEG_INF)
        l_s[...] = jnp.zeros_like(l_s)
        acc_s[...] = jnp.zeros_like(acc_s)
        carry_s[...] = jnp.zeros_like(carry_s)

    hm = hm_ref[...]
    q4 = q_ref[0]
    qm = jnp.concatenate([jnp.broadcast_to(q4[i:i + 1], (FOX_HEADS, FOX_W)) for i in range(n_q)], axis=0) * hm
    tri = tri_ref[...]

    def cum_bias(lf):
        ct = _split_dot(lf, tri) + carry_s[...]
        carry_s[...] = jnp.broadcast_to(ct[:, page - 1:page], carry_s.shape)
        return jnp.concatenate([ct] * n_q, axis=0)

    def update(s_all, pv):
        m_old = m_s[...]
        m_new = jnp.maximum(m_old, jnp.max(s_all, axis=1, keepdims=True))
        alpha = jnp.exp(m_old - m_new)
        p = jnp.exp(s_all - m_new)
        l_s[...] = l_s[...] * alpha + jnp.sum(p, axis=1, keepdims=True)
        acc_s[...] = acc_s[...] * alpha + pv(p.astype(bf16))
        m_s[...] = m_new

    def pv_pages(p):
        acc = jnp.zeros((n_rows, FOX_W), f32)
        for g in range(n_pg):
            acc = acc + lax.dot_general(p[:, g * page:(g + 1) * page], v_refs[g][0].astype(bf16),
                                        (((1,), (1,)), ((), ())), preferred_element_type=f32)
        return acc

    s_list = []
    for g in range(n_pg):
        s = jnp.dot(qm, k_refs[g][0].astype(bf16), preferred_element_type=f32)
        s_list.append(s - cum_bias(lf_refs[g][0]))
    update(jnp.concatenate(s_list, axis=1), pv_pages)

    @pl.when(jg == pl.num_programs(1) - 1)
    def _():
        pad = jnp.zeros((page - n_q, FOX_W), bf16)
        kn = jnp.concatenate([kn_ref[0], pad], axis=0)
        vn = jnp.concatenate([vn_ref[0], pad], axis=0)
        s = lax.dot_general(qm, kn, (((1,), (1,)), ((), ())), preferred_element_type=f32)
        bias = cum_bias(lfn_ref[0])
        key = lax.broadcasted_iota(jnp.int32, (n_rows, page), 1)
        qpos = lax.broadcasted_iota(jnp.int32, (n_rows, page), 0) // FOX_HEADS
        update(jnp.where(key <= qpos, s - bias, NEG_INF),
               lambda p: jnp.dot(p, vn, preferred_element_type=f32))
        o = acc_s[...] / l_s[...] * hm.astype(f32)
        o_ref[0] = jnp.concatenate(
            [jnp.sum(o[i * FOX_HEADS:(i + 1) * FOX_HEADS], axis=0, keepdims=True) for i in range(n_q)],
            axis=0).astype(o_ref.dtype)


def _fox_sample(page_table, q, kn, vn, lfn_t, cache_kt, cache_vt, cache_lft, *, n_pg):
    nb, n_q, _ = q.shape
    n_pages = page_table.shape[1]
    page = cache_kt.shape[2]
    n_rows = n_q * FOX_HEADS
    tri = (lax.broadcasted_iota(jnp.int32, (page, page), 0)
           <= lax.broadcasted_iota(jnp.int32, (page, page), 1)).astype(bf16)
    hm = (lax.broadcasted_iota(jnp.int32, (n_rows, FOX_W), 0) % FOX_HEADS
          == lax.broadcasted_iota(jnp.int32, (n_rows, FOX_W), 1) // FOX_HD).astype(bf16)
    per_b = lambda shape: pl.BlockSpec((1,) + shape, lambda b, j, pt: (b, 0, 0))
    const = lambda a: pl.BlockSpec(a.shape, lambda b, j, pt: (0, 0))
    paged = lambda shape, g: pl.BlockSpec((1,) + shape, lambda b, j, pt: (pt[b, j * n_pg + g], 0, 0))
    in_specs = ([per_b((n_q, FOX_W))] * 3 + [per_b((FOX_HEADS, page)), const(tri), const(hm)]
                + [paged((FOX_W, page), g) for g in range(n_pg)]
                + [paged((FOX_W, page), g) for g in range(n_pg)]
                + [paged((FOX_HEADS, page), g) for g in range(n_pg)])
    grid_spec = pltpu.PrefetchScalarGridSpec(
        num_scalar_prefetch=1, grid=(nb, n_pages // n_pg), in_specs=in_specs,
        out_specs=pl.BlockSpec((1, n_q, FOX_W), lambda b, j, pt: (b, 0, 0)),
        scratch_shapes=[pltpu.VMEM((n_rows, 1), f32), pltpu.VMEM((n_rows, 1), f32),
                        pltpu.VMEM((n_rows, FOX_W), f32), pltpu.VMEM((FOX_HEADS, page), f32)])
    return pl.pallas_call(
        functools.partial(_fox_sample_body, n_pg=n_pg, n_q=n_q),
        grid_spec=grid_spec,
        out_shape=jax.ShapeDtypeStruct((nb, n_q, FOX_W), bf16),
        compiler_params=_cparams(("arbitrary", "arbitrary")),
        name="fox_sample",
    )(page_table, q, kn, vn, lfn_t, tri, hm, *([cache_kt] * n_pg), *([cache_vt] * n_pg), *([cache_lft] * n_pg))


def _mem_kv_body(m_ref, g_ref, w_ref, knw_ref, k_ref, v_ref):
    n = _rms(m_ref[0], g_ref[...]).astype(bf16)
    kv = jnp.dot(n, w_ref[...], preferred_element_type=f32)
    for h in range(MEM_HEADS):
        sl = slice(h * MEM_HD, (h + 1) * MEM_HD)
        k_ref[0, :, sl] = _rms(kv[:, sl], knw_ref[...])
    v_ref[0] = kv[:, MEM_W:]


def _mem_kv(mem, mem_norm_w, w_mem_kv, mem_kn_w):
    nb, nm, _ = mem.shape
    const = lambda shape: pl.BlockSpec(shape, lambda b: (0,) * len(shape))
    out = jax.ShapeDtypeStruct((nb, nm, MEM_W), f32)
    return pl.pallas_call(
        _mem_kv_body,
        grid=(nb,),
        in_specs=[pl.BlockSpec((1, nm, D_MODEL), lambda b: (b, 0, 0)), const((1, D_MODEL)),
                  const((D_MODEL, 2 * MEM_W)), const((1, MEM_HD))],
        out_specs=(pl.BlockSpec((1, nm, MEM_W), lambda b: (b, 0, 0)),) * 2,
        out_shape=(out, out),
        compiler_params=_cparams(("arbitrary",)),
        name="mem_kv",
    )(mem, mem_norm_w.reshape(1, D_MODEL), w_mem_kv.astype(bf16), mem_kn_w.reshape(1, MEM_HD))


def _mem_attend_body(q_ref, k_ref, v_ref, o_ref):
    q = q_ref[0]
    for h in range(MEM_HEADS):
        sl = slice(h * MEM_HD, (h + 1) * MEM_HD)
        k = k_ref[0, :, sl].astype(bf16)
        v = v_ref[0, :, sl].astype(bf16)
        s = lax.dot_general(q[:, sl], k, (((1,), (1,)), ((), ())), preferred_element_type=f32) * MEM_SCALE
        e = jnp.exp(s - jnp.max(s, axis=1, keepdims=True))
        p = e / jnp.sum(e, axis=1, keepdims=True)
        o_ref[0, :, sl] = jnp.dot(p.astype(bf16), v, preferred_element_type=f32).astype(o_ref.dtype)


def _mem_attend_rows_body(q_ref, k_ref, v_ref, o_ref):
    n_mem = k_ref.shape[1] // MEM_HEADS
    for i in range(q_ref.shape[0]):
        q = q_ref[i]
        for h in range(MEM_HEADS):
            sl = slice(h * MEM_HD, (h + 1) * MEM_HD)
            k = k_ref[i, pl.ds(h, n_mem, stride=MEM_HEADS), :].astype(bf16)
            v = v_ref[i, pl.ds(h, n_mem, stride=MEM_HEADS), :].astype(bf16)
            s = lax.dot_general(q[:, sl], k, (((1,), (1,)), ((), ())), preferred_element_type=f32) * MEM_SCALE
            e = jnp.exp(s - jnp.max(s, axis=1, keepdims=True))
            p = e / jnp.sum(e, axis=1, keepdims=True)
            o_ref[i, :, sl] = jnp.dot(p.astype(bf16), v, preferred_element_type=f32).astype(o_ref.dtype)


def _mem_attend_rows(q, mk_rows, mv_rows, *, nb_blk):
    nb, ns, _ = q.shape
    rows = mk_rows.shape[1]
    kv = pl.BlockSpec((nb_blk, rows, MEM_HD), lambda b: (b, 0, 0))
    qo = pl.BlockSpec((nb_blk, ns, MEM_W), lambda b: (b, 0, 0))
    return pl.pallas_call(
        _mem_attend_rows_body,
        grid=(nb // nb_blk,),
        in_specs=[qo, kv, kv],
        out_specs=qo,
        out_shape=jax.ShapeDtypeStruct((nb, ns, MEM_W), bf16),
        compiler_params=_cparams(("arbitrary",)),
        name="mem_attend_rows",
    )(q, mk_rows, mv_rows)


def _mem_attend(q, mk, mv, *, tq):
    nb, ns, _ = q.shape
    nm = mk.shape[1]
    kv = pl.BlockSpec((1, nm, MEM_W), lambda b, j: (b, 0, 0))
    return pl.pallas_call(
        _mem_attend_body,
        grid=(nb, ns // tq),
        in_specs=[pl.BlockSpec((1, tq, MEM_W), lambda b, j: (b, j, 0)), kv, kv],
        out_specs=pl.BlockSpec((1, tq, MEM_W), lambda b, j: (b, j, 0)),
        out_shape=jax.ShapeDtypeStruct((nb, ns, MEM_W), bf16),
        compiler_params=_cparams(("arbitrary", "arbitrary")),
        name="mem_attend",
    )(q, mk, mv)


def _finish_body(x_ref, of_ref, or_ref, om_ref, g1_ref, wg_ref, wb_ref, wo_ref, g2_ref, h_ref, xn2_ref):
    x = x_ref[...]
    xn = _rms(x, g1_ref[...]).astype(bf16)
    merged = jnp.zeros_like(x)
    for n, br in enumerate((of_ref, or_ref, om_ref)):
        gate = _sigmoid(jnp.dot(xn, wg_ref[:, n * D_MODEL:(n + 1) * D_MODEL], preferred_element_type=f32))
        merged = merged + gate * jnp.dot(br[...], wb_ref[n], preferred_element_type=f32)
    h = x + jnp.dot(merged.astype(bf16), wo_ref[...], preferred_element_type=f32)
    h_ref[...] = h
    xn2_ref[...] = _rms(h, g2_ref[...]).astype(bf16)


def _finish(x, o_fox, o_rw, o_mem, norm1_w, w_gate, w_branch, w_out, norm2_w, *, tile):
    n = x.shape[0]
    const = lambda shape: pl.BlockSpec(shape, lambda j: (0,) * len(shape))
    tok = lambda w: pl.BlockSpec((tile, w), lambda j: (j, 0))
    return pl.pallas_call(
        _finish_body,
        grid=(n // tile,),
        in_specs=[tok(D_MODEL), tok(FOX_W), tok(RW_W), tok(MEM_W), const((1, D_MODEL)),
                  const((D_MODEL, 3 * D_MODEL)), const((3, FOX_W, D_MODEL)), const((D_MODEL, D_MODEL)),
                  const((1, D_MODEL))],
        out_specs=(tok(D_MODEL), tok(D_MODEL)),
        out_shape=(jax.ShapeDtypeStruct((n, D_MODEL), f32), jax.ShapeDtypeStruct((n, D_MODEL), bf16)),
        compiler_params=_cparams(("arbitrary",)),
        name="finish",
    )(x, o_fox, o_rw, o_mem, norm1_w.reshape(1, D_MODEL), w_gate, w_branch.astype(bf16), w_out.astype(bf16),
      norm2_w.reshape(1, D_MODEL))


_PEER_SLOTS = [(ra, rb) for ra in range(PEER_TOPK) for rb in range(PEER_TOPK) if (ra + 1) * (rb + 1) <= PEER_TOPK]
_PEER_SLOT_ROWS = -(-len(_PEER_SLOTS) // 8) * 8


def _top_rows(s, n_take):
    iota = lax.broadcasted_iota(jnp.int32, s.shape, 0).astype(f32)
    big = float(s.shape[0])
    vals, idxs = [], []
    for _ in range(n_take):
        m = jnp.max(s, axis=0, keepdims=True)
        idx = jnp.min(jnp.where(s == m, iota, big), axis=0, keepdims=True)
        vals.append(m)
        idxs.append(idx)
        s = jnp.where(iota == idx, -jnp.inf, s)
    return vals, idxs


def _route_body(x_ref, wq_ref, k1_ref, k2_ref, a_ref, b_ref, g_ref,
                q_s, cand_s, ea_s, eb_s, a_s, b_s, g_s, at_s, bt_s, gt_s):
    q_s[...] = lax.dot_general(wq_ref[...], x_ref[...], (((1,), (1,)), ((), ())),
                               preferred_element_type=f32).astype(bf16)
    cand_s[...] = jnp.full_like(cand_s, -jnp.inf)
    ea_s[...] = jnp.zeros_like(ea_s)
    eb_s[...] = jnp.zeros_like(eb_s)

    def head(h, _):
        base = pl.multiple_of(h * 2 * PEER_HALF, 2 * PEER_HALF)
        s1 = jnp.dot(k1_ref[h], q_s[pl.ds(base, PEER_HALF), :], preferred_element_type=f32)
        s2 = jnp.dot(k2_ref[h], q_s[pl.ds(base + PEER_HALF, PEER_HALF), :], preferred_element_type=f32)
        v1, i1 = _top_rows(s1, PEER_TOPK)
        v2, i2 = _top_rows(s2, PEER_TOPK)
        for slot, (ra, rb) in enumerate(_PEER_SLOTS):
            cand_s[slot:slot + 1, :] = v1[ra] + v2[rb]
            ea_s[slot:slot + 1, :] = i1[ra]
            eb_s[slot:slot + 1, :] = i2[rb]
        cand = cand_s[...]
        ea = ea_s[...]
        eb = eb_s[...]
        iota = lax.broadcasted_iota(jnp.int32, cand.shape, 0).astype(f32)
        scs = []
        for r in range(PEER_TOPK):
            m = jnp.max(cand, axis=0, keepdims=True)
            slot = jnp.min(jnp.where(cand == m, iota, float(_PEER_SLOT_ROWS)), axis=0, keepdims=True)
            hit = iota == slot
            a_s[r:r + 1, :] = jnp.sum(jnp.where(hit, ea, 0.0), axis=0, keepdims=True)
            b_s[r:r + 1, :] = jnp.sum(jnp.where(hit, eb, 0.0), axis=0, keepdims=True)
            cand = jnp.where(hit, -jnp.inf, cand)
            scs.append(m)
        es = [jnp.exp(sc - scs[0]) for sc in scs]
        z = es[0]
        for e in es[1:]:
            z = z + e
        for r in range(PEER_TOPK):
            g_s[r:r + 1, :] = es[r] / z
        rows = pl.ds(pl.multiple_of(h * PEER_TOPK, PEER_TOPK), PEER_TOPK)
        at_s[rows, :] = a_s[...]
        bt_s[rows, :] = b_s[...]
        gt_s[rows, :] = g_s[...]
        return 0

    lax.fori_loop(0, PEER_HEADS, head, 0)
    a_ref[...] = jnp.transpose(at_s[...])
    b_ref[...] = jnp.transpose(bt_s[...])
    g_ref[...] = jnp.transpose(gt_s[...])


def _route(xn2, wq_t, keys1, keys2, *, tile):
    n = xn2.shape[0]
    n_slots = PEER_HEADS * PEER_TOPK
    const = lambda shape: pl.BlockSpec(shape, lambda j: (0,) * len(shape))
    out = jax.ShapeDtypeStruct((n, n_slots), f32)
    ospec = pl.BlockSpec((tile, n_slots), lambda j: (j, 0))
    cand = lambda: pltpu.VMEM((_PEER_SLOT_ROWS, tile), f32)
    top = lambda: pltpu.VMEM((PEER_TOPK, tile), f32)
    full = lambda: pltpu.VMEM((n_slots, tile), f32)
    return pl.pallas_call(
        _route_body,
        grid=(n // tile,),
        in_specs=[pl.BlockSpec((tile, D_MODEL), lambda j: (j, 0)), const(wq_t.shape), const(keys1.shape),
                  const(keys2.shape)],
        out_specs=(ospec, ospec, ospec),
        out_shape=(out, out, out),
        scratch_shapes=[pltpu.VMEM((wq_t.shape[0], tile), bf16), cand(), cand(), cand(), top(), top(), top(),
                        full(), full(), full()],
        compiler_params=_cparams(("arbitrary",)),
        name="peer_route",
    )(xn2, wq_t, keys1, keys2)


G_PITCH = PEER_NK + 8
EXPERT_SUB = 2048


def _gelu_tanh(x):
    return 0.5 * x * (1.0 + jnp.tanh(math.sqrt(2.0 / math.pi) * (x + 0.044715 * (x * x * x))))


def _peer_body(x_ref, a_ref, b_ref, g_ref, h_ref, u_ref, v_ref, y_ref, gs_ref, acc_ref):
    tile = x_ref.shape[0]
    c = pl.program_id(1)
    n_exp = u_ref.shape[0]

    @pl.when(c == 0)
    def _():
        acc_ref[...] = jnp.zeros_like(acc_ref)
        sub = lax.broadcasted_iota(jnp.int32, (PEER_NK, LANES), 0).astype(f32)

        def tok8(t8, _):
            rows = pl.ds(pl.multiple_of(t8 * 8, 8), 8)
            a8, b8, g8 = a_ref[rows, :], b_ref[rows, :], g_ref[rows, :]
            for j in range(8):
                hit_a = sub == a8[j:j + 1]
                gate = jnp.where(hit_a, g8[j:j + 1], 0.0)
                g_hi = gate.astype(bf16)
                g_lo = (gate - g_hi.astype(f32)).astype(bf16)
                bt = jnp.where(sub == b8[j:j + 1], 1.0, 0.0).astype(bf16)
                gt = lax.dot_general(jnp.concatenate([g_hi, g_lo], axis=1), jnp.concatenate([bt, bt], axis=1),
                                     (((1,), (1,)), ((), ())), preferred_element_type=f32)
                gs_ref[pl.ds(pl.multiple_of((t8 * 8 + j) * G_PITCH, 8), PEER_NK), :] = gt
            return 0

        lax.fori_loop(0, tile // 8, tok8, 0)

    x = x_ref[...]
    acc = acc_ref[...]
    for s in range(n_exp // EXPERT_SUB):
        e0 = s * EXPERT_SUB
        act = lax.dot_general(x, u_ref[e0:e0 + EXPERT_SUB, :], (((1,), (1,)), ((), ())),
                              preferred_element_type=f32)
        i1 = (c * n_exp + e0) // PEER_NK
        gates = jnp.concatenate(
            [gs_ref[pl.ds(i1 + k, tile, stride=G_PITCH), :] for k in range(EXPERT_SUB // PEER_NK)], axis=1)
        w = (gates * _gelu_tanh(act)).astype(bf16)
        acc = acc + jnp.dot(w, v_ref[e0:e0 + EXPERT_SUB, :], preferred_element_type=f32)
    acc_ref[...] = acc

    @pl.when(c == pl.num_programs(1) - 1)
    def _():
        y_ref[...] = h_ref[...] + acc


def _peer(xn2, a, b, g, h, u_bf, v_bf, *, tile, chunk):
    n = xn2.shape[0]
    n_slots = a.shape[1]
    n_experts = u_bf.shape[0]
    tok = lambda w: pl.BlockSpec((tile, w), lambda j, c: (j, 0))
    tab = pl.BlockSpec((chunk, D_MODEL), lambda j, c: (c, 0))
    return pl.pallas_call(
        _peer_body,
        grid=(n // tile, n_experts // chunk),
        in_specs=[tok(D_MODEL), tok(n_slots), tok(n_slots), tok(n_slots), tok(D_MODEL), tab, tab],
        out_specs=tok(D_MODEL),
        out_shape=jax.ShapeDtypeStruct((n, D_MODEL), f32),
        scratch_shapes=[pltpu.VMEM((tile * G_PITCH, PEER_NK), f32), pltpu.VMEM((tile, D_MODEL), f32)],
        compiler_params=_cparams(("arbitrary", "arbitrary")),
        name="peer_experts",
    )(xn2, a, b, g, h, u_bf, v_bf)


TOK_TILE = 256
ROUTE_TILE = 512
ATT_BLK = 1024
RW_CHUNK = 128
RW_SAMPLE_ROWS = 128
PAGES_PER_STEP = 32
EXPERT_CHUNK = 2048
MEM_SAMPLE_BATCHES = 4


def _merge_and_peer(x2, o_fox, o_rw, o_mem, shared):
    h, xn2 = _finish(x2, o_fox, o_rw, o_mem, shared["norm1_w"], shared["w_gate"], shared["w_branch"],
                     shared["w_out"], shared["norm2_w"], tile=TOK_TILE)
    a, b, g = _route(xn2, shared["wq_t"], shared["keys1"], shared["keys2"], tile=ROUTE_TILE)
    return _peer(xn2, a, b, g, h, shared["u_bf"], shared["v_bf"], tile=TOK_TILE, chunk=EXPERT_CHUNK)


def kernel(x_prompt, x_sample, cache_fox_k, cache_fox_v, cache_fox_logf, cache_mem_k, cache_mem_v, state_rwkv, state_rwkv_shift, page_table, mem_prompt, norm1_w, w_in, fox_b_f, fox_qn_w, fox_kn_w, rw_mu, rw_w0, rw_w_up, rw_a0, rw_a_up, rw_g_up, rw_k_k, rw_k_a, rw_r_k, rw_ln_w, rw_ln_b, mem_norm_w, w_mem_kv, mem_qn_w, mem_kn_w, w_branch, w_out, norm2_w, peer_w_q, peer_keys1, peer_keys2, peer_u, peer_v):
    bp, sp, _ = x_prompt.shape
    db, ds, _ = x_sample.shape
    w_pack, bf_pad, qnw, knw, mqw, w_gate = _prep_inproj_weights(w_in, fox_b_f, fox_qn_w, fox_kn_w, mem_qn_w)
    g1 = norm1_w.reshape(1, D_MODEL)
    rw_w = _rwkv_weights(rw_mu, rw_w0, rw_w_up, rw_a0, rw_a_up, rw_g_up, rw_k_k, rw_k_a, rw_r_k, rw_ln_w, rw_ln_b)
    shared = dict(norm1_w=norm1_w, w_gate=w_gate, w_branch=w_branch, w_out=w_out, norm2_w=norm2_w,
                  wq_t=peer_w_q.T.astype(bf16), keys1=peer_keys1.astype(bf16), keys2=peer_keys2.astype(bf16),
                  u_bf=peer_u.astype(bf16), v_bf=peer_v.astype(bf16))

    qh, kh, vh, k_p, v_p, logf_p, prw_p, mq_p = _inproj(
        x_prompt, g1, w_pack, bf_pad, qnw, knw, mqw, head_major=True, tile=TOK_TILE)
    o_fox = _fox_prompt(qh, kh, vh, blk=ATT_BLK)
    o_rw, st_p = _rwkv_prompt(prw_p, rw_w, tc=RW_CHUNK)
    mem_k_p, mem_v_p = _mem_kv(mem_prompt, mem_norm_w, w_mem_kv, mem_kn_w)
    o_mem = _mem_attend(mq_p, mem_k_p, mem_v_p, tq=ATT_BLK)
    n_p = bp * sp
    y_prompt = _merge_and_peer(x_prompt.reshape(n_p, D_MODEL), o_fox.reshape(n_p, FOX_W), o_rw.reshape(n_p, RW_W),
                               o_mem.reshape(n_p, MEM_W), shared).reshape(bp, sp, D_MODEL)

    n_s = db * ds
    qt, kt, vt, k_s, v_s, logf_s, prw_s, mq_s = _inproj(
        x_sample.reshape(1, n_s, D_MODEL), g1, w_pack, bf_pad, qnw, knw, mqw, head_major=False, tile=TOK_TILE)
    pool, page = cache_fox_k.shape[:2]
    logf_s = logf_s.reshape(db, ds, LANES)[:, :, :FOX_HEADS]
    lfn_t = jnp.pad(jnp.swapaxes(logf_s, 1, 2), ((0, 0), (0, 0), (0, page - ds)))
    pos_minor = lambda a: jnp.transpose(a, (0, 2, 3, 1)).reshape(pool, FOX_W, page)
    o_fox_s = _fox_sample(page_table, qt.reshape(db, ds, FOX_W), kt.reshape(db, ds, FOX_W), vt.reshape(db, ds, FOX_W),
                          lfn_t, pos_minor(cache_fox_k), pos_minor(cache_fox_v), jnp.swapaxes(cache_fox_logf, 1, 2),
                          n_pg=PAGES_PER_STEP)
    prw_rows = prw_s.reshape(n_s, RW_IN)
    o_rw_s, st_s = _rwkv_sample(prw_rows, jnp.repeat(state_rwkv_shift, ds, axis=0), _pack_state(state_rwkv), rw_w,
                                seq=ds, rows_per_blk=RW_SAMPLE_ROWS)
    nm = cache_mem_k.shape[1]
    o_mem_s = _mem_attend_rows(mq_s.reshape(db, ds, MEM_W), cache_mem_k.reshape(db, nm * MEM_HEADS, MEM_HD),
                               cache_mem_v.reshape(db, nm * MEM_HEADS, MEM_HD), nb_blk=MEM_SAMPLE_BATCHES)
    y_sample = _merge_and_peer(x_sample.reshape(n_s, D_MODEL), o_fox_s.reshape(n_s, FOX_W), o_rw_s,
                               o_mem_s.reshape(n_s, MEM_W), shared).reshape(db, ds, D_MODEL)

    heads = lambda a, b, s: a.reshape(b, s, FOX_HEADS, FOX_HD)
    return (y_prompt, y_sample,
            heads(k_p, bp, sp), heads(v_p, bp, sp), logf_p[:, :, :FOX_HEADS],
            _unpack_state(st_p), prw_p[:, -1],
            mem_k_p.reshape(bp, nm, MEM_HEADS, MEM_HD), mem_v_p.reshape(bp, nm, MEM_HEADS, MEM_HD),
            heads(k_s, db, ds), heads(v_s, db, ds), logf_s,
            _unpack_state(st_s), prw_rows.reshape(db, ds, RW_IN)[:, -1])
```

```python
import functools
import math

import jax
import jax.numpy as jnp
from jax import lax
from jax.experimental import pallas as pl
from jax.experimental.pallas import tpu as pltpu

f32 = jnp.float32
bf16 = jnp.bfloat16

D_MODEL = 1024
RMS_EPS = 1e-6
NEG_INF = -1e30
FOX_HEADS = 8
FOX_HD = 64
FOX_W = 512
FOX_SCALE = FOX_HD ** -0.5
LOG2E = math.log2(math.e)
RW_W = 512
RW_IN = 1792
RW_DECAY = math.exp(-0.5)
GN_EPS = 64e-5
MEM_HEADS = 4
MEM_HD = 128
MEM_W = 512
MEM_SCALE = MEM_HD ** -0.5
PEER_HEADS = 8
PEER_NK = 128
PEER_HALF = 128
PEER_TOPK = 16

LANES = 128
VMEM_LIMIT = 56 * 1024 * 1024

_PK_Q, _PK_K, _PK_V, _PK_RW, _PK_MQ, _PK_F, _PK_END = 0, 512, 1024, 1536, 3328, 3840, 3968


def _cparams(sem):
    return pltpu.CompilerParams(dimension_semantics=sem, vmem_limit_bytes=VMEM_LIMIT)


def _block_diag_ones(width, group):
    r = lax.broadcasted_iota(jnp.int32, (width, width), 0) // group
    c = lax.broadcasted_iota(jnp.int32, (width, width), 1) // group
    return (r == c).astype(bf16)


def _split_dot(x, w):
    hi = x.astype(bf16)
    lo = (x - hi.astype(f32)).astype(bf16)
    return (jnp.dot(hi, w, preferred_element_type=f32)
            + jnp.dot(lo, w, preferred_element_type=f32))


def _rms(x, g):
    return x * lax.rsqrt(jnp.mean(x * x, axis=-1, keepdims=True) + RMS_EPS) * g


def _log_sigmoid(x):
    return jnp.minimum(x, 0.0) - jnp.log1p(jnp.exp(-jnp.abs(x)))


def _sigmoid(x):
    return 1.0 / (1.0 + jnp.exp(-x))


def _inproj_body(x_ref, g_ref, w_ref, bf_ref, qnw_ref, knw_ref, mqw_ref, bd_ref, tri_ref, sc_ref, ones_ref,
                 q_ref, kb_ref, vb_ref, kp_ref, vp_ref, logf_ref, prw_ref, mq_ref,
                 carry_ref, *, head_major):
    @pl.when(pl.program_id(1) == 0)
    def _():
        carry_ref[...] = jnp.zeros_like(carry_ref)

    x = x_ref[0]
    xn = _rms(x, g_ref[...]).astype(bf16)
    p = jnp.dot(xn, w_ref[...], preferred_element_type=f32)
    q = p[:, _PK_Q:_PK_K]
    k = p[:, _PK_K:_PK_V]
    v = p[:, _PK_V:_PK_RW]
    prw_ref[0] = p[:, _PK_RW:_PK_MQ]
    bd = bd_ref[...]
    qn = q * lax.rsqrt(_split_dot(q * q, bd) * (1.0 / FOX_HD) + RMS_EPS) * qnw_ref[...]
    kn = k * lax.rsqrt(_split_dot(k * k, bd) * (1.0 / FOX_HD) + RMS_EPS) * knw_ref[...]
    kp_ref[0] = kn
    vp_ref[0] = v
    kb = kn.astype(bf16)
    vb = v.astype(bf16)
    lane = lax.broadcasted_iota(jnp.int32, (x.shape[0], LANES), 1)
    logf = jnp.where(lane < FOX_HEADS, _log_sigmoid(p[:, _PK_F:_PK_END] + bf_ref[...]), 0.0)
    logf_ref[0] = logf
    for h in range(MEM_HEADS):
        sl = slice(_PK_MQ + h * MEM_HD, _PK_MQ + (h + 1) * MEM_HD)
        mq_ref[0, :, h * MEM_HD:(h + 1) * MEM_HD] = _rms(p[:, sl], mqw_ref[...]).astype(bf16)
    if not head_major:
        q_ref[0] = (qn * FOX_SCALE).astype(bf16)
        kb_ref[0] = kb
        vb_ref[0] = vb
        return
    c = _split_dot_left(tri_ref[...], logf) + carry_ref[0:1, :]
    carry_ref[0:1, :] = c[x.shape[0] - 1:x.shape[0], :]
    c2 = c * LOG2E
    c_hi = c2.astype(bf16)
    rest = c2 - c_hi.astype(f32)
    c_mid = rest.astype(bf16)
    parts = (c_hi, c_mid, (rest - c_mid.astype(f32)).astype(bf16))
    extra_q = ones_ref[0:1, :]
    extra_k = ones_ref[1:2, :]
    for i, part in enumerate(parts):
        extra_q = extra_q + jnp.dot(part, sc_ref[i], preferred_element_type=f32)
        extra_k = extra_k + jnp.dot(part, sc_ref[len(parts) + i], preferred_element_type=f32)
    qs = (qn * (FOX_SCALE * LOG2E)).astype(bf16)
    for h in range(FOX_HEADS):
        pair = slice((h // 2) * LANES, (h // 2 + 1) * LANES)
        own = slice(h * LANES, (h + 1) * LANES)
        in_own_half = lane // FOX_HD == h % 2
        q_ref[0, h] = jnp.where(in_own_half, qs[:, pair], extra_q[:, own].astype(bf16))
        kb_ref[0, h] = jnp.where(in_own_half, kb[:, pair], extra_k[:, own].astype(bf16))
        vb_ref[0, h] = vb[:, h * FOX_HD:(h + 1) * FOX_HD]


def _split_dot_left(w, x):
    hi = x.astype(bf16)
    lo = (x - hi.astype(f32)).astype(bf16)
    return (jnp.dot(w, hi, preferred_element_type=f32)
            + jnp.dot(w, lo, preferred_element_type=f32))


def _inproj(x3, norm1_w, w_pack, bf_pad, qnw, knw, mqw, *, head_major, tile):
    nb, ns, _ = x3.shape
    nt = ns // tile
    bd = _block_diag_ones(FOX_W, FOX_HD)
    tri = (lax.broadcasted_iota(jnp.int32, (tile, tile), 0)
           >= lax.broadcasted_iota(jnp.int32, (tile, tile), 1)).astype(bf16)
    const = lambda shape: pl.BlockSpec(shape, lambda b, j: (0,) * len(shape))
    tok = lambda w: pl.BlockSpec((1, tile, w), lambda b, j: (b, j, 0))
    if head_major:
        hm = lambda w: (jax.ShapeDtypeStruct((nb, FOX_HEADS, ns, w), bf16),
                        pl.BlockSpec((1, FOX_HEADS, tile, w), lambda b, j: (b, 0, j, 0)))
        qkv = (hm(LANES), hm(LANES), hm(FOX_HD))
    else:
        qkv = ((jax.ShapeDtypeStruct((nb, ns, FOX_W), bf16), tok(FOX_W)),) * 3
    n_parts = 3
    lanes = lax.broadcasted_iota(jnp.int32, (2 * n_parts, LANES, FOX_HEADS * LANES), 2)
    rows = lax.broadcasted_iota(jnp.int32, (2 * n_parts, LANES, FOX_HEADS * LANES), 1)
    slot = lax.broadcasted_iota(jnp.int32, (2 * n_parts, LANES, FOX_HEADS * LANES), 0)
    head = lanes // LANES
    off = lanes % LANES - (1 - head % 2) * FOX_HD
    sc = jnp.where((rows == head) & (off == slot), jnp.where(slot < n_parts, 1.0, -1.0), 0.0).astype(bf16)
    off1 = off[0, 0:1, :]
    ones = jnp.concatenate([(off1 >= n_parts) & (off1 < 2 * n_parts), (off1 >= 0) & (off1 < n_parts)],
                           axis=0).astype(f32)
    out_shape = tuple(s for s, _ in qkv) + (
        jax.ShapeDtypeStruct((nb, ns, FOX_W), f32),
        jax.ShapeDtypeStruct((nb, ns, FOX_W), f32),
        jax.ShapeDtypeStruct((nb, ns, LANES), f32),
        jax.ShapeDtypeStruct((nb, ns, RW_IN), f32),
        jax.ShapeDtypeStruct((nb, ns, MEM_W), bf16),
    )
    out_specs = tuple(s for _, s in qkv) + (tok(FOX_W), tok(FOX_W), tok(LANES), tok(RW_IN), tok(MEM_W))
    return pl.pallas_call(
        functools.partial(_inproj_body, head_major=head_major),
        grid=(nb, nt),
        in_specs=[tok(D_MODEL), const((1, D_MODEL)), const((D_MODEL, _PK_END)), const((1, LANES)),
                  const((1, FOX_W)), const((1, FOX_W)), const((1, MEM_HD)), const((FOX_W, FOX_W)),
                  const((tile, tile)), const(sc.shape), const(ones.shape)],
        out_specs=out_specs,
        out_shape=out_shape,
        scratch_shapes=[pltpu.VMEM((8, LANES), f32)],
        compiler_params=_cparams(("arbitrary", "arbitrary")),
        name="inproj",
    )(x3, norm1_w, w_pack, bf_pad, qnw, knw, mqw, bd, tri, sc, ones)


def _prep_inproj_weights(w_in, fox_b_f, fox_qn_w, fox_kn_w, mem_qn_w):
    fox_in = 3 * FOX_W + FOX_HEADS
    w_fox = w_in[:, :fox_in]
    w_rw = w_in[:, fox_in:fox_in + RW_IN]
    w_mq = w_in[:, fox_in + RW_IN:fox_in + RW_IN + MEM_W]
    w_gate = w_in[:, fox_in + RW_IN + MEM_W:]
    w_f = jnp.pad(w_fox[:, 3 * FOX_W:], ((0, 0), (0, LANES - FOX_HEADS)))
    w_pack = jnp.concatenate([w_fox[:, :3 * FOX_W], w_rw, w_mq, w_f], axis=1).astype(bf16)
    bf_pad = jnp.pad(fox_b_f, (0, LANES - FOX_HEADS)).reshape(1, LANES)
    qnw = jnp.tile(fox_qn_w, FOX_HEADS).reshape(1, FOX_W)
    knw = jnp.tile(fox_kn_w, FOX_HEADS).reshape(1, FOX_W)
    return w_pack, bf_pad, qnw, knw, mem_qn_w.reshape(1, MEM_HD), w_gate.astype(bf16)


def _fox_prompt_body(q_ref, k_ref, v_ref, o_ref, *, blk):
    qi = pl.program_id(2)
    row = lax.broadcasted_iota(jnp.int32, (blk, blk), 0)
    col = lax.broadcasted_iota(jnp.int32, (blk, blk), 1)
    qs = [q_ref[0, hh] for hh in range(2)]

    def step(kj, carry, masked):
        start = pl.multiple_of(kj * blk, blk)
        new = []
        for hh, (m, l, acc) in enumerate(carry):
            k = k_ref[0, hh, pl.ds(start, blk), :]
            v = v_ref[0, hh, pl.ds(start, blk), :]
            s = lax.dot_general(qs[hh], k, (((1,), (1,)), ((), ())), preferred_element_type=f32)
            if masked:
                s = jnp.where(col <= row, s, NEG_INF)
            m_new = jnp.maximum(m, jnp.max(s, axis=1, keepdims=True))
            alpha = jnp.exp2(m - m_new)
            p = jnp.exp2(s - m_new)
            l = l * alpha + jnp.sum(p, axis=1, keepdims=True)
            acc = acc * alpha + jnp.dot(p.astype(bf16), v, preferred_element_type=f32)
            new.append((m_new, l, acc))
        return tuple(new)

    init = (jnp.full((blk, 1), NEG_INF, f32), jnp.zeros((blk, 1), f32), jnp.zeros((blk, FOX_HD), f32))
    carry = lax.fori_loop(0, qi, functools.partial(step, masked=False), (init, init))
    final = step(qi, carry, True)
    o_ref[0] = jnp.concatenate([acc / l for _, l, acc in final], axis=1).astype(o_ref.dtype)


def _fox_prompt(q, k, v, *, blk):
    nb, nh, ns, hd = v.shape
    pair = lambda w: pl.BlockSpec((1, 2, ns, w), lambda b, hp, qi: (b, hp, 0, 0))
    return pl.pallas_call(
        functools.partial(_fox_prompt_body, blk=blk),
        grid=(nb, nh // 2, ns // blk),
        in_specs=[pl.BlockSpec((1, 2, blk, q.shape[3]), lambda b, hp, qi: (b, hp, qi, 0)),
                  pair(k.shape[3]), pair(hd)],
        out_specs=pl.BlockSpec((1, blk, 2 * hd), lambda b, hp, qi: (b, qi, hp)),
        out_shape=jax.ShapeDtypeStruct((nb, ns, nh * hd), bf16),
        compiler_params=_cparams(("arbitrary", "arbitrary", "arbitrary")),
        name="fox_prompt",
    )(q, k, v)


UNIT_COLS = 16
RW_PAIRS = 4


def _rwkv_body(*refs, n_blk_b, tc, n_sb, n_st, prompt_mode):
    if prompt_mode:
        (p_ref, mu_ref, w0_ref, wup_ref, a0_ref, aup_ref, gup_ref, kk_w_ref, ka_ref, rk_ref, lnw_ref, lnb_ref,
         bd_ref, wsa_ref, ob_ref, e_ref, o_ref, s_ref, prev_ref, kk_s, dec_s, beta_s, kt_s, r_s, v_s, g_s,
         v0_s, v1_s, v2_s, vts_s, ots_s, stage_s, o_s) = refs
    else:
        (p_ref, pf_ref, sin_ref, mu_ref, w0_ref, wup_ref, a0_ref, aup_ref, gup_ref, kk_w_ref, ka_ref, rk_ref,
         lnw_ref, lnb_ref, bd_ref, wsa_ref, ob_ref, e_ref, o_ref, s_ref, kk_s, dec_s, beta_s, kt_s, r_s, v_s,
         g_s, v0_s, v1_s, v2_s, vts_s, ots_s, stage_s, o_s) = refs
    n_rows = n_blk_b * tc
    n_units = n_rows // UNIT_COLS
    step = pl.program_id(0)

    if prompt_mode:
        @pl.when(step == 0)
        def _():
            s_ref[...] = jnp.zeros_like(s_ref)
            prev_ref[...] = jnp.zeros_like(prev_ref)
    else:
        s_ref[...] = sin_ref[...]
    stage_s[...] = jnp.zeros_like(stage_s)

    p = p_ref[...].reshape(n_rows, RW_IN)
    rowi = lax.broadcasted_iota(jnp.int32, (n_rows, 1), 0)
    rolled = pltpu.roll(p, 1, axis=0)
    if prompt_mode:
        prev = rolled
        for bb in range(n_blk_b):
            prev = jnp.where(rowi == bb * tc, prev_ref[bb, 0:1, :], prev)
            prev_ref[bb, 0:1, :] = p[(bb + 1) * tc - 1:(bb + 1) * tc, :]
    else:
        prev = jnp.where(rowi % tc == 0, pf_ref[...], rolled)
    xs = p + (prev - p) * mu_ref[...]
    r = xs[:, 0:512]
    k = xs[:, 512:1024]
    v = xs[:, 1024:1536]
    zwa = xs[:, 1536:1664]
    zg = xs[:, 1664:1792]
    log_w = -RW_DECAY * _sigmoid(w0_ref[...] + jnp.dot(jnp.tanh(zwa).astype(bf16), wup_ref[...],
                                                        preferred_element_type=f32))
    a = _sigmoid(a0_ref[...] + jnp.dot(zwa.astype(bf16), aup_ref[...], preferred_element_type=f32))
    g = jnp.dot(_sigmoid(zg).astype(bf16), gup_ref[...], preferred_element_type=f32)
    bd = bd_ref[...]
    kk = k * kk_w_ref[...]
    kk = kk / jnp.maximum(jnp.sqrt(_split_dot(kk * kk, bd)), 1e-12)
    kt = k * (1.0 + (a - 1.0) * ka_ref[...])
    kk_s[...] = kk
    dec_s[...] = jnp.exp(log_w)
    beta_s[...] = kk * a
    kt_s[...] = kt
    r_s[...] = r
    v_s[...] = v
    g_s[...] = g
    v0 = v.astype(bf16).astype(f32)
    v1 = (v - v0).astype(bf16).astype(f32)
    v0_s[...] = v0
    v1_s[...] = v1
    v2_s[...] = v - v0 - v1

    def unit_block(u, bb):
        if prompt_mode:
            return bb * tc + u * n_st, n_st, 0
        return u * UNIT_COLS, UNIT_COLS, bb * n_st

    def vt_body(u, _):
        for bb in range(n_sb):
            blk0, blk_n, roff = unit_block(u, bb)
            rows = pl.ds(pl.multiple_of(blk0, 8), blk_n)
            for part, ref in enumerate((v0_s, v1_s, v2_s)):
                stage_s[part * n_st:(part + 1) * n_st, :] = ref[rows, :][roff:roff + n_st]
            vts_s[u * n_sb + bb] = jnp.transpose(stage_s[...])
        return 0

    lax.fori_loop(0, n_units, vt_body, 0)

    lane = lax.broadcasted_iota(jnp.int32, (FOX_HD, LANES), 1) % FOX_HD
    pairs = [(bb, pr) for bb in range(n_sb) for pr in range(RW_PAIRS)]

    def pack_pairs(tiles):
        return jnp.concatenate([jnp.concatenate(tiles[n:n + 2], axis=1) for n in range(0, len(tiles), 2)], axis=0)

    def unpack_pairs(res, n_tiles):
        return [res[(n // 2) * FOX_HD:(n // 2 + 1) * FOX_HD, (n % 2) * LANES:(n % 2 + 1) * LANES]
                for n in range(n_tiles)]

    def head_sums(tiles):
        return unpack_pairs(jnp.dot(pack_pairs(tiles), wsa_ref[...], preferred_element_type=f32), len(tiles))

    def unit_body(u, _):
        rowvecs, states, v_lhs = [], [], []
        for bb, pr in pairs:
            blk0, blk_n, roff = unit_block(u, bb)
            rows = pl.ds(pl.multiple_of(blk0, 8), blk_n)
            sl = slice(pr * LANES, (pr + 1) * LANES)
            rowvecs.append(tuple(ref[rows, sl][roff:roff + n_st] for ref in (kk_s, dec_s, beta_s, kt_s, r_s)))
            states.append(s_ref[u * n_sb + bb if not prompt_mode else bb, pr])
            vt = vts_s[u * n_sb + bb, sl, :]
            v_lhs.append((vt[0:FOX_HD] + pltpu.roll(vt[FOX_HD:LANES], 3 * n_st, axis=1)).astype(bf16))
        v_lhs = pack_pairs(v_lhs)
        ots = [jnp.zeros((FOX_HD, LANES), f32) for _ in range(RW_PAIRS)]
        for j in range(n_st):
            jr = slice(j, j + 1)
            pk = [(s * kk_b[jr]).astype(bf16) for (kk_b, _, _, _, _), s in zip(rowvecs, states)]
            sa_all = head_sums(pk)
            vb_all = unpack_pairs(jnp.dot(v_lhs, e_ref[j], preferred_element_type=f32), len(pairs))
            q_lhs = []
            for n, (_, w_b, be_b, kt_b, r_b) in enumerate(rowvecs):
                states[n] = states[n] * w_b[jr] - sa_all[n] * be_b[jr] + vb_all[n] * kt_b[jr]
                q_lhs.append((states[n] * r_b[jr]).astype(bf16))
            res_all = head_sums(q_lhs)
            for n, (bb, pr) in enumerate(pairs):
                ots[pr] = jnp.where(lane == bb * n_st + j, res_all[n], ots[pr])
        for n, (bb, pr) in enumerate(pairs):
            s_ref[u * n_sb + bb if not prompt_mode else bb, pr] = states[n]
        for pr in range(RW_PAIRS):
            ots_s[u, pr * FOX_HD:(pr + 1) * FOX_HD, :] = ots[pr]
        return 0

    lax.fori_loop(0, n_units, unit_body, 0)

    def ot_body(u, _):
        t = jnp.transpose(ots_s[u])
        ta, tb = t[0:UNIT_COLS], t[FOX_HD:FOX_HD + UNIT_COLS]
        o16 = jnp.concatenate(
            [x[:, pr * FOX_HD:(pr + 1) * FOX_HD] for pr in range(RW_PAIRS) for x in (ta, tb)], axis=1)
        if prompt_mode:
            for bb in range(n_sb):
                o_s[pl.ds(pl.multiple_of(bb * tc + u * n_st, 8), n_st), :] = o16[bb * n_st:(bb + 1) * n_st]
        else:
            o_s[pl.ds(pl.multiple_of(u * UNIT_COLS, 8), UNIT_COLS), :] = o16
        return 0

    lax.fori_loop(0, n_units, ot_body, 0)

    o = o_s[...]
    inv = 1.0 / FOX_HD
    mean = _split_dot(o, bd) * inv
    d = o - mean
    var = _split_dot(d * d, bd) * inv
    o_gn = d * lax.rsqrt(var + GN_EPS) * lnw_ref[...] + lnb_ref[...]
    bonus = _split_dot(r_s[...] * kt_s[...] * rk_ref[...], bd) * v_s[...]
    out = (o_gn + bonus) * g_s[...]
    o_ref[...] = out.reshape(o_ref.shape).astype(o_ref.dtype)


def _rwkv_weights(rw_mu, rw_w0, rw_w_up, rw_a0, rw_a_up, rw_g_up, rw_k_k, rw_k_a, rw_r_k, rw_ln_w, rw_ln_b):
    row = lambda a: a.reshape(1, -1)
    wup = jnp.concatenate([rw_w_up, jnp.zeros_like(rw_a_up)], axis=0).astype(bf16)
    aup = jnp.concatenate([jnp.zeros_like(rw_w_up), rw_a_up], axis=0).astype(bf16)
    return (row(rw_mu), row(rw_w0), wup, row(rw_a0), aup, rw_g_up.astype(bf16), row(rw_k_k), row(rw_k_a),
            row(rw_r_k), row(rw_ln_w), row(rw_ln_b), _block_diag_ones(RW_W, FOX_HD))


def _rwkv_scratch(n_rows, n_sb):
    n_units = n_rows // UNIT_COLS
    rows = [pltpu.VMEM((n_rows, RW_W), f32) for _ in range(10)]
    return rows + [pltpu.VMEM((n_units * n_sb, RW_W, LANES), f32),
                   pltpu.VMEM((n_units, RW_PAIRS * FOX_HD, LANES), f32),
                   pltpu.VMEM((LANES, RW_W), f32), pltpu.VMEM((n_rows, RW_W), f32)]


def _rwkv_selectors(n_st):
    ob = _block_diag_ones(LANES, FOX_HD)
    wsa = _block_diag_ones(2 * LANES, FOX_HD)
    shape = (n_st, 2 * LANES, 2 * LANES)
    r = lax.broadcasted_iota(jnp.int32, shape, 1)
    c = lax.broadcasted_iota(jnp.int32, shape, 2)
    j = lax.broadcasted_iota(jnp.int32, shape, 0)
    col = r % LANES
    e = ((r // LANES == c // LANES) & (col < 6 * n_st) & (col % n_st == j)
         & (col // (3 * n_st) == (c % LANES) // FOX_HD)).astype(bf16)
    return wsa, ob, e


def _rwkv_prompt(prw, weights, *, tc):
    nb, ns, _ = prw.shape
    const = lambda a: pl.BlockSpec(a.shape, lambda j: (0,) * a.ndim)
    n_st = UNIT_COLS // nb
    weights = tuple(weights) + _rwkv_selectors(n_st)
    return pl.pallas_call(
        functools.partial(_rwkv_body, n_blk_b=nb, tc=tc, n_sb=nb, n_st=n_st, prompt_mode=True),
        grid=(ns // tc,),
        in_specs=[pl.BlockSpec((nb, tc, RW_IN), lambda j: (0, j, 0))] + [const(w) for w in weights],
        out_specs=(pl.BlockSpec((nb, tc, RW_W), lambda j: (0, j, 0)),
                   pl.BlockSpec((nb, RW_PAIRS, FOX_HD, LANES), lambda j: (0, 0, 0, 0))),
        out_shape=(jax.ShapeDtypeStruct((nb, ns, RW_W), bf16),
                   jax.ShapeDtypeStruct((nb, RW_PAIRS, FOX_HD, LANES), f32)),
        scratch_shapes=[pltpu.VMEM((nb, 8, RW_IN), f32)] + _rwkv_scratch(nb * tc, nb),
        compiler_params=_cparams(("arbitrary",)),
        name="rwkv_prompt",
    )(prw, *weights)


def _rwkv_sample(prw_rows, prev_first_rows, state, weights, *, seq, rows_per_blk):
    n_rows = prw_rows.shape[0]
    nb_blk = rows_per_blk // seq
    const = lambda a: pl.BlockSpec(a.shape, lambda j: (0,) * a.ndim)
    n_sb = UNIT_COLS // seq
    weights = tuple(weights) + _rwkv_selectors(seq)
    return pl.pallas_call(
        functools.partial(_rwkv_body, n_blk_b=nb_blk, tc=seq, n_sb=n_sb, n_st=seq, prompt_mode=False),
        grid=(n_rows // rows_per_blk,),
        in_specs=[pl.BlockSpec((rows_per_blk, RW_IN), lambda j: (j, 0)),
                  pl.BlockSpec((rows_per_blk, RW_IN), lambda j: (j, 0)),
                  pl.BlockSpec((nb_blk, RW_PAIRS, FOX_HD, LANES), lambda j: (j, 0, 0, 0))]
                 + [const(w) for w in weights],
        out_specs=(pl.BlockSpec((rows_per_blk, RW_W), lambda j: (j, 0)),
                   pl.BlockSpec((nb_blk, RW_PAIRS, FOX_HD, LANES), lambda j: (j, 0, 0, 0))),
        out_shape=(jax.ShapeDtypeStruct((n_rows, RW_W), bf16),
                   jax.ShapeDtypeStruct((n_rows // seq, RW_PAIRS, FOX_HD, LANES), f32)),
        scratch_shapes=_rwkv_scratch(rows_per_blk, n_sb),
        compiler_params=_cparams(("arbitrary",)),
        name="rwkv_sample",
    )(prw_rows, prev_first_rows, state, *weights)


def _pack_state(state):
    b = state.shape[0]
    return state.reshape(b, RW_PAIRS, 2, FOX_HD, FOX_HD).transpose(0, 1, 3, 2, 4).reshape(b, RW_PAIRS, FOX_HD, LANES)


def _unpack_state(packed):
    b = packed.shape[0]
    return packed.reshape(b, RW_PAIRS, FOX_HD, 2, FOX_HD).transpose(0, 1, 3, 2, 4).reshape(b, 2 * RW_PAIRS, FOX_HD, FOX_HD)


def _fox_sample_body(pt_ref, *refs, n_pg, n_q):
    del pt_ref
    q_ref, kn_ref, vn_ref, lfn_ref, tri_ref, hm_ref = refs[:6]
    k_refs = refs[6:6 + n_pg]
    v_refs = refs[6 + n_pg:6 + 2 * n_pg]
    lf_refs = refs[6 + 2 * n_pg:6 + 3 * n_pg]
    o_ref, m_s, l_s, acc_s, carry_s = refs[6 + 3 * n_pg:]
    jg = pl.program_id(1)
    n_rows = n_q * FOX_HEADS
    page = tri_ref.shape[0]

    @pl.when(jg == 0)
    def _():
        m_s[...] = jnp.full_like(m_s, NEG_INF)
        l_s[...] = jnp.zeros_like(l_s)
        acc_s[...] = jnp.zeros_like(acc_s)
        carry_s[...] = jnp.zeros_like(carry_s)

    hm = hm_ref[...]
    q4 = q_ref[0]
    qm = jnp.concatenate([jnp.broadcast_to(q4[i:i + 1], (FOX_HEADS, FOX_W)) for i in range(n_q)], axis=0) * hm
    tri = tri_ref[...]

    def cum_bias(lf):
        ct = _split_dot(lf, tri) + carry_s[...]
        carry_s[...] = jnp.broadcast_to(ct[:, page - 1:page], carry_s.shape)
        return jnp.concatenate([ct] * n_q, axis=0)

    def update(s_all, pv):
        m_old = m_s[...]
        m_new = jnp.maximum(m_old, jnp.max(s_all, axis=1, keepdims=True))
        alpha = jnp.exp(m_old - m_new)
        p = jnp.exp(s_all - m_new)
        l_s[...] = l_s[...] * alpha + jnp.sum(p, axis=1, keepdims=True)
        acc_s[...] = acc_s[...] * alpha + pv(p.astype(bf16))
        m_s[...] = m_new

    def pv_pages(p):
        acc = jnp.zeros((n_rows, FOX_W), f32)
        for g in range(n_pg):
            acc = acc + lax.dot_general(p[:, g * page:(g + 1) * page], v_refs[g][0].astype(bf16),
                                        (((1,), (1,)), ((), ())), preferred_element_type=f32)
        return acc

    s_list = []
    for g in range(n_pg):
        s = jnp.dot(qm, k_refs[g][0].astype(bf16), preferred_element_type=f32)
        s_list.append(s - cum_bias(lf_refs[g][0]))
    update(jnp.concatenate(s_list, axis=1), pv_pages)

    @pl.when(jg == pl.num_programs(1) - 1)
    def _():
        pad = jnp.zeros((page - n_q, FOX_W), bf16)
        kn = jnp.concatenate([kn_ref[0], pad], axis=0)
        vn = jnp.concatenate([vn_ref[0], pad], axis=0)
        s = lax.dot_general(qm, kn, (((1,), (1,)), ((), ())), preferred_element_type=f32)
        bias = cum_bias(lfn_ref[0])
        key = lax.broadcasted_iota(jnp.int32, (n_rows, page), 1)
        qpos = lax.broadcasted_iota(jnp.int32, (n_rows, page), 0) // FOX_HEADS
        update(jnp.where(key <= qpos, s - bias, NEG_INF),
               lambda p: jnp.dot(p, vn, preferred_element_type=f32))
        o = acc_s[...] / l_s[...] * hm.astype(f32)
        o_ref[0] = jnp.concatenate(
            [jnp.sum(o[i * FOX_HEADS:(i + 1) * FOX_HEADS], axis=0, keepdims=True) for i in range(n_q)],
            axis=0).astype(o_ref.dtype)


def _fox_sample(page_table, q, kn, vn, lfn_t, cache_kt, cache_vt, cache_lft, *, n_pg):
    nb, n_q, _ = q.shape
    n_pages = page_table.shape[1]
    page = cache_kt.shape[2]
    n_rows = n_q * FOX_HEADS
    tri = (lax.broadcasted_iota(jnp.int32, (page, page), 0)
           <= lax.broadcasted_iota(jnp.int32, (page, page), 1)).astype(bf16)
    hm = (lax.broadcasted_iota(jnp.int32, (n_rows, FOX_W), 0) % FOX_HEADS
          == lax.broadcasted_iota(jnp.int32, (n_rows, FOX_W), 1) // FOX_HD).astype(bf16)
    per_b = lambda shape: pl.BlockSpec((1,) + shape, lambda b, j, pt: (b, 0, 0))
    const = lambda a: pl.BlockSpec(a.shape, lambda b, j, pt: (0, 0))
    paged = lambda shape, g: pl.BlockSpec((1,) + shape, lambda b, j, pt: (pt[b, j * n_pg + g], 0, 0))
    in_specs = ([per_b((n_q, FOX_W))] * 3 + [per_b((FOX_HEADS, page)), const(tri), const(hm)]
                + [paged((FOX_W, page), g) for g in range(n_pg)]
                + [paged((FOX_W, page), g) for g in range(n_pg)]
                + [paged((FOX_HEADS, page), g) for g in range(n_pg)])
    grid_spec = pltpu.PrefetchScalarGridSpec(
        num_scalar_prefetch=1, grid=(nb, n_pages // n_pg), in_specs=in_specs,
        out_specs=pl.BlockSpec((1, n_q, FOX_W), lambda b, j, pt: (b, 0, 0)),
        scratch_shapes=[pltpu.VMEM((n_rows, 1), f32), pltpu.VMEM((n_rows, 1), f32),
                        pltpu.VMEM((n_rows, FOX_W), f32), pltpu.VMEM((FOX_HEADS, page), f32)])
    return pl.pallas_call(
        functools.partial(_fox_sample_body, n_pg=n_pg, n_q=n_q),
        grid_spec=grid_spec,
        out_shape=jax.ShapeDtypeStruct((nb, n_q, FOX_W), bf16),
        compiler_params=_cparams(("arbitrary", "arbitrary")),
        name="fox_sample",
    )(page_table, q, kn, vn, lfn_t, tri, hm, *([cache_kt] * n_pg), *([cache_vt] * n_pg), *([cache_lft] * n_pg))


def _mem_kv_body(m_ref, g_ref, w_ref, knw_ref, k_ref, v_ref):
    n = _rms(m_ref[0], g_ref[...]).astype(bf16)
    kv = jnp.dot(n, w_ref[...], preferred_element_type=f32)
    for h in range(MEM_HEADS):
        sl = slice(h * MEM_HD, (h + 1) * MEM_HD)
        k_ref[0, :, sl] = _rms(kv[:, sl], knw_ref[...])
    v_ref[0] = kv[:, MEM_W:]


def _mem_kv(mem, mem_norm_w, w_mem_kv, mem_kn_w):
    nb, nm, _ = mem.shape
    const = lambda shape: pl.BlockSpec(shape, lambda b: (0,) * len(shape))
    out = jax.ShapeDtypeStruct((nb, nm, MEM_W), f32)
    return pl.pallas_call(
        _mem_kv_body,
        grid=(nb,),
        in_specs=[pl.BlockSpec((1, nm, D_MODEL), lambda b: (b, 0, 0)), const((1, D_MODEL)),
                  const((D_MODEL, 2 * MEM_W)), const((1, MEM_HD))],
        out_specs=(pl.BlockSpec((1, nm, MEM_W), lambda b: (b, 0, 0)),) * 2,
        out_shape=(out, out),
        compiler_params=_cparams(("arbitrary",)),
        name="mem_kv",
    )(mem, mem_norm_w.reshape(1, D_MODEL), w_mem_kv.astype(bf16), mem_kn_w.reshape(1, MEM_HD))


def _mem_attend_body(q_ref, k_ref, v_ref, o_ref):
    q = q_ref[0]
    for h in range(MEM_HEADS):
        sl = slice(h * MEM_HD, (h + 1) * MEM_HD)
        k = k_ref[0, :, sl].astype(bf16)
        v = v_ref[0, :, sl].astype(bf16)
        s = lax.dot_general(q[:, sl], k, (((1,), (1,)), ((), ())), preferred_element_type=f32) * MEM_SCALE
        e = jnp.exp(s - jnp.max(s, axis=1, keepdims=True))
        p = e / jnp.sum(e, axis=1, keepdims=True)
        o_ref[0, :, sl] = jnp.dot(p.astype(bf16), v, preferred_element_type=f32).astype(o_ref.dtype)


def _mem_attend_rows_body(q_ref, k_ref, v_ref, o_ref):
    n_mem = k_ref.shape[1] // MEM_HEADS
    for i in range(q_ref.shape[0]):
        q = q_ref[i]
        for h in range(MEM_HEADS):
            sl = slice(h * MEM_HD, (h + 1) * MEM_HD)
            k = k_ref[i, pl.ds(h, n_mem, stride=MEM_HEADS), :].astype(bf16)
            v = v_ref[i, pl.ds(h, n_mem, stride=MEM_HEADS), :].astype(bf16)
            s = lax.dot_general(q[:, sl], k, (((1,), (1,)), ((), ())), preferred_element_type=f32) * MEM_SCALE
            e = jnp.exp(s - jnp.max(s, axis=1, keepdims=True))
            p = e / jnp.sum(e, axis=1, keepdims=True)
            o_ref[i, :, sl] = jnp.dot(p.astype(bf16), v, preferred_element_type=f32).astype(o_ref.dtype)


def _mem_attend_rows(q, mk_rows, mv_rows, *, nb_blk):
    nb, ns, _ = q.shape
    rows = mk_rows.shape[1]
    kv = pl.BlockSpec((nb_blk, rows, MEM_HD), lambda b: (b, 0, 0))
    qo = pl.BlockSpec((nb_blk, ns, MEM_W), lambda b: (b, 0, 0))
    return pl.pallas_call(
        _mem_attend_rows_body,
        grid=(nb // nb_blk,),
        in_specs=[qo, kv, kv],
        out_specs=qo,
        out_shape=jax.ShapeDtypeStruct((nb, ns, MEM_W), bf16),
        compiler_params=_cparams(("arbitrary",)),
        name="mem_attend_rows",
    )(q, mk_rows, mv_rows)


def _mem_attend(q, mk, mv, *, tq):
    nb, ns, _ = q.shape
    nm = mk.shape[1]
    kv = pl.BlockSpec((1, nm, MEM_W), lambda b, j: (b, 0, 0))
    return pl.pallas_call(
        _mem_attend_body,
        grid=(nb, ns // tq),
        in_specs=[pl.BlockSpec((1, tq, MEM_W), lambda b, j: (b, j, 0)), kv, kv],
        out_specs=pl.BlockSpec((1, tq, MEM_W), lambda b, j: (b, j, 0)),
        out_shape=jax.ShapeDtypeStruct((nb, ns, MEM_W), bf16),
        compiler_params=_cparams(("arbitrary", "arbitrary")),
        name="mem_attend",
    )(q, mk, mv)


def _finish_body(x_ref, of_ref, or_ref, om_ref, g1_ref, wg_ref, wb_ref, wo_ref, g2_ref, h_ref, xn2_ref):
    x = x_ref[...]
    xn = _rms(x, g1_ref[...]).astype(bf16)
    merged = jnp.zeros_like(x)
    for n, br in enumerate((of_ref, or_ref, om_ref)):
        gate = _sigmoid(jnp.dot(xn, wg_ref[:, n * D_MODEL:(n + 1) * D_MODEL], preferred_element_type=f32))
        merged = merged + gate * jnp.dot(br[...], wb_ref[n], preferred_element_type=f32)
    h = x + jnp.dot(merged.astype(bf16), wo_ref[...], preferred_element_type=f32)
    h_ref[...] = h
    xn2_ref[...] = _rms(h, g2_ref[...]).astype(bf16)


def _finish(x, o_fox, o_rw, o_mem, norm1_w, w_gate, w_branch, w_out, norm2_w, *, tile):
    n = x.shape[0]
    const = lambda shape: pl.BlockSpec(shape, lambda j: (0,) * len(shape))
    tok = lambda w: pl.BlockSpec((tile, w), lambda j: (j, 0))
    return pl.pallas_call(
        _finish_body,
        grid=(n // tile,),
        in_specs=[tok(D_MODEL), tok(FOX_W), tok(RW_W), tok(MEM_W), const((1, D_MODEL)),
                  const((D_MODEL, 3 * D_MODEL)), const((3, FOX_W, D_MODEL)), const((D_MODEL, D_MODEL)),
                  const((1, D_MODEL))],
        out_specs=(tok(D_MODEL), tok(D_MODEL)),
        out_shape=(jax.ShapeDtypeStruct((n, D_MODEL), f32), jax.ShapeDtypeStruct((n, D_MODEL), bf16)),
        compiler_params=_cparams(("arbitrary",)),
        name="finish",
    )(x, o_fox, o_rw, o_mem, norm1_w.reshape(1, D_MODEL), w_gate, w_branch.astype(bf16), w_out.astype(bf16),
      norm2_w.reshape(1, D_MODEL))


_PEER_SLOTS = [(ra, rb) for ra in range(PEER_TOPK) for rb in range(PEER_TOPK) if (ra + 1) * (rb + 1) <= PEER_TOPK]
_PEER_SLOT_ROWS = -(-len(_PEER_SLOTS) // 8) * 8


def _top_rows(s, n_take):
    iota = lax.broadcasted_iota(jnp.int32, s.shape, 0).astype(f32)
    big = float(s.shape[0])
    vals, idxs = [], []
    for _ in range(n_take):
        m = jnp.max(s, axis=0, keepdims=True)
        idx = jnp.min(jnp.where(s == m, iota, big), axis=0, keepdims=True)
        vals.append(m)
        idxs.append(idx)
        s = jnp.where(iota == idx, -jnp.inf, s)
    return vals, idxs


def _route_body(x_ref, wq_ref, k1_ref, k2_ref, a_ref, b_ref, g_ref,
                q_s, cand_s, ea_s, eb_s, a_s, b_s, g_s, at_s, bt_s, gt_s):
    q_s[...] = lax.dot_general(wq_ref[...], x_ref[...], (((1,), (1,)), ((), ())),
                               preferred_element_type=f32).astype(bf16)
    cand_s[...] = jnp.full_like(cand_s, -jnp.inf)
    ea_s[...] = jnp.zeros_like(ea_s)
    eb_s[...] = jnp.zeros_like(eb_s)

    def head(h, _):
        base = pl.multiple_of(h * 2 * PEER_HALF, 2 * PEER_HALF)
        s1 = jnp.dot(k1_ref[h], q_s[pl.ds(base, PEER_HALF), :], preferred_element_type=f32)
        s2 = jnp.dot(k2_ref[h], q_s[pl.ds(base + PEER_HALF, PEER_HALF), :], preferred_element_type=f32)
        v1, i1 = _top_rows(s1, PEER_TOPK)
        v2, i2 = _top_rows(s2, PEER_TOPK)
        for slot, (ra, rb) in enumerate(_PEER_SLOTS):
            cand_s[slot:slot + 1, :] = v1[ra] + v2[rb]
            ea_s[slot:slot + 1, :] = i1[ra]
            eb_s[slot:slot + 1, :] = i2[rb]
        cand = cand_s[...]
        ea = ea_s[...]
        eb = eb_s[...]
        iota = lax.broadcasted_iota(jnp.int32, cand.shape, 0).astype(f32)
        scs = []
        for r in range(PEER_TOPK):
            m = jnp.max(cand, axis=0, keepdims=True)
            slot = jnp.min(jnp.where(cand == m, iota, float(_PEER_SLOT_ROWS)), axis=0, keepdims=True)
            hit = iota == slot
            a_s[r:r + 1, :] = jnp.sum(jnp.where(hit, ea, 0.0), axis=0, keepdims=True)
            b_s[r:r + 1, :] = jnp.sum(jnp.where(hit, eb, 0.0), axis=0, keepdims=True)
            cand = jnp.where(hit, -jnp.inf, cand)
            scs.append(m)
        es = [jnp.exp(sc - scs[0]) for sc in scs]
        z = es[0]
        for e in es[1:]:
            z = z + e
        for r in range(PEER_TOPK):
            g_s[r:r + 1, :] = es[r] / z
        rows = pl.ds(pl.multiple_of(h * PEER_TOPK, PEER_TOPK), PEER_TOPK)
        at_s[rows, :] = a_s[...]
        bt_s[rows, :] = b_s[...]
        gt_s[rows, :] = g_s[...]
        return 0

    lax.fori_loop(0, PEER_HEADS, head, 0)
    a_ref[...] = jnp.transpose(at_s[...])
    b_ref[...] = jnp.transpose(bt_s[...])
    g_ref[...] = jnp.transpose(gt_s[...])


def _route(xn2, wq_t, keys1, keys2, *, tile):
    n = xn2.shape[0]
    n_slots = PEER_HEADS * PEER_TOPK
    const = lambda shape: pl.BlockSpec(shape, lambda j: (0,) * len(shape))
    out = jax.ShapeDtypeStruct((n, n_slots), f32)
    ospec = pl.BlockSpec((tile, n_slots), lambda j: (j, 0))
    cand = lambda: pltpu.VMEM((_PEER_SLOT_ROWS, tile), f32)
    top = lambda: pltpu.VMEM((PEER_TOPK, tile), f32)
    full = lambda: pltpu.VMEM((n_slots, tile), f32)
    return pl.pallas_call(
        _route_body,
        grid=(n // tile,),
        in_specs=[pl.BlockSpec((tile, D_MODEL), lambda j: (j, 0)), const(wq_t.shape), const(keys1.shape),
                  const(keys2.shape)],
        out_specs=(ospec, ospec, ospec),
        out_shape=(out, out, out),
        scratch_shapes=[pltpu.VMEM((wq_t.shape[0], tile), bf16), cand(), cand(), cand(), top(), top(), top(),
                        full(), full(), full()],
        compiler_params=_cparams(("arbitrary",)),
        name="peer_route",
    )(xn2, wq_t, keys1, keys2)


G_PITCH = PEER_NK + 8
EXPERT_SUB = 2048


def _gelu_tanh(x):
    return 0.5 * x * (1.0 + jnp.tanh(math.sqrt(2.0 / math.pi) * (x + 0.044715 * (x * x * x))))


def _peer_body(x_ref, a_ref, b_ref, g_ref, h_ref, u_ref, v_ref, y_ref, gs_ref, acc_ref):
    tile = x_ref.shape[0]
    c = pl.program_id(1)
    n_exp = u_ref.shape[0]

    @pl.when(c == 0)
    def _():
        acc_ref[...] = jnp.zeros_like(acc_ref)
        sub = lax.broadcasted_iota(jnp.int32, (PEER_NK, LANES), 0).astype(f32)

        def tok8(t8, _):
            rows = pl.ds(pl.multiple_of(t8 * 8, 8), 8)
            a8, b8, g8 = a_ref[rows, :], b_ref[rows, :], g_ref[rows, :]
            for j in range(8):
                hit_a = sub == a8[j:j + 1]
                gate = jnp.where(hit_a, g8[j:j + 1], 0.0)
                g_hi = gate.astype(bf16)
                g_lo = (gate - g_hi.astype(f32)).astype(bf16)
                bt = jnp.where(sub == b8[j:j + 1], 1.0, 0.0).astype(bf16)
                gt = lax.dot_general(jnp.concatenate([g_hi, g_lo], axis=1), jnp.concatenate([bt, bt], axis=1),
                                     (((1,), (1,)), ((), ())), preferred_element_type=f32)
                gs_ref[pl.ds(pl.multiple_of((t8 * 8 + j) * G_PITCH, 8), PEER_NK), :] = gt
            return 0

        lax.fori_loop(0, tile // 8, tok8, 0, unroll=4)

    x = x_ref[...]
    acc = acc_ref[...]
    for s in range(n_exp // EXPERT_SUB):
        e0 = s * EXPERT_SUB
        act = lax.dot_general(x, u_ref[e0:e0 + EXPERT_SUB, :], (((1,), (1,)), ((), ())),
                              preferred_element_type=f32)
        i1 = (c * n_exp + e0) // PEER_NK
        gates = jnp.concatenate(
            [gs_ref[pl.ds(i1 + k, tile, stride=G_PITCH), :] for k in range(EXPERT_SUB // PEER_NK)], axis=1)
        w = (gates * _gelu_tanh(act)).astype(bf16)
        acc = acc + jnp.dot(w, v_ref[e0:e0 + EXPERT_SUB, :], preferred_element_type=f32)
    acc_ref[...] = acc

    @pl.when(c == pl.num_programs(1) - 1)
    def _():
        y_ref[...] = h_ref[...] + acc


def _peer(xn2, a, b, g, h, u_bf, v_bf, *, tile, chunk):
    n = xn2.shape[0]
    n_slots = a.shape[1]
    n_experts = u_bf.shape[0]
    tok = lambda w: pl.BlockSpec((tile, w), lambda j, c: (j, 0))
    tab = pl.BlockSpec((chunk, D_MODEL), lambda j, c: (c, 0))
    return pl.pallas_call(
        _peer_body,
        grid=(n // tile, n_experts // chunk),
        in_specs=[tok(D_MODEL), tok(n_slots), tok(n_slots), tok(n_slots), tok(D_MODEL), tab, tab],
        out_specs=tok(D_MODEL),
        out_shape=jax.ShapeDtypeStruct((n, D_MODEL), f32),
        scratch_shapes=[pltpu.VMEM((tile * G_PITCH, PEER_NK), f32), pltpu.VMEM((tile, D_MODEL), f32)],
        compiler_params=_cparams(("arbitrary", "arbitrary")),
        name="peer_experts",
    )(xn2, a, b, g, h, u_bf, v_bf)


TOK_TILE = 256
ROUTE_TILE = 1024
ATT_BLK = 1024
RW_CHUNK = 128
RW_SAMPLE_ROWS = 128
PAGES_PER_STEP = 32
EXPERT_CHUNK = 2048
MEM_SAMPLE_BATCHES = 4


def _merge_and_peer(x2, o_fox, o_rw, o_mem, shared):
    h, xn2 = _finish(x2, o_fox, o_rw, o_mem, shared["norm1_w"], shared["w_gate"], shared["w_branch"],
                     shared["w_out"], shared["norm2_w"], tile=TOK_TILE)
    a, b, g = _route(xn2, shared["wq_t"], shared["keys1"], shared["keys2"], tile=min(ROUTE_TILE, xn2.shape[0]))
    return _peer(xn2, a, b, g, h, shared["u_bf"], shared["v_bf"], tile=TOK_TILE, chunk=EXPERT_CHUNK)


def kernel(x_prompt, x_sample, cache_fox_k, cache_fox_v, cache_fox_logf, cache_mem_k, cache_mem_v, state_rwkv, state_rwkv_shift, page_table, mem_prompt, norm1_w, w_in, fox_b_f, fox_qn_w, fox_kn_w, rw_mu, rw_w0, rw_w_up, rw_a0, rw_a_up, rw_g_up, rw_k_k, rw_k_a, rw_r_k, rw_ln_w, rw_ln_b, mem_norm_w, w_mem_kv, mem_qn_w, mem_kn_w, w_branch, w_out, norm2_w, peer_w_q, peer_keys1, peer_keys2, peer_u, peer_v):
    bp, sp, _ = x_prompt.shape
    db, ds, _ = x_sample.shape
    w_pack, bf_pad, qnw, knw, mqw, w_gate = _prep_inproj_weights(w_in, fox_b_f, fox_qn_w, fox_kn_w, mem_qn_w)
    g1 = norm1_w.reshape(1, D_MODEL)
    rw_w = _rwkv_weights(rw_mu, rw_w0, rw_w_up, rw_a0, rw_a_up, rw_g_up, rw_k_k, rw_k_a, rw_r_k, rw_ln_w, rw_ln_b)
    shared = dict(norm1_w=norm1_w, w_gate=w_gate, w_branch=w_branch, w_out=w_out, norm2_w=norm2_w,
                  wq_t=peer_w_q.T.astype(bf16), keys1=peer_keys1.astype(bf16), keys2=peer_keys2.astype(bf16),
                  u_bf=peer_u.astype(bf16), v_bf=peer_v.astype(bf16))

    qh, kh, vh, k_p, v_p, logf_p, prw_p, mq_p = _inproj(
        x_prompt, g1, w_pack, bf_pad, qnw, knw, mqw, head_major=True, tile=TOK_TILE)
    o_fox = _fox_prompt(qh, kh, vh, blk=ATT_BLK)
    o_rw, st_p = _rwkv_prompt(prw_p, rw_w, tc=RW_CHUNK)
    mem_k_p, mem_v_p = _mem_kv(mem_prompt, mem_norm_w, w_mem_kv, mem_kn_w)
    o_mem = _mem_attend(mq_p, mem_k_p, mem_v_p, tq=ATT_BLK)
    n_p = bp * sp
    y_prompt = _merge_and_peer(x_prompt.reshape(n_p, D_MODEL), o_fox.reshape(n_p, FOX_W), o_rw.reshape(n_p, RW_W),
                               o_mem.reshape(n_p, MEM_W), shared).reshape(bp, sp, D_MODEL)

    n_s = db * ds
    qt, kt, vt, k_s, v_s, logf_s, prw_s, mq_s = _inproj(
        x_sample.reshape(1, n_s, D_MODEL), g1, w_pack, bf_pad, qnw, knw, mqw, head_major=False, tile=TOK_TILE)
    pool, page = cache_fox_k.shape[:2]
    logf_s = logf_s.reshape(db, ds, LANES)[:, :, :FOX_HEADS]
    lfn_t = jnp.pad(jnp.swapaxes(logf_s, 1, 2), ((0, 0), (0, 0), (0, page - ds)))
    pos_minor = lambda a: jnp.transpose(a, (0, 2, 3, 1)).reshape(pool, FOX_W, page)
    o_fox_s = _fox_sample(page_table, qt.reshape(db, ds, FOX_W), kt.reshape(db, ds, FOX_W), vt.reshape(db, ds, FOX_W),
                          lfn_t, pos_minor(cache_fox_k), pos_minor(cache_fox_v), jnp.swapaxes(cache_fox_logf, 1, 2),
                          n_pg=PAGES_PER_STEP)
    prw_rows = prw_s.reshape(n_s, RW_IN)
    o_rw_s, st_s = _rwkv_sample(prw_rows, jnp.repeat(state_rwkv_shift, ds, axis=0), _pack_state(state_rwkv), rw_w,
                                seq=ds, rows_per_blk=RW_SAMPLE_ROWS)
    nm = cache_mem_k.shape[1]
    o_mem_s = _mem_attend_rows(mq_s.reshape(db, ds, MEM_W), cache_mem_k.reshape(db, nm * MEM_HEADS, MEM_HD),
                               cache_mem_v.reshape(db, nm * MEM_HEADS, MEM_HD), nb_blk=MEM_SAMPLE_BATCHES)
    y_sample = _merge_and_peer(x_sample.reshape(n_s, D_MODEL), o_fox_s.reshape(n_s, FOX_W), o_rw_s,
                               o_mem_s.reshape(n_s, MEM_W), shared).reshape(db, ds, D_MODEL)

    heads = lambda a, b, s: a.reshape(b, s, FOX_HEADS, FOX_HD)
    return (y_prompt, y_sample,
            heads(k_p, bp, sp), heads(v_p, bp, sp), logf_p[:, :, :FOX_HEADS],
            _unpack_state(st_p), prw_p[:, -1],
            mem_k_p.reshape(bp, nm, MEM_HEADS, MEM_HD), mem_v_p.reshape(bp, nm, MEM_HEADS, MEM_HD),
            heads(k_s, db, ds), heads(v_s, db, ds), logf_s,
            _unpack_state(st_s), prw_rows.reshape(db, ds, RW_IN)[:, -1])
```
